```python
import jax, jax.numpy as jnp
from jax import lax
import numpy as np

D_MODEL = 4096
BATCH = 4
SEQ = 2048
DEPTH = 1
DEC_BATCH = 128
DEC_SEQ = 8
PAST_LEN = 16384
PAGE_SIZE = 128

G_RWKV = D_MODEL // 2
G_CONV = D_MODEL - G_RWKV
HEAD_DIM = 64
N_HEADS = G_RWKV // HEAD_DIM
CONV_W = 3
D_DECAY_LORA = max(32, int(round(1.8 * D_MODEL ** 0.5 / 32)) * 32)
D_AAA_LORA = max(32, int(round(1.8 * D_MODEL ** 0.5 / 32)) * 32)
D_GATE_LORA = max(32, int(round(0.6 * D_MODEL ** 0.8 / 32)) * 32)
D_FF = 11008
PROJ_COLS = 3 * G_RWKV + 3 * G_CONV
RMS_EPS = 1e-6
GN_EPS = 64e-5

kernel_name = "hybrid_rwkv7_shortconv_convffn_step"


def rmsnorm(x, g):
    xf = x.astype(jnp.float32)
    y = xf * lax.rsqrt(jnp.mean(xf * xf, axis=-1, keepdims=True) + RMS_EPS)
    return (y * g.astype(jnp.float32)).astype(x.dtype)


def causal_dwconv(u, buf, w):
    t = u.shape[1]
    upad = jnp.concatenate([buf.astype(u.dtype), u], axis=1)
    out = upad[:, 0:t] * w[0]
    for j in range(1, CONV_W):
        out = out + upad[:, j:j + t] * w[j]
    return out, upad[:, -(CONV_W - 1):]


def wkv7_scan(r, decay, k, v, kk, a, s0):
    def step(s, inp):
        r_t, w_t, k_t, v_t, kk_t, a_t = inp
        sa = jnp.einsum('bhvk,bhk->bhv', s, -kk_t)
        s = s * w_t[:, :, None, :] + sa[..., None] * (kk_t * a_t)[:, :, None, :] \
            + v_t[..., :, None] * k_t[..., None, :]
        return s, jnp.einsum('bhvk,bhk->bhv', s, r_t)
    xs = tuple(jnp.swapaxes(z, 0, 1) for z in (r, decay, k, v, kk, a))
    s_fin, o = lax.scan(step, s0, xs)
    return jnp.swapaxes(o, 0, 1), s_fin


def hybrid_layer(x, shift_prev, wkv_prev, conv_prev, ffn_prev, p):
    bsz, t, _ = x.shape
    f32 = jnp.float32
    xn = rmsnorm(x, p['norm1_g'])
    xprev = jnp.concatenate([shift_prev[:, None].astype(x.dtype), xn[:, :-1]], axis=1)
    dx = xprev - xn
    proj = xn @ p['w_in']
    p_rkv = proj[..., :3 * G_RWKV]
    p0 = shift_prev.astype(x.dtype) @ p['w_in'][:, :3 * G_RWKV]
    p_prev = jnp.concatenate([p0[:, None], p_rkv[:, :-1]], axis=1)
    rkv = p_rkv + p['mu_rkv'] * (p_prev - p_rkv)
    r, k, v = jnp.split(rkv, 3, axis=-1)
    xw = xn + dx * p['mu_lora'][0]
    xa = xn + dx * p['mu_lora'][1]
    xg = xn + dx * p['mu_lora'][2]
    wlog = -jax.nn.softplus(-(p['decay_w0'] + jnp.tanh(xw @ p['decay_w1']) @ p['decay_w2'])) - 0.5
    decay = jnp.exp(-jnp.exp(wlog.astype(f32)))
    a = jax.nn.sigmoid(p['aaa_a0'] + (xa @ p['aaa_a1']) @ p['aaa_a2'])
    g = jax.nn.sigmoid(xg @ p['gate_g1']) @ p['gate_g2']
    hs = (bsz, t, N_HEADS, HEAD_DIM)
    r = r.astype(f32).reshape(hs)
    v = v.astype(f32).reshape(hs)
    a = a.astype(f32).reshape(hs)
    k = k.astype(f32).reshape(hs)
    kk = k * p['k_k'].astype(f32).reshape(N_HEADS, HEAD_DIM)
    kk = kk / jnp.maximum(jnp.sqrt(jnp.sum(kk * kk, axis=-1, keepdims=True)), 1e-12)
    k = k * (1.0 + (a - 1.0) * p['k_a'].astype(f32).reshape(N_HEADS, HEAD_DIM))
    decay = decay.reshape(hs)
    o, wkv_new = wkv7_scan(r, decay, k, v, kk, a, wkv_prev.astype(f32))
    mu = jnp.mean(o, axis=-1, keepdims=True)
    var = jnp.mean(jnp.square(o - mu), axis=-1, keepdims=True)
    on = ((o - mu) * lax.rsqrt(var + GN_EPS)).reshape(bsz, t, G_RWKV)
    on = on * p['ln_x_w'].astype(f32) + p['ln_x_b'].astype(f32)
    bonus = jnp.sum(r * k * p['r_k'].astype(f32), axis=-1, keepdims=True) * v
    o_rwkv = ((on + bonus.reshape(bsz, t, G_RWKV)) * g.astype(f32)).astype(x.dtype)
    gb = proj[..., 3 * G_RWKV:3 * G_RWKV + G_CONV]
    gc = proj[..., 3 * G_RWKV + G_CONV:3 * G_RWKV + 2 * G_CONV]
    xin = proj[..., 3 * G_RWKV + 2 * G_CONV:]
    hconv, conv_new = causal_dwconv(gc * xin, conv_prev, p['conv_w'])
    o_conv = rmsnorm(gb * hconv, p['conv_norm_g'])
    x = x + jnp.concatenate([o_rwkv, o_conv], axis=-1) @ p['w_out']
    hn = rmsnorm(x, p['norm2_g'])
    u = hn @ p['ffn_w1']
    uc, ffn_new = causal_dwconv(u, ffn_prev, p['ffn_conv_w'])
    x = x + (jax.nn.silu(uc + p['ffn_conv_b']) * (hn @ p['ffn_w3'])) @ p['ffn_w2']
    return x, xn[:, -1], wkv_new.astype(x.dtype), conv_new, ffn_new


def setup_inputs(seed: int = 0) -> dict:
    key = jax.random.key(seed)
    ks = iter(jax.random.split(key, 40))
    nrm = lambda shape, s=1.0: jax.random.normal(next(ks), shape, jnp.float32) * s
    L = DEPTH
    return {
        "x_prompt": nrm((BATCH, SEQ, D_MODEL)),
        "x_sample": nrm((DEC_BATCH, DEC_SEQ, D_MODEL)),
        "state_shift": nrm((L, DEC_BATCH, D_MODEL)),
        "state_wkv": nrm((L, DEC_BATCH, N_HEADS, HEAD_DIM, HEAD_DIM), 0.5),
        "state_conv": nrm((L, DEC_BATCH, CONV_W - 1, G_CONV)),
        "state_ffn": nrm((L, DEC_BATCH, CONV_W - 1, D_FF)),
        "norm1_g": 1.0 + nrm((L, D_MODEL), 0.02),
        "w_in": nrm((L, D_MODEL, PROJ_COLS), D_MODEL ** -0.5),
        "mu_rkv": jax.random.uniform(next(ks), (L, 3 * G_RWKV), jnp.float32),
        "mu_lora": jax.random.uniform(next(ks), (L, 3, D_MODEL), jnp.float32),
        "decay_w0": jax.random.uniform(next(ks), (L, G_RWKV), jnp.float32, -6.0, -1.0),
        "decay_w1": nrm((L, D_MODEL, D_DECAY_LORA), D_MODEL ** -0.5),
        "decay_w2": nrm((L, D_DECAY_LORA, G_RWKV), 0.5 * D_DECAY_LORA ** -0.5),
        "aaa_a0": nrm((L, G_RWKV), 0.1),
        "aaa_a1": nrm((L, D_MODEL, D_AAA_LORA), D_MODEL ** -0.5),
        "aaa_a2": nrm((L, D_AAA_LORA, G_RWKV), 0.5 * D_AAA_LORA ** -0.5),
        "gate_g1": nrm((L, D_MODEL, D_GATE_LORA), D_MODEL ** -0.5),
        "gate_g2": nrm((L, D_GATE_LORA, G_RWKV), D_GATE_LORA ** -0.5),
        "k_k": 0.85 + nrm((L, G_RWKV), 0.02),
        "k_a": 1.0 + nrm((L, G_RWKV), 0.02),
        "r_k": nrm((L, N_HEADS, HEAD_DIM), 0.1),
        "ln_x_w": 1.0 + nrm((L, G_RWKV), 0.02),
        "ln_x_b": nrm((L, G_RWKV), 0.01),
        "conv_w": nrm((L, CONV_W, G_CONV), CONV_W ** -0.5),
        "conv_norm_g": 1.0 + nrm((L, G_CONV), 0.02),
        "w_out": nrm((L, D_MODEL, D_MODEL), D_MODEL ** -0.5),
        "norm2_g": 1.0 + nrm((L, D_MODEL), 0.02),
        "ffn_w1": nrm((L, D_MODEL, D_FF), D_MODEL ** -0.5),
        "ffn_conv_w": nrm((L, CONV_W, D_FF), CONV_W ** -0.5),
        "ffn_conv_b": nrm((L, D_FF), 0.01),
        "ffn_w3": nrm((L, D_MODEL, D_FF), D_MODEL ** -0.5),
        "ffn_w2": nrm((L, D_FF, D_MODEL), D_FF ** -0.5),
        "final_norm_g": 1.0 + nrm((D_MODEL,), 0.02),
    }


def reference(x_prompt, x_sample, state_shift, state_wkv, state_conv, state_ffn,
              norm1_g, w_in, mu_rkv, mu_lora, decay_w0, decay_w1, decay_w2,
              aaa_a0, aaa_a1, aaa_a2, gate_g1, gate_g2, k_k, k_a, r_k, ln_x_w, ln_x_b,
              conv_w, conv_norm_g, w_out, norm2_g, ffn_w1, ffn_conv_w, ffn_conv_b,
              ffn_w3, ffn_w2, final_norm_g):
    bp = x_prompt.shape[0]
    dt = x_prompt.dtype
    yp, ys = x_prompt, x_sample
    ps_shift, ps_wkv, ps_conv, ps_ffn = [], [], [], []
    ss_shift, ss_wkv, ss_conv, ss_ffn = [], [], [], []
    for l in range(DEPTH):
        p = dict(norm1_g=norm1_g[l], w_in=w_in[l], mu_rkv=mu_rkv[l], mu_lora=mu_lora[l],
                 decay_w0=decay_w0[l], decay_w1=decay_w1[l], decay_w2=decay_w2[l],
                 aaa_a0=aaa_a0[l], aaa_a1=aaa_a1[l], aaa_a2=aaa_a2[l],
                 gate_g1=gate_g1[l], gate_g2=gate_g2[l], k_k=k_k[l], k_a=k_a[l], r_k=r_k[l],
                 ln_x_w=ln_x_w[l], ln_x_b=ln_x_b[l], conv_w=conv_w[l], conv_norm_g=conv_norm_g[l],
                 w_out=w_out[l], norm2_g=norm2_g[l], ffn_w1=ffn_w1[l], ffn_conv_w=ffn_conv_w[l],
                 ffn_conv_b=ffn_conv_b[l], ffn_w3=ffn_w3[l], ffn_w2=ffn_w2[l])
        yp, a1, a2, a3, a4 = hybrid_layer(
            yp, jnp.zeros((bp, D_MODEL), dt), jnp.zeros((bp, N_HEADS, HEAD_DIM, HEAD_DIM), jnp.float32),
            jnp.zeros((bp, CONV_W - 1, G_CONV), dt), jnp.zeros((bp, CONV_W - 1, D_FF), dt), p)
        ps_shift.append(a1); ps_wkv.append(a2); ps_conv.append(a3); ps_ffn.append(a4)
        ys, b1, b2, b3, b4 = hybrid_layer(ys, state_shift[l], state_wkv[l], state_conv[l], state_ffn[l], p)
        ss_shift.append(b1); ss_wkv.append(b2); ss_conv.append(b3); ss_ffn.append(b4)
    y_prompt = rmsnorm(yp, final_norm_g)
    y_sample = rmsnorm(ys, final_norm_g)
    return (y_prompt, y_sample,
            jnp.stack(ps_shift), jnp.stack(ps_wkv), jnp.stack(ps_conv), jnp.stack(ps_ffn),
            jnp.stack(ss_shift), jnp.stack(ss_wkv), jnp.stack(ss_conv), jnp.stack(ss_ffn))
```

```python
import functools
import math

import jax
import jax.numpy as jnp
from jax import lax
from jax.experimental import pallas as pl
from jax.experimental.pallas import tpu as pltpu

F32 = jnp.float32
BF16 = jnp.bfloat16

HEAD_DIM = 64
LANES = 128
SUBLANES = 8
RMS_EPS = 1e-6
GN_EPS = 64e-5
VMEM_LIMIT_BYTES = 56 * 1024 * 1024


def _params(*sem):
    return pltpu.CompilerParams(dimension_semantics=sem, vmem_limit_bytes=VMEM_LIMIT_BYTES)


def _dot(a, b):
    return jnp.dot(a, b, preferred_element_type=F32)


def _dot_nt(a, b):
    return lax.dot_general(a, b, (((1,), (1,)), ((), ())), preferred_element_type=F32)


def _dot_tn(a, b):
    return lax.dot_general(a, b, (((0,), (0,)), ((), ())), preferred_element_type=F32)


def _sigmoid(z):
    return 1.0 / (1.0 + jnp.exp(-z))


def _split_dot(x, b_exact, terms, dot=_dot):
    acc = None
    rem = x
    for _ in range(terms):
        hi = rem.astype(BF16)
        part = dot(hi, b_exact)
        acc = part if acc is None else acc + part
        rem = rem - hi.astype(F32)
    return acc


def _row_iota(shape):
    return lax.broadcasted_iota(jnp.int32, shape, 0)


def _prev_rows(p, first1, first2, *, sample):
    row = _row_iota(p.shape)
    r1 = pltpu.roll(p, 1, 0)
    r2 = pltpu.roll(p, 2, 0)
    if sample:
        t = row % SUBLANES
        return jnp.where(t == 0, first1, r1), jnp.where(t < 2, first2, r2)
    prev1 = jnp.where(row == 0, first1, r1)
    prev2 = jnp.where(row == 0, first2, jnp.where(row == 1, first1, r2))
    return prev1, prev2


def _carry_rows(carry_ref, j, i, tiles_per_seq):
    c = carry_ref[j]
    keep = (i % tiles_per_seq) != 0
    first1 = jnp.where(keep, c[SUBLANES - 1:SUBLANES, :], 0.0)
    first2 = jnp.where(keep, c[SUBLANES - 2:SUBLANES - 1, :], 0.0)
    return first1, first2


def _mix_kernel(*refs, sample, tiles_per_seq):
    if sample:
        (x_ref, ext_ref, g1_ref, mu_ref, w0_ref, a0_ref, dw1_ref, aw1_ref, gw1_ref, dw2_ref,
         aw2_ref, gw2_ref, xnb_ref, lw_ref, a_ref, g_ref, last_ref) = refs
    else:
        (x_ref, g1_ref, mu_ref, w0_ref, a0_ref, dw1_ref, aw1_ref, gw1_ref, dw2_ref,
         aw2_ref, gw2_ref, xnb_ref, lw_ref, a_ref, g_ref, last_ref, carry_ref) = refs
    i = pl.program_id(0)
    x = x_ref[...]
    xn = x * lax.rsqrt(jnp.mean(x * x, axis=-1, keepdims=True) + RMS_EPS) * g1_ref[...]
    rolled = pltpu.roll(xn, 1, 0)
    row = _row_iota(xn.shape)
    tm = xn.shape[0]
    if sample:
        prev = jnp.where(row % SUBLANES == 0, ext_ref[...], rolled)
        last_ref[...] = xn
    else:
        @pl.when(i == 0)
        def _():
            carry_ref[...] = jnp.zeros_like(carry_ref)
        first = jnp.where((i % tiles_per_seq) != 0, carry_ref[SUBLANES - 1:SUBLANES, :], 0.0)
        prev = jnp.where(row == 0, first, rolled)
        carry_ref[...] = xn[tm - SUBLANES:, :]
        last_ref[...] = xn[tm - SUBLANES:, :]
    dx = prev - xn
    xnb_ref[...] = xn.astype(BF16)
    xw = (xn + dx * mu_ref[0:1, :]).astype(BF16)
    hw = jnp.tanh(_dot(xw, dw1_ref[...]))
    wl = w0_ref[...] + _dot(hw.astype(BF16), dw2_ref[...])
    lw_ref[...] = -_sigmoid(wl) * math.exp(-0.5)
    xa = (xn + dx * mu_ref[1:2, :]).astype(BF16)
    ha = _dot(xa, aw1_ref[...])
    a_ref[...] = _sigmoid(a0_ref[...] + _dot(ha.astype(BF16), aw2_ref[...]))
    xg = (xn + dx * mu_ref[2:3, :]).astype(BF16)
    hg = _sigmoid(_dot(xg, gw1_ref[...]))
    g_ref[...] = _dot(hg.astype(BF16), gw2_ref[...])


def _mix_call(x2d, ext, p, *, sample, seq_len, tm):
    rows, d = x2d.shape
    g = p["dw2"].shape[1]
    n_i = rows // tm
    row_blk = lambda w: pl.BlockSpec((tm, w), lambda i: (i, 0))
    full = lambda a: pl.BlockSpec(a.shape, lambda i: (0, 0))
    weights = [p["norm1_g"], p["mu_lora"], p["decay_w0"], p["aaa_a0"], p["dw1"], p["aw1"], p["gw1"],
               p["dw2"], p["aw2"], p["gw2"]]
    in_specs = [row_blk(d)] + ([row_blk(d)] if sample else []) + [full(w) for w in weights]
    args = [x2d] + ([ext] if sample else []) + weights
    last_rows = tm if sample else SUBLANES
    out_shape = [jax.ShapeDtypeStruct((rows, d), BF16)] + [jax.ShapeDtypeStruct((rows, g), F32)] * 3 + [
        jax.ShapeDtypeStruct((n_i * last_rows, d), F32)]
    out_specs = [row_blk(d), row_blk(g), row_blk(g), row_blk(g), pl.BlockSpec((last_rows, d), lambda i: (i, 0))]
    return pl.pallas_call(
        functools.partial(_mix_kernel, sample=sample, tiles_per_seq=max(seq_len // tm, 1)),
        grid=(n_i,), in_specs=in_specs, out_specs=out_specs, out_shape=out_shape,
        scratch_shapes=[] if sample else [pltpu.VMEM((SUBLANES, d), F32)],
        compiler_params=_params("arbitrary"), name="mix")(*args)


def _rkv_kernel(*refs, sample, tiles_per_seq):
    if sample:
        x_ref, w_ref, mu_ref, ext_ref, o_ref = refs
    else:
        x_ref, w_ref, mu_ref, o_ref, carry_ref = refs
    i, j = pl.program_id(0), pl.program_id(1)
    p = _dot(x_ref[...], w_ref[...])
    tm = p.shape[0]
    if sample:
        prev = jnp.where(_row_iota(p.shape) % SUBLANES == 0, ext_ref[...], pltpu.roll(p, 1, 0))
    else:
        @pl.when(i == 0)
        def _():
            carry_ref[j] = jnp.zeros(carry_ref.shape[1:], F32)
        first1, _unused = _carry_rows(carry_ref, j, i, tiles_per_seq)
        prev = jnp.where(_row_iota(p.shape) == 0, first1, pltpu.roll(p, 1, 0))
        carry_ref[j] = p[tm - SUBLANES:, :]
    o_ref[...] = p + mu_ref[...] * (prev - p)


def _rkv_call(xnb, w_in, mu, ext, *, sample, seq_len, tm, tn):
    rows, d = xnb.shape
    n = mu.shape[1]
    n_i, n_j = rows // tm, n // tn
    in_specs = [pl.BlockSpec((tm, d), lambda i, j: (i, 0)), pl.BlockSpec((d, tn), lambda i, j: (0, j)),
                pl.BlockSpec((1, tn), lambda i, j: (0, j))]
    args = [xnb, w_in, mu]
    if sample:
        in_specs.append(pl.BlockSpec((tm, tn), lambda i, j: (i, j)))
        args.append(ext)
    return pl.pallas_call(
        functools.partial(_rkv_kernel, sample=sample, tiles_per_seq=max(seq_len // tm, 1)),
        grid=(n_i, n_j), in_specs=in_specs, out_specs=pl.BlockSpec((tm, tn), lambda i, j: (i, j)),
        out_shape=jax.ShapeDtypeStruct((rows, n), F32),
        scratch_shapes=[] if sample else [pltpu.VMEM((n_j, SUBLANES, tn), F32)],
        compiler_params=_params("arbitrary", "arbitrary"), name="rkv")(*args)


def _matmul_kernel(x_ref, w_ref, o_ref):
    o_ref[...] = _dot(x_ref[...], w_ref[...])


def _matmul_call(x, w, n, tn):
    rows, d = x.shape
    return pl.pallas_call(
        _matmul_kernel, grid=(n // tn,),
        in_specs=[pl.BlockSpec((rows, d), lambda j: (0, 0)), pl.BlockSpec((d, tn), lambda j: (0, j))],
        out_specs=pl.BlockSpec((rows, tn), lambda j: (0, j)),
        out_shape=jax.ShapeDtypeStruct((rows, n), F32),
        compiler_params=_params("arbitrary"), name="shift_proj")(x, w)


def _convbr_kernel(*refs, sample, tiles_per_seq):
    if sample:
        x_ref, wb_ref, wc_ref, wx_ref, cw_ref, e1_ref, e2_ref, y_ref, last_ref = refs
    else:
        x_ref, wb_ref, wc_ref, wx_ref, cw_ref, y_ref, last_ref, carry_ref = refs
    i, j = pl.program_id(0), pl.program_id(1)
    x = x_ref[...]
    cx = _dot(x, wc_ref[...]) * _dot(x, wx_ref[...])
    tm = cx.shape[0]
    if sample:
        prev1, prev2 = _prev_rows(cx, e1_ref[...], e2_ref[...], sample=True)
        last_ref[...] = cx
    else:
        @pl.when(i == 0)
        def _():
            carry_ref[j] = jnp.zeros(carry_ref.shape[1:], F32)
        first1, first2 = _carry_rows(carry_ref, j, i, tiles_per_seq)
        prev1, prev2 = _prev_rows(cx, first1, first2, sample=False)
        carry_ref[j] = cx[tm - SUBLANES:, :]
        last_ref[...] = cx[tm - SUBLANES:, :]
    hconv = prev2 * cw_ref[0:1, :] + prev1 * cw_ref[1:2, :] + cx * cw_ref[2:3, :]
    y_ref[...] = _dot(x, wb_ref[...]) * hconv


def _convbr_call(xnb, w_in, conv_w, e1, e2, *, col0, sample, seq_len, tm, tn):
    rows, d = xnb.shape
    gc = conv_w.shape[1]
    n_i, n_j = rows // tm, gc // tn
    off = col0 // tn
    nb = gc // tn
    wspec = lambda k: pl.BlockSpec((d, tn), lambda i, j: (0, off + k * nb + j))
    tile = pl.BlockSpec((tm, tn), lambda i, j: (i, j))
    in_specs = [pl.BlockSpec((tm, d), lambda i, j: (i, 0)), wspec(0), wspec(1), wspec(2),
                pl.BlockSpec((3, tn), lambda i, j: (0, j))]
    args = [xnb, w_in, w_in, w_in, conv_w]
    if sample:
        in_specs += [tile, tile]
        args += [e1, e2]
    last_rows = tm if sample else SUBLANES
    return pl.pallas_call(
        functools.partial(_convbr_kernel, sample=sample, tiles_per_seq=max(seq_len // tm, 1)),
        grid=(n_i, n_j), in_specs=in_specs,
        out_specs=[tile, pl.BlockSpec((last_rows, tn), lambda i, j: (i, j))],
        out_shape=[jax.ShapeDtypeStruct((rows, gc), F32), jax.ShapeDtypeStruct((n_i * last_rows, gc), F32)],
        scratch_shapes=[] if sample else [pltpu.VMEM((n_j, SUBLANES, tn), F32)],
        compiler_params=_params("arbitrary", "arbitrary"), name="convbr")(*args)


def _neumann_inverse(a):
    c = a.shape[0]
    eye = (_row_iota((c, c)) == lax.broadcasted_iota(jnp.int32, (c, c), 1)).astype(F32)
    p = -a
    t = eye + p
    n = 1
    while 2 * n < c:
        pb = p.astype(BF16)
        p = _dot(pb, pb)
        t = t + _dot(t.astype(BF16), p.astype(BF16))
        n *= 2
    return t


def _wkv_chunk(s, r, kt, v, kp, bt, lw, consts):
    tril_b, strict, incl, lane_lo, blockdiag = consts
    c = r.shape[0]
    lg = _split_dot(lw, tril_b, 3, dot=lambda x, b: _dot(b, x))
    e1 = jnp.exp(lg)
    e0 = jnp.exp(lg - lw)
    ei = jnp.exp(-lg)
    kt_ = kp * e0
    rt_ = r * e1
    kh = (kt * ei).astype(BF16)
    bh = (bt * ei).astype(BF16)
    vb = v.astype(BF16)
    sb = s.astype(BF16)
    zero = jnp.zeros_like(kt_)
    lhs4 = jnp.concatenate([jnp.where(lane_lo, kt_, zero), jnp.where(lane_lo, zero, kt_),
                            jnp.where(lane_lo, rt_, zero), jnp.where(lane_lo, zero, rt_)], axis=0).astype(BF16)
    gb = _dot_nt(lhs4, bh)
    gk = _dot_nt(lhs4, kh)
    p0 = _dot_nt(kt_.astype(BF16), sb)
    q0 = _dot_nt(rt_.astype(BF16), sb)
    us, os_ = [], []
    for h in range(2):
        ab = jnp.where(strict, gb[h * c:(h + 1) * c], 0.0)
        ak = jnp.where(strict, gk[h * c:(h + 1) * c], 0.0)
        bb = jnp.where(incl, gb[(2 + h) * c:(3 + h) * c], 0.0)
        bk = jnp.where(incl, gk[(2 + h) * c:(3 + h) * c], 0.0)
        t = _neumann_inverse(ab)
        rhs = p0 + _dot(ak.astype(BF16), vb)
        u = _dot(t.astype(BF16), rhs.astype(BF16))
        o = q0 + _dot(bk.astype(BF16), vb) - _dot(bb.astype(BF16), u.astype(BF16))
        us.append(u)
        os_.append(o)
    u = jnp.where(lane_lo, us[0], us[1])
    o = jnp.where(lane_lo, os_[0], os_[1])
    upd = _dot_tn(jnp.concatenate([vb, (-u).astype(BF16)], axis=0), jnp.concatenate([kh, bh], axis=0))
    s_new = (s + jnp.where(blockdiag, upd, 0.0)) * e1[c - 1:c, :]
    return o, s_new


def _wkv_kernel(*refs, sample, chunk, n_chunks):
    if sample:
        (r_ref, k_ref, v_ref, lw_ref, a_ref, g_ref, kk_ref, ka_ref, rk_ref, lnw_ref, lnb_ref, sin_ref,
         o_ref, sout_ref, kt_s, kp_s, bt_s, o_s) = refs
    else:
        (r_ref, k_ref, v_ref, lw_ref, a_ref, g_ref, kk_ref, ka_ref, rk_ref, lnw_ref, lnb_ref,
         o_ref, sout_ref, kt_s, kp_s, bt_s, o_s, s_s) = refs
    c = chunk
    lane128 = lax.broadcasted_iota(jnp.int32, (LANES, LANES), 1)
    row128 = _row_iota((LANES, LANES))
    blockdiag = (row128 < HEAD_DIM) == (lane128 < HEAD_DIM)
    block_ones = blockdiag.astype(BF16)
    seg_sum = lambda z: _split_dot(z, block_ones, 2)

    k = k_ref[...]
    a = a_ref[...]
    kk = k * kk_ref[...]
    kk = kk / jnp.maximum(jnp.sqrt(seg_sum(kk * kk)), 1e-12)
    kmod = k * (1.0 + (a - 1.0) * ka_ref[...])
    kt_s[...] = kmod
    kp_s[...] = kk
    bt_s[...] = kk * a

    ti = _row_iota((c, c))
    si = lax.broadcasted_iota(jnp.int32, (c, c), 1)
    consts = ((si <= ti).astype(BF16), si < ti, si <= ti,
              lax.broadcasted_iota(jnp.int32, (c, LANES), 1) < HEAD_DIM, blockdiag)

    def chunk_inputs(n):
        rows = pl.ds(pl.multiple_of(n * c, c), c)
        return (r_ref[rows, :], kt_s[rows, :], v_ref[rows, :], kp_s[rows, :], bt_s[rows, :], lw_ref[rows, :]), rows

    if sample:
        def body(n, carry):
            ins, rows = chunk_inputs(n)
            o, s_new = _wkv_chunk(sin_ref[n, 0], *ins, consts)
            o_s[rows, :] = o
            sout_ref[n, 0] = s_new
            return carry
        lax.fori_loop(0, n_chunks, body, 0)
    else:
        nb = pl.program_id(2)

        @pl.when(nb == 0)
        def _():
            s_s[...] = jnp.zeros_like(s_s)

        def body(n, carry):
            ins, rows = chunk_inputs(n)
            o, s_new = _wkv_chunk(s_s[...], *ins, consts)
            o_s[rows, :] = o
            s_s[...] = s_new
            return carry
        lax.fori_loop(0, n_chunks, body, 0)
        sout_ref[0, 0] = s_s[...]

    o = o_s[...]
    inv_n = 1.0 / HEAD_DIM
    mu = seg_sum(o) * inv_n
    dev = o - mu
    var = seg_sum(dev * dev) * inv_n
    on = dev * lax.rsqrt(var + GN_EPS) * lnw_ref[...] + lnb_ref[...]
    bonus = seg_sum(r_ref[...] * kmod * rk_ref[...]) * v_ref[...]
    o_ref[...] = ((on + bonus) * g_ref[...]).astype(BF16)


def _wkv_call(rkv, lw, a, g, p, s_bd, *, sample, seq_len, t_blk, chunk):
    rows, gdim = lw.shape
    npair = gdim // LANES
    nseq = rows // seq_len
    head = [p["k_k"], p["k_a"], p["r_k"], p["ln_x_w"], p["ln_x_b"]]
    scratch = [pltpu.VMEM((t_blk, LANES), F32)] * 4
    if sample:
        seq_blk = t_blk // seq_len
        grid = (nseq // seq_blk, npair)
        blk = lambda off: pl.BlockSpec((t_blk, LANES), lambda b, q, off=off: (b, off + q))
        hspec = pl.BlockSpec((1, LANES), lambda b, q: (0, q))
        sspec = pl.BlockSpec((seq_blk, 1, LANES, LANES), lambda b, q: (b, q, 0, 0))
        in_specs = [blk(0), blk(npair), blk(2 * npair), blk(0), blk(0), blk(0)] + [hspec] * 5 + [sspec]
        args = [rkv, rkv, rkv, lw, a, g] + head + [s_bd]
        sem = ("arbitrary", "arbitrary")
        n_chunks = seq_blk
    else:
        nblk = seq_len // t_blk
        grid = (nseq, npair, nblk)
        blk = lambda off: pl.BlockSpec((t_blk, LANES), lambda b, q, n, off=off: (b * nblk + n, off + q))
        hspec = pl.BlockSpec((1, LANES), lambda b, q, n: (0, q))
        sspec = pl.BlockSpec((1, 1, LANES, LANES), lambda b, q, n: (b, q, 0, 0))
        in_specs = [blk(0), blk(npair), blk(2 * npair), blk(0), blk(0), blk(0)] + [hspec] * 5
        args = [rkv, rkv, rkv, lw, a, g] + head
        sem = ("arbitrary", "arbitrary", "arbitrary")
        scratch = scratch + [pltpu.VMEM((LANES, LANES), F32)]
        n_chunks = t_blk // chunk
    return pl.pallas_call(
        functools.partial(_wkv_kernel, sample=sample, chunk=chunk, n_chunks=n_chunks),
        grid=grid, in_specs=in_specs, out_specs=[blk(0), sspec],
        out_shape=[jax.ShapeDtypeStruct((rows, gdim), BF16),
                   jax.ShapeDtypeStruct((nseq, npair, LANES, LANES), F32)],
        scratch_shapes=scratch, compiler_params=_params(*sem), name="wkv")(*args)


def _outproj_kernel(orw_ref, ycv_ref, cg_ref, w_ref, x_ref, o_ref, lhs_s):
    g = orw_ref.shape[1]

    @pl.when(pl.program_id(1) == 0)
    def _():
        y = ycv_ref[...]
        yn = y * lax.rsqrt(jnp.mean(y * y, axis=-1, keepdims=True) + RMS_EPS) * cg_ref[...]
        lhs_s[:, :g] = orw_ref[...]
        lhs_s[:, g:] = yn.astype(BF16)

    o_ref[...] = x_ref[...] + _dot(lhs_s[...], w_ref[...])


def _outproj_call(orw, ycv, conv_g, w_out, x2d, *, tm, tn):
    rows, d = x2d.shape
    g, gc = orw.shape[1], ycv.shape[1]
    tile = pl.BlockSpec((tm, tn), lambda i, j: (i, j))
    return pl.pallas_call(
        _outproj_kernel, grid=(rows // tm, d // tn),
        in_specs=[pl.BlockSpec((tm, g), lambda i, j: (i, 0)), pl.BlockSpec((tm, gc), lambda i, j: (i, 0)),
                  pl.BlockSpec((1, gc), lambda i, j: (0, 0)), pl.BlockSpec((d, tn), lambda i, j: (0, j)), tile],
        out_specs=tile, out_shape=jax.ShapeDtypeStruct((rows, d), F32),
        scratch_shapes=[pltpu.VMEM((tm, d), BF16)],
        compiler_params=_params("arbitrary", "arbitrary"), name="outproj")(orw, ycv, conv_g, w_out, x2d)


def _norm_kernel(x_ref, g_ref, o_ref):
    x = x_ref[...]
    y = x * lax.rsqrt(jnp.mean(x * x, axis=-1, keepdims=True) + RMS_EPS) * g_ref[...]
    o_ref[...] = y.astype(o_ref.dtype)


def _norm_call(x2d, g, dtype, *, tm):
    rows, d = x2d.shape
    blk = pl.BlockSpec((tm, d), lambda i: (i, 0))
    return pl.pallas_call(
        _norm_kernel, grid=(rows // tm,), in_specs=[blk, pl.BlockSpec((1, d), lambda i: (0, 0))],
        out_specs=blk, out_shape=jax.ShapeDtypeStruct((rows, d), dtype),
        compiler_params=_params("arbitrary"), name="rmsnorm")(x2d, g)


def _ffnup_kernel(*refs, sample, tiles_per_seq):
    if sample:
        x_ref, w1_ref, w3_ref, cw_ref, cb_ref, e1_ref, e2_ref, h_ref, last_ref = refs
    else:
        x_ref, w1_ref, w3_ref, cw_ref, cb_ref, h_ref, last_ref, carry_ref = refs
    i, j = pl.program_id(0), pl.program_id(1)
    x = x_ref[...]
    u = _dot(x, w1_ref[...])
    tm = u.shape[0]
    if sample:
        prev1, prev2 = _prev_rows(u, e1_ref[...], e2_ref[...], sample=True)
        last_ref[...] = u
    else:
        @pl.when(i == 0)
        def _():
            carry_ref[j] = jnp.zeros(carry_ref.shape[1:], F32)
        first1, first2 = _carry_rows(carry_ref, j, i, tiles_per_seq)
        prev1, prev2 = _prev_rows(u, first1, first2, sample=False)
        carry_ref[j] = u[tm - SUBLANES:, :]
        last_ref[...] = u[tm - SUBLANES:, :]
    z = prev2 * cw_ref[0:1, :] + prev1 * cw_ref[1:2, :] + u * cw_ref[2:3, :] + cb_ref[...]
    h_ref[...] = (z * _sigmoid(z) * _dot(x, w3_ref[...])).astype(BF16)


def _ffnup_call(hn, w1, w3, conv_w, conv_b, e1, e2, *, sample, seq_len, tm, tn):
    rows, d = hn.shape
    dff = w1.shape[1]
    n_i, n_j = rows // tm, dff // tn
    tile = pl.BlockSpec((tm, tn), lambda i, j: (i, j))
    wspec = pl.BlockSpec((d, tn), lambda i, j: (0, j))
    in_specs = [pl.BlockSpec((tm, d), lambda i, j: (i, 0)), wspec, wspec,
                pl.BlockSpec((3, tn), lambda i, j: (0, j)), pl.BlockSpec((1, tn), lambda i, j: (0, j))]
    args = [hn, w1, w3, conv_w, conv_b]
    if sample:
        in_specs += [tile, tile]
        args += [e1, e2]
    last_rows = tm if sample else SUBLANES
    return pl.pallas_call(
        functools.partial(_ffnup_kernel, sample=sample, tiles_per_seq=max(seq_len // tm, 1)),
        grid=(n_i, n_j), in_specs=in_specs,
        out_specs=[tile, pl.BlockSpec((last_rows, tn), lambda i, j: (i, j))],
        out_shape=[jax.ShapeDtypeStruct((rows, dff), BF16), jax.ShapeDtypeStruct((n_i * last_rows, dff), F32)],
        scratch_shapes=[] if sample else [pltpu.VMEM((n_j, SUBLANES, tn), F32)],
        compiler_params=_params("arbitrary", "arbitrary"), name="ffnup")(*args)


def _ffndown_kernel(h_ref, w_ref, x_ref, o_ref):
    o_ref[...] = x_ref[...] + _dot(h_ref[...], w_ref[...])


def _ffndown_call(h, w2, x1, *, tm, tn):
    rows, dff = h.shape
    d = w2.shape[1]
    tile = pl.BlockSpec((tm, tn), lambda i, j: (i, j))
    return pl.pallas_call(
        _ffndown_kernel, grid=(rows // tm, d // tn),
        in_specs=[pl.BlockSpec((tm, dff), lambda i, j: (i, 0)), pl.BlockSpec((dff, tn), lambda i, j: (0, j)), tile],
        out_specs=tile, out_shape=jax.ShapeDtypeStruct((rows, d), F32),
        compiler_params=_params("arbitrary", "arbitrary"), name="ffndown")(h, w2, x1)


def _tile(n, want):
    t = min(n, want)
    while n % t or (t % SUBLANES and t != n):
        t -= 1
    return t


def _col_tile(n, want):
    t = min(n, want)
    while n % t or t % LANES:
        t -= LANES
    return t


def _state_rows(state, taps, seq_len):
    nseq, _, n = state.shape
    z = jnp.zeros((nseq, seq_len, n), state.dtype)
    e1 = z.at[:, 0].set(state[:, taps - 1])
    e2 = z.at[:, 0].set(state[:, taps - 2]).at[:, 1].set(state[:, taps - 1]) if taps > 1 else None
    flat = lambda e: None if e is None else e.reshape(nseq * seq_len, n)
    return flat(e1), flat(e2)


def _layer(x, states, p, *, sample):
    nseq, seq_len, d = x.shape
    rows = nseq * seq_len
    g = p["dw2"].shape[1]
    gc = d - g
    dff = p["ffn_w1"].shape[1]
    npair = g // LANES
    x2d = x.reshape(rows, d)
    big = dict(sample=sample, seq_len=seq_len)
    tm_big = rows if sample else _tile(seq_len, 1024)

    if sample:
        shift, wkv, conv, ffn = states
        ext_x, _ = _state_rows(shift[:, None, :], 1, seq_len)
        p0 = _matmul_call(shift.astype(BF16), p["w_in"], 3 * g, _col_tile(3 * g, 512))
        ext_p, _ = _state_rows(p0[:, None, :], 1, seq_len)
        ce1, ce2 = _state_rows(conv, 2, seq_len)
        fe1, fe2 = _state_rows(ffn, 2, seq_len)
        s5 = wkv.reshape(nseq, npair, 2, HEAD_DIM, HEAD_DIM)
        zero = jnp.zeros_like(s5[:, :, 0])
        s_bd = jnp.concatenate([jnp.concatenate([s5[:, :, 0], zero], -1),
                                jnp.concatenate([zero, s5[:, :, 1]], -1)], -2)
    else:
        ext_x = ext_p = ce1 = ce2 = fe1 = fe2 = s_bd = None

    tm_mix = _tile(rows if sample else seq_len, 128)
    xnb, lw, a, gate, xlast = _mix_call(x2d, ext_x, p, sample=sample, seq_len=seq_len, tm=tm_mix)
    rkv = _rkv_call(xnb, p["w_in"], p["mu_rkv"], ext_p, tm=tm_big, tn=_col_tile(3 * g, 512), **big)
    ycv, cxlast = _convbr_call(xnb, p["w_in"], p["conv_w"], ce1, ce2, col0=3 * g, tm=tm_big,
                               tn=_col_tile(gc, 256), **big)
    if sample:
        t_blk, chunk = SUBLANES * seq_len, seq_len
    else:
        chunk = _tile(seq_len, 64)
        t_blk = _tile(seq_len, 256)
    orw, s_new = _wkv_call(rkv, lw, a, gate, p, s_bd, t_blk=t_blk, chunk=chunk, **big)
    tm_e = _tile(rows, 512)
    x1 = _outproj_call(orw, ycv, p["conv_norm_g"], p["w_out"], x2d, tm=tm_e, tn=_col_tile(d, 512))
    hn = _norm_call(x1, p["norm2_g"], BF16, tm=_tile(rows, 256))
    h, ulast = _ffnup_call(hn, p["ffn_w1"], p["ffn_w3"], p["ffn_conv_w"], p["ffn_conv_b"], fe1, fe2,
                           tm=tm_big, tn=_col_tile(dff, 256), **big)
    x2 = _ffndown_call(h, p["ffn_w2"], x1, tm=tm_e, tn=_col_tile(d, 256))

    def last_rows(arr, tile_rows, k):
        if sample:
            return arr.reshape(nseq, seq_len, -1)[:, seq_len - k:]
        per_seq = seq_len // tile_rows
        return arr.reshape(nseq, per_seq, SUBLANES, -1)[:, -1, SUBLANES - k:]
    new_shift = last_rows(xlast, tm_mix, 1)[:, 0]
    new_conv = last_rows(cxlast, tm_big, 2)
    new_ffn = last_rows(ulast, tm_big, 2)
    new_wkv = jnp.stack([s_new[:, :, :HEAD_DIM, :HEAD_DIM], s_new[:, :, HEAD_DIM:, HEAD_DIM:]], axis=2)
    new_wkv = new_wkv.reshape(nseq, 2 * npair, HEAD_DIM, HEAD_DIM)
    return x2.reshape(nseq, seq_len, d), new_shift, new_wkv, new_conv, new_ffn


def _pad_to(a, axis, mult):
    pad = (-a.shape[axis]) % mult
    if not pad:
        return a
    widths = [(0, 0)] * a.ndim
    widths[axis] = (0, pad)
    return jnp.pad(a, widths)


def kernel(x_prompt, x_sample, state_shift, state_wkv, state_conv, state_ffn, norm1_g, w_in, mu_rkv, mu_lora, decay_w0, decay_w1, decay_w2, aaa_a0, aaa_a1, aaa_a2, gate_g1, gate_g2, k_k, k_a, r_k, ln_x_w, ln_x_b, conv_w, conv_norm_g, w_out, norm2_g, ffn_w1, ffn_conv_w, ffn_conv_b, ffn_w3, ffn_w2, final_norm_g):
    depth = w_in.shape[0]
    d = x_prompt.shape[-1]
    row = lambda v: v.reshape(1, -1).astype(F32)
    yp, ys = x_prompt, x_sample
    outs_p, outs_s = [], []
    for l in range(depth):
        p = dict(
            norm1_g=row(norm1_g[l]), w_in=w_in[l].astype(BF16), mu_rkv=row(mu_rkv[l]), mu_lora=mu_lora[l],
            decay_w0=row(decay_w0[l]), aaa_a0=row(aaa_a0[l]),
            dw1=decay_w1[l].astype(BF16), dw2=decay_w2[l].astype(BF16),
            aw1=aaa_a1[l].astype(BF16), aw2=aaa_a2[l].astype(BF16),
            gw1=_pad_to(gate_g1[l], 1, LANES).astype(BF16), gw2=_pad_to(gate_g2[l], 0, LANES).astype(BF16),
            k_k=row(k_k[l]), k_a=row(k_a[l]), r_k=row(r_k[l]), ln_x_w=row(ln_x_w[l]), ln_x_b=row(ln_x_b[l]),
            conv_w=conv_w[l], conv_norm_g=row(conv_norm_g[l]), w_out=w_out[l].astype(BF16),
            norm2_g=row(norm2_g[l]), ffn_w1=ffn_w1[l].astype(BF16), ffn_conv_w=ffn_conv_w[l],
            ffn_conv_b=row(ffn_conv_b[l]), ffn_w3=ffn_w3[l].astype(BF16), ffn_w2=ffn_w2[l].astype(BF16))
        yp, *st_p = _layer(yp, None, p, sample=False)
        ys, *st_s = _layer(ys, (state_shift[l], state_wkv[l], state_conv[l], state_ffn[l]), p, sample=True)
        outs_p.append(st_p)
        outs_s.append(st_s)
    fin = row(final_norm_g)
    y_prompt = _norm_call(yp.reshape(-1, d), fin, F32, tm=_tile(yp.shape[0] * yp.shape[1], 256)).reshape(yp.shape)
    y_sample = _norm_call(ys.reshape(-1, d), fin, F32, tm=_tile(ys.shape[0] * ys.shape[1], 256)).reshape(ys.shape)
    stack = lambda outs, k: jnp.stack([o[k] for o in outs])
    return (y_prompt, y_sample,
            stack(outs_p, 0), stack(outs_p, 1), stack(outs_p, 2), stack(outs_p, 3),
            stack(outs_s, 0), stack(outs_s, 1), stack(outs_s, 2), stack(outs_s, 3))
```

```python
import functools
import math

import jax
import jax.numpy as jnp
from jax import lax
from jax.experimental import pallas as pl
from jax.experimental.pallas import tpu as pltpu

F32 = jnp.float32
BF16 = jnp.bfloat16

HEAD_DIM = 64
LANES = 128
SUBLANES = 8
WKV_SUB = 64
RMS_EPS = 1e-6
GN_EPS = 64e-5
VMEM_LIMIT_BYTES = 56 * 1024 * 1024


def _params(*sem):
    return pltpu.CompilerParams(dimension_semantics=sem, vmem_limit_bytes=VMEM_LIMIT_BYTES)


def _dot(a, b):
    return jnp.dot(a, b, preferred_element_type=F32)


def _dot_nt(a, b):
    return lax.dot_general(a, b, (((1,), (1,)), ((), ())), preferred_element_type=F32)


def _dot_tn(a, b):
    return lax.dot_general(a, b, (((0,), (0,)), ((), ())), preferred_element_type=F32)


def _sigmoid(z):
    return 1.0 / (1.0 + jnp.exp(-z))


def _split_dot(x, b_exact, terms, dot=_dot):
    acc = None
    rem = x
    for _ in range(terms):
        hi = rem.astype(BF16)
        part = dot(hi, b_exact)
        acc = part if acc is None else acc + part
        rem = rem - hi.astype(F32)
    return acc


def _row_iota(shape):
    return lax.broadcasted_iota(jnp.int32, shape, 0)


def _prev_rows(p, first1, first2, *, sample):
    row = _row_iota(p.shape)
    r1 = pltpu.roll(p, 1, 0)
    r2 = pltpu.roll(p, 2, 0)
    if sample:
        t = row % SUBLANES
        return jnp.where(t == 0, first1, r1), jnp.where(t < 2, first2, r2)
    prev1 = jnp.where(row == 0, first1, r1)
    prev2 = jnp.where(row == 0, first2, jnp.where(row == 1, first1, r2))
    return prev1, prev2


def _carry_rows(carry_ref, j, i, tiles_per_seq):
    c = carry_ref[j]
    keep = (i % tiles_per_seq) != 0
    first1 = jnp.where(keep, c[SUBLANES - 1:SUBLANES, :], 0.0)
    first2 = jnp.where(keep, c[SUBLANES - 2:SUBLANES - 1, :], 0.0)
    return first1, first2


def _mix_kernel(*refs, sample, tiles_per_seq):
    if sample:
        (x_ref, ext_ref, g1_ref, mu_ref, w0_ref, a0_ref, dw1_ref, aw1_ref, gw1_ref, dw2_ref,
         aw2_ref, gw2_ref, xnb_ref, lw_ref, a_ref, g_ref, last_ref) = refs
    else:
        (x_ref, g1_ref, mu_ref, w0_ref, a0_ref, dw1_ref, aw1_ref, gw1_ref, dw2_ref,
         aw2_ref, gw2_ref, xnb_ref, lw_ref, a_ref, g_ref, last_ref, carry_ref) = refs
    i = pl.program_id(0)
    x = x_ref[...]
    xn = x * lax.rsqrt(jnp.mean(x * x, axis=-1, keepdims=True) + RMS_EPS) * g1_ref[...]
    rolled = pltpu.roll(xn, 1, 0)
    row = _row_iota(xn.shape)
    tm = xn.shape[0]
    if sample:
        prev = jnp.where(row % SUBLANES == 0, ext_ref[...], rolled)
        last_ref[...] = xn
    else:
        @pl.when(i == 0)
        def _():
            carry_ref[...] = jnp.zeros_like(carry_ref)
        first = jnp.where((i % tiles_per_seq) != 0, carry_ref[SUBLANES - 1:SUBLANES, :], 0.0)
        prev = jnp.where(row == 0, first, rolled)
        carry_ref[...] = xn[tm - SUBLANES:, :]
        last_ref[...] = xn[tm - SUBLANES:, :]
    dx = prev - xn
    xnb_ref[...] = xn.astype(BF16)
    xw = (xn + dx * mu_ref[0:1, :]).astype(BF16)
    hw = jnp.tanh(_dot(xw, dw1_ref[...]))
    wl = w0_ref[...] + _dot(hw.astype(BF16), dw2_ref[...])
    lw_ref[...] = -_sigmoid(wl) * math.exp(-0.5)
    xa = (xn + dx * mu_ref[1:2, :]).astype(BF16)
    ha = _dot(xa, aw1_ref[...])
    a_ref[...] = _sigmoid(a0_ref[...] + _dot(ha.astype(BF16), aw2_ref[...]))
    xg = (xn + dx * mu_ref[2:3, :]).astype(BF16)
    hg = _sigmoid(_dot(xg, gw1_ref[...]))
    g_ref[...] = _dot(hg.astype(BF16), gw2_ref[...])


def _mix_call(x2d, ext, p, *, sample, seq_len, tm):
    rows, d = x2d.shape
    g = p["dw2"].shape[1]
    n_i = rows // tm
    row_blk = lambda w: pl.BlockSpec((tm, w), lambda i: (i, 0))
    full = lambda a: pl.BlockSpec(a.shape, lambda i: (0, 0))
    weights = [p["norm1_g"], p["mu_lora"], p["decay_w0"], p["aaa_a0"], p["dw1"], p["aw1"], p["gw1"],
               p["dw2"], p["aw2"], p["gw2"]]
    in_specs = [row_blk(d)] + ([row_blk(d)] if sample else []) + [full(w) for w in weights]
    args = [x2d] + ([ext] if sample else []) + weights
    last_rows = tm if sample else SUBLANES
    out_shape = [jax.ShapeDtypeStruct((rows, d), BF16)] + [jax.ShapeDtypeStruct((rows, g), F32)] * 3 + [
        jax.ShapeDtypeStruct((n_i * last_rows, d), F32)]
    out_specs = [row_blk(d), row_blk(g), row_blk(g), row_blk(g), pl.BlockSpec((last_rows, d), lambda i: (i, 0))]
    return pl.pallas_call(
        functools.partial(_mix_kernel, sample=sample, tiles_per_seq=max(seq_len // tm, 1)),
        grid=(n_i,), in_specs=in_specs, out_specs=out_specs, out_shape=out_shape,
        scratch_shapes=[] if sample else [pltpu.VMEM((SUBLANES, d), F32)],
        compiler_params=_params("arbitrary"), name="mix")(*args)


def _rkv_kernel(*refs, sample, tiles_per_seq):
    if sample:
        x_ref, w_ref, mu_ref, ext_ref, o_ref = refs
    else:
        x_ref, w_ref, mu_ref, o_ref, carry_ref = refs
    i, j = pl.program_id(0), pl.program_id(1)
    p = _dot(x_ref[...], w_ref[...])
    tm = p.shape[0]
    if sample:
        prev = jnp.where(_row_iota(p.shape) % SUBLANES == 0, ext_ref[...], pltpu.roll(p, 1, 0))
    else:
        @pl.when(i == 0)
        def _():
            carry_ref[j] = jnp.zeros(carry_ref.shape[1:], F32)
        first1, _unused = _carry_rows(carry_ref, j, i, tiles_per_seq)
        prev = jnp.where(_row_iota(p.shape) == 0, first1, pltpu.roll(p, 1, 0))
        carry_ref[j] = p[tm - SUBLANES:, :]
    o_ref[...] = p + mu_ref[...] * (prev - p)


def _rkv_call(xnb, w_in, mu, ext, *, sample, seq_len, tm, tn):
    rows, d = xnb.shape
    n = mu.shape[1]
    n_i, n_j = rows // tm, n // tn
    in_specs = [pl.BlockSpec((tm, d), lambda i, j: (i, 0)), pl.BlockSpec((d, tn), lambda i, j: (0, j)),
                pl.BlockSpec((1, tn), lambda i, j: (0, j))]
    args = [xnb, w_in, mu]
    if sample:
        in_specs.append(pl.BlockSpec((tm, tn), lambda i, j: (i, j)))
        args.append(ext)
    return pl.pallas_call(
        functools.partial(_rkv_kernel, sample=sample, tiles_per_seq=max(seq_len // tm, 1)),
        grid=(n_i, n_j), in_specs=in_specs, out_specs=pl.BlockSpec((tm, tn), lambda i, j: (i, j)),
        out_shape=jax.ShapeDtypeStruct((rows, n), F32),
        scratch_shapes=[] if sample else [pltpu.VMEM((n_j, SUBLANES, tn), F32)],
        compiler_params=_params("arbitrary", "arbitrary"), name="rkv")(*args)


def _matmul_kernel(x_ref, w_ref, o_ref):
    o_ref[...] = _dot(x_ref[...], w_ref[...])


def _matmul_call(x, w, n, tn):
    rows, d = x.shape
    return pl.pallas_call(
        _matmul_kernel, grid=(n // tn,),
        in_specs=[pl.BlockSpec((rows, d), lambda j: (0, 0)), pl.BlockSpec((d, tn), lambda j: (0, j))],
        out_specs=pl.BlockSpec((rows, tn), lambda j: (0, j)),
        out_shape=jax.ShapeDtypeStruct((rows, n), F32),
        compiler_params=_params("arbitrary"), name="shift_proj")(x, w)


def _convbr_kernel(*refs, sample, tiles_per_seq):
    if sample:
        x_ref, wb_ref, wc_ref, wx_ref, cw_ref, e1_ref, e2_ref, y_ref, last_ref = refs
    else:
        x_ref, wb_ref, wc_ref, wx_ref, cw_ref, y_ref, last_ref, carry_ref = refs
    i, j = pl.program_id(0), pl.program_id(1)
    x = x_ref[...]
    cx = _dot(x, wc_ref[...]) * _dot(x, wx_ref[...])
    tm = cx.shape[0]
    if sample:
        prev1, prev2 = _prev_rows(cx, e1_ref[...], e2_ref[...], sample=True)
        last_ref[...] = cx
    else:
        @pl.when(i == 0)
        def _():
            carry_ref[j] = jnp.zeros(carry_ref.shape[1:], F32)
        first1, first2 = _carry_rows(carry_ref, j, i, tiles_per_seq)
        prev1, prev2 = _prev_rows(cx, first1, first2, sample=False)
        carry_ref[j] = cx[tm - SUBLANES:, :]
        last_ref[...] = cx[tm - SUBLANES:, :]
    hconv = prev2 * cw_ref[0:1, :] + prev1 * cw_ref[1:2, :] + cx * cw_ref[2:3, :]
    y_ref[...] = _dot(x, wb_ref[...]) * hconv


def _convbr_call(xnb, w_in, conv_w, e1, e2, *, col0, sample, seq_len, tm, tn):
    rows, d = xnb.shape
    gc = conv_w.shape[1]
    n_i, n_j = rows // tm, gc // tn
    off = col0 // tn
    nb = gc // tn
    wspec = lambda k: pl.BlockSpec((d, tn), lambda i, j: (0, off + k * nb + j))
    tile = pl.BlockSpec((tm, tn), lambda i, j: (i, j))
    in_specs = [pl.BlockSpec((tm, d), lambda i, j: (i, 0)), wspec(0), wspec(1), wspec(2),
                pl.BlockSpec((3, tn), lambda i, j: (0, j))]
    args = [xnb, w_in, w_in, w_in, conv_w]
    if sample:
        in_specs += [tile, tile]
        args += [e1, e2]
    last_rows = tm if sample else SUBLANES
    return pl.pallas_call(
        functools.partial(_convbr_kernel, sample=sample, tiles_per_seq=max(seq_len // tm, 1)),
        grid=(n_i, n_j), in_specs=in_specs,
        out_specs=[tile, pl.BlockSpec((last_rows, tn), lambda i, j: (i, j))],
        out_shape=[jax.ShapeDtypeStruct((rows, gc), F32), jax.ShapeDtypeStruct((n_i * last_rows, gc), F32)],
        scratch_shapes=[] if sample else [pltpu.VMEM((n_j, SUBLANES, tn), F32)],
        compiler_params=_params("arbitrary", "arbitrary"), name="convbr")(*args)


def _wkv_masks(c):
    m = 2 * WKV_SUB
    r2 = jnp.bitwise_and(_row_iota((m, m)), WKV_SUB - 1)
    c2 = jnp.bitwise_and(lax.broadcasted_iota(jnp.int32, (m, m), 1), WKV_SUB - 1)
    shift = int(math.log2(c))
    same = jnp.right_shift(r2, shift) == jnp.right_shift(c2, shift)
    eye = (_row_iota((m, m)) == lax.broadcasted_iota(jnp.int32, (m, m), 1)).astype(F32)
    lane_lo = lax.broadcasted_iota(jnp.int32, (WKV_SUB, LANES), 1) < HEAD_DIM
    return lane_lo, same & (c2 < r2), same & (c2 <= r2), eye


def _stack_heads(x, lane_lo):
    z = jnp.zeros_like(x)
    return jnp.concatenate([jnp.where(lane_lo, x, z), jnp.where(lane_lo, z, x)], axis=0)


def _fold_heads(x):
    return x[:WKV_SUB] + x[WKV_SUB:]


def _wkv_phase1(r, kt, v, kp, bt, lg, lw, masks, c):
    lane_lo, strict, incl, eye = masks
    m = 2 * WKV_SUB
    e1 = jnp.exp(lg)
    e0 = jnp.exp(lg - lw)
    ei = jnp.exp(-lg)
    rt_st = _stack_heads(r * e1, lane_lo)
    kp_b = _stack_heads(kp * e0, lane_lo).astype(BF16)
    kh_st = _stack_heads(kt * ei, lane_lo)
    bh_st = _stack_heads(bt * ei, lane_lo)
    v_st = _stack_heads(v, lane_lo)
    kh_b, bh_b, v_b = kh_st.astype(BF16), bh_st.astype(BF16), v_st.astype(BF16)
    g = _dot_nt(jnp.concatenate([kp_b, rt_st.astype(BF16)], axis=0), jnp.concatenate([bh_b, kh_b], axis=0))
    ab = jnp.where(strict, g[:m, :m], 0.0)
    ak = jnp.where(strict, g[:m, m:], 0.0)
    bb = jnp.where(incl, g[m:, :m], 0.0)
    bk = jnp.where(incl, g[m:, m:], 0.0)
    p = -ab
    t = eye + p
    n = 1
    while 2 * n < c:
        pb = p.astype(BF16)
        p = _dot(pb, pb)
        t = t + _dot(t.astype(BF16), p.astype(BF16))
        n *= 2
    abv = _dot(jnp.concatenate([ak, bk], axis=0).astype(BF16), v_b)
    akv, bkv = abv[:m], abv[m:]
    tt = _dot(t.astype(BF16), jnp.concatenate([kp_b, akv.astype(BF16)], axis=1))
    bbtt = _dot(bb.astype(BF16), tt.astype(BF16))
    rq = _fold_heads(rt_st - bbtt[:, :LANES])
    ov = _fold_heads(bkv - bbtt[:, LANES:])
    return dict(rq=rq, ov=ov, tk_st=tt[:, :LANES], tav_st=tt[:, LANES:], kh_st=kh_st, bh_st=bh_st,
                v_st=v_st, e1=e1)


def _wkv_transition(ph):
    gam = ph["e1"][WKV_SUB - 1:WKV_SUB, :]
    bh_b, kh_b = ph["bh_st"].astype(BF16), ph["kh_st"].astype(BF16)
    kg = (_dot_tn(ph["tk_st"].astype(BF16), bh_b) * gam).astype(BF16)
    lhs = jnp.concatenate([ph["v_st"].astype(BF16), (-ph["tav_st"]).astype(BF16)], axis=0)
    bcg = _dot_tn(lhs, jnp.concatenate([kh_b, bh_b], axis=0)) * gam
    return gam, kg, bcg


def _wkv_unit_small(s, ph, u, c, lane_lo_c):
    rows = slice(u * c, (u + 1) * c)
    rows_hi = slice(WKV_SUB + u * c, WKV_SUB + (u + 1) * c)
    pick = lambda x: jnp.concatenate([x[rows], x[rows_hi]], axis=0)
    tkm = ph["tkm"][rows]
    tav = ph["tav"][rows]
    ou = _dot_nt(jnp.concatenate([ph["rq"][rows], tkm], axis=0).astype(BF16), s.astype(BF16))
    o = ou[:c] + ph["ov"][rows]
    uu = ou[c:] + tav
    z = jnp.zeros_like(uu)
    u_st = jnp.concatenate([jnp.where(lane_lo_c, uu, z), jnp.where(lane_lo_c, z, uu)], axis=0)
    gam = ph["e1"][(u + 1) * c - 1:(u + 1) * c, :]
    lhs = jnp.concatenate([pick(ph["v_st"]), -u_st], axis=0).astype(BF16)
    rhs = (jnp.concatenate([pick(ph["kh_st"]), pick(ph["bh_st"])], axis=0) * gam).astype(BF16)
    return o, s * gam + _dot_tn(lhs, rhs)


def _wkv_kernel(*refs, sample, chunk, n_pairs, n_sub):
    if sample:
        (r_ref, k_ref, v_ref, lw_ref, a_ref, g_ref, kk_ref, ka_ref, rk_ref, lnw_ref, lnb_ref, sin_ref,
         o_ref, sout_ref) = refs
    else:
        (r_ref, k_ref, v_ref, lw_ref, a_ref, g_ref, kk_ref, ka_ref, rk_ref, lnw_ref, lnb_ref,
         o_ref, sout_ref, s_s) = refs
    c = chunk
    lane128 = lax.broadcasted_iota(jnp.int32, (LANES, LANES), 1)
    row128 = _row_iota((LANES, LANES))
    block_ones = ((row128 < HEAD_DIM) == (lane128 < HEAD_DIM)).astype(BF16)
    seg_sum = lambda z: _split_dot(z, block_ones, 2)
    masks = _wkv_masks(c)
    r2 = jnp.bitwise_and(_row_iota((WKV_SUB, WKV_SUB)), WKV_SUB - 1)
    c2 = lax.broadcasted_iota(jnp.int32, (WKV_SUB, WKV_SUB), 1)
    shift = int(math.log2(c))
    tril_b = ((jnp.right_shift(r2, shift) == jnp.right_shift(c2, shift)) & (c2 <= r2)).astype(BF16)
    lane_lo_c = lax.broadcasted_iota(jnp.int32, (c, LANES), 1) < HEAD_DIM

    if not sample:
        @pl.when(pl.program_id(2) == 0)
        def _():
            s_s[...] = jnp.zeros_like(s_s)

    lgs = []
    for sb in range(n_sub):
        rows = slice(sb * WKV_SUB, (sb + 1) * WKV_SUB)
        lgs.append(_split_dot(lw_ref[rows, :], tril_b, 3, dot=lambda x, b: _dot(b, x)))

    for q in range(n_pairs):
        lanes = slice(q * LANES, (q + 1) * LANES)
        r = r_ref[:, lanes]
        k = k_ref[:, lanes]
        v = v_ref[:, lanes]
        a = a_ref[:, lanes]
        lw = lw_ref[:, lanes]
        kk = k * kk_ref[:, lanes]
        kk = kk / jnp.maximum(jnp.sqrt(seg_sum(kk * kk)), 1e-12)
        kmod = k * (1.0 + (a - 1.0) * ka_ref[:, lanes])
        bt = kk * a
        outs = []
        if not sample:
            s = s_s[q]
        for sb in range(n_sub):
            rows = slice(sb * WKV_SUB, (sb + 1) * WKV_SUB)
            ph = _wkv_phase1(r[rows], kmod[rows], v[rows], kk[rows], bt[rows], lgs[sb][:, lanes], lw[rows],
                             masks, c)
            if sample:
                ph["tkm"] = _fold_heads(ph["tk_st"])
                ph["tav"] = _fold_heads(ph["tav_st"])
                for u in range(WKV_SUB // c):
                    seq = sb * (WKV_SUB // c) + u
                    o, s_new = _wkv_unit_small(sin_ref[seq, q], ph, u, c, lane_lo_c)
                    sout_ref[seq, q] = s_new
                    outs.append(o)
            else:
                gam, kg, bcg = _wkv_transition(ph)
                sb16 = s.astype(BF16)
                outs.append(_dot_nt(ph["rq"].astype(BF16), sb16) + ph["ov"])
                s = s * gam - _dot(sb16, kg) + bcg
        if not sample:
            s_s[q] = s
            sout_ref[0, q] = s
        o = jnp.concatenate(outs, axis=0)
        inv_n = 1.0 / HEAD_DIM
        mu = seg_sum(o) * inv_n
        dev = o - mu
        var = seg_sum(dev * dev) * inv_n
        on = dev * lax.rsqrt(var + GN_EPS) * lnw_ref[:, lanes] + lnb_ref[:, lanes]
        bonus = seg_sum(r * kmod * rk_ref[:, lanes]) * v
        o_ref[:, lanes] = ((on + bonus) * g_ref[:, lanes]).astype(BF16)


def _wkv_call(rkv, lw, a, g, p, s_bd, *, sample, seq_len, t_blk, chunk, n_pairs):
    rows, gdim = lw.shape
    npair = gdim // LANES
    nseq = rows // seq_len
    width = n_pairs * LANES
    pair_blocks = npair // n_pairs
    head = [p["k_k"], p["k_a"], p["r_k"], p["ln_x_w"], p["ln_x_b"]]
    if sample:
        seq_blk = t_blk // seq_len
        grid = (nseq // seq_blk, pair_blocks)
        blk = lambda off: pl.BlockSpec((t_blk, width), lambda b, q, off=off: (b, off + q))
        hspec = pl.BlockSpec((1, width), lambda b, q: (0, q))
        sspec = pl.BlockSpec((seq_blk, n_pairs, LANES, LANES), lambda b, q: (b, q, 0, 0))
        extra_specs, extra_args = [sspec], [s_bd]
        sem = ("arbitrary", "arbitrary")
        scratch = []
    else:
        nblk = seq_len // t_blk
        grid = (nseq, pair_blocks, nblk)
        blk = lambda off: pl.BlockSpec((t_blk, width), lambda b, q, n, off=off: (b * nblk + n, off + q))
        hspec = pl.BlockSpec((1, width), lambda b, q, n: (0, q))
        sspec = pl.BlockSpec((1, n_pairs, LANES, LANES), lambda b, q, n: (b, q, 0, 0))
        extra_specs, extra_args = [], []
        sem = ("arbitrary", "arbitrary", "arbitrary")
        scratch = [pltpu.VMEM((n_pairs, LANES, LANES), F32)]
    in_specs = [blk(0), blk(pair_blocks), blk(2 * pair_blocks), blk(0), blk(0), blk(0)] + [hspec] * 5 + extra_specs
    args = [rkv, rkv, rkv, lw, a, g] + head + extra_args
    return pl.pallas_call(
        functools.partial(_wkv_kernel, sample=sample, chunk=chunk, n_pairs=n_pairs, n_sub=t_blk // WKV_SUB),
        grid=grid, in_specs=in_specs, out_specs=[blk(0), sspec],
        out_shape=[jax.ShapeDtypeStruct((rows, gdim), BF16),
                   jax.ShapeDtypeStruct((nseq, npair, LANES, LANES), F32)],
        scratch_shapes=scratch, compiler_params=_params(*sem), name="wkv")(*args)


def _outproj_kernel(orw_ref, ycv_ref, cg_ref, w_ref, x_ref, o_ref, lhs_s):
    g = orw_ref.shape[1]

    @pl.when(pl.program_id(1) == 0)
    def _():
        y = ycv_ref[...]
        yn = y * lax.rsqrt(jnp.mean(y * y, axis=-1, keepdims=True) + RMS_EPS) * cg_ref[...]
        lhs_s[:, :g] = orw_ref[...]
        lhs_s[:, g:] = yn.astype(BF16)

    o_ref[...] = x_ref[...] + _dot(lhs_s[...], w_ref[...])


def _outproj_call(orw, ycv, conv_g, w_out, x2d, *, tm, tn):
    rows, d = x2d.shape
    g, gc = orw.shape[1], ycv.shape[1]
    tile = pl.BlockSpec((tm, tn), lambda i, j: (i, j))
    return pl.pallas_call(
        _outproj_kernel, grid=(rows // tm, d // tn),
        in_specs=[pl.BlockSpec((tm, g), lambda i, j: (i, 0)), pl.BlockSpec((tm, gc), lambda i, j: (i, 0)),
                  pl.BlockSpec((1, gc), lambda i, j: (0, 0)), pl.BlockSpec((d, tn), lambda i, j: (0, j)), tile],
        out_specs=tile, out_shape=jax.ShapeDtypeStruct((rows, d), F32),
        scratch_shapes=[pltpu.VMEM((tm, d), BF16)],
        compiler_params=_params("arbitrary", "arbitrary"), name="outproj")(orw, ycv, conv_g, w_out, x2d)


def _norm_kernel(x_ref, g_ref, o_ref):
    x = x_ref[...]
    y = x * lax.rsqrt(jnp.mean(x * x, axis=-1, keepdims=True) + RMS_EPS) * g_ref[...]
    o_ref[...] = y.astype(o_ref.dtype)


def _norm_call(x2d, g, dtype, *, tm):
    rows, d = x2d.shape
    blk = pl.BlockSpec((tm, d), lambda i: (i, 0))
    return pl.pallas_call(
        _norm_kernel, grid=(rows // tm,), in_specs=[blk, pl.BlockSpec((1, d), lambda i: (0, 0))],
        out_specs=blk, out_shape=jax.ShapeDtypeStruct((rows, d), dtype),
        compiler_params=_params("arbitrary"), name="rmsnorm")(x2d, g)


def _ffnup_kernel(*refs, sample, tiles_per_seq):
    if sample:
        x_ref, w13_ref, cw_ref, cb_ref, e1_ref, e2_ref, h_ref, last_ref = refs
    else:
        x_ref, w13_ref, cw_ref, cb_ref, h_ref, last_ref, carry_ref = refs
    i, j = pl.program_id(0), pl.program_id(1)
    tn = h_ref.shape[1]
    uw = _dot(x_ref[...], w13_ref[...])
    u = uw[:, :tn]
    tm = u.shape[0]
    if sample:
        prev1, prev2 = _prev_rows(u, e1_ref[...], e2_ref[...], sample=True)
        last_ref[...] = u
    else:
        @pl.when(i == 0)
        def _():
            carry_ref[j] = jnp.zeros(carry_ref.shape[1:], F32)
        first1, first2 = _carry_rows(carry_ref, j, i, tiles_per_seq)
        prev1, prev2 = _prev_rows(u, first1, first2, sample=False)
        carry_ref[j] = u[tm - SUBLANES:, :]
        last_ref[...] = u[tm - SUBLANES:, :]
    z = prev2 * cw_ref[0:1, :] + prev1 * cw_ref[1:2, :] + u * cw_ref[2:3, :] + cb_ref[...]
    h_ref[...] = (z * _sigmoid(z) * uw[:, tn:]).astype(BF16)


def _ffnup_call(hn, w13, conv_w, conv_b, e1, e2, *, sample, seq_len, tm, tn):
    rows, d = hn.shape
    dff = conv_w.shape[1]
    n_i, n_j = rows // tm, dff // tn
    tile = pl.BlockSpec((tm, tn), lambda i, j: (i, j))
    in_specs = [pl.BlockSpec((tm, d), lambda i, j: (i, 0)), pl.BlockSpec((d, 2 * tn), lambda i, j: (0, j)),
                pl.BlockSpec((3, tn), lambda i, j: (0, j)), pl.BlockSpec((1, tn), lambda i, j: (0, j))]
    args = [hn, w13, conv_w, conv_b]
    if sample:
        in_specs += [tile, tile]
        args += [e1, e2]
    last_rows = tm if sample else SUBLANES
    return pl.pallas_call(
        functools.partial(_ffnup_kernel, sample=sample, tiles_per_seq=max(seq_len // tm, 1)),
        grid=(n_i, n_j), in_specs=in_specs,
        out_specs=[tile, pl.BlockSpec((last_rows, tn), lambda i, j: (i, j))],
        out_shape=[jax.ShapeDtypeStruct((rows, dff), BF16), jax.ShapeDtypeStruct((n_i * last_rows, dff), F32)],
        scratch_shapes=[] if sample else [pltpu.VMEM((n_j, SUBLANES, tn), F32)],
        compiler_params=_params("arbitrary", "arbitrary"), name="ffnup")(*args)


def _ffndown_kernel(h_ref, w_ref, x_ref, o_ref):
    o_ref[...] = x_ref[...] + _dot(h_ref[...], w_ref[...])


def _ffndown_call(h, w2, x1, *, tm, tn):
    rows, dff = h.shape
    d = w2.shape[1]
    tile = pl.BlockSpec((tm, tn), lambda i, j: (i, j))
    return pl.pallas_call(
        _ffndown_kernel, grid=(rows // tm, d // tn),
        in_specs=[pl.BlockSpec((tm, dff), lambda i, j: (i, 0)), pl.BlockSpec((dff, tn), lambda i, j: (0, j)), tile],
        out_specs=tile, out_shape=jax.ShapeDtypeStruct((rows, d), F32),
        compiler_params=_params("arbitrary", "arbitrary"), name="ffndown")(h, w2, x1)


def _tile(n, want):
    t = min(n, want)
    while n % t or (t % SUBLANES and t != n):
        t -= 1
    return t


def _col_tile(n, want):
    t = min(n, want)
    while n % t or t % LANES:
        t -= LANES
    return t


FFN_TN = 256


def _state_rows(state, taps, seq_len):
    nseq, _, n = state.shape
    z = jnp.zeros((nseq, seq_len, n), state.dtype)
    e1 = z.at[:, 0].set(state[:, taps - 1])
    e2 = z.at[:, 0].set(state[:, taps - 2]).at[:, 1].set(state[:, taps - 1]) if taps > 1 else None
    flat = lambda e: None if e is None else e.reshape(nseq * seq_len, n)
    return flat(e1), flat(e2)


def _layer(x, states, p, *, sample):
    nseq, seq_len, d = x.shape
    rows = nseq * seq_len
    g = p["dw2"].shape[1]
    gc = d - g
    dff = p["ffn_conv_w"].shape[1]
    npair = g // LANES
    x2d = x.reshape(rows, d)
    big = dict(sample=sample, seq_len=seq_len)
    tm_big = rows if sample else _tile(seq_len, 1024)

    if sample:
        shift, wkv, conv, ffn = states
        ext_x, _ = _state_rows(shift[:, None, :], 1, seq_len)
        p0 = _matmul_call(shift.astype(BF16), p["w_in"], 3 * g, _col_tile(3 * g, 512))
        ext_p, _ = _state_rows(p0[:, None, :], 1, seq_len)
        ce1, ce2 = _state_rows(conv, 2, seq_len)
        fe1, fe2 = _state_rows(ffn, 2, seq_len)
        s5 = wkv.reshape(nseq, npair, 2, HEAD_DIM, HEAD_DIM)
        zero = jnp.zeros_like(s5[:, :, 0])
        s_bd = jnp.concatenate([jnp.concatenate([s5[:, :, 0], zero], -1),
                                jnp.concatenate([zero, s5[:, :, 1]], -1)], -2)
    else:
        ext_x = ext_p = ce1 = ce2 = fe1 = fe2 = s_bd = None

    tm_mix = _tile(rows if sample else seq_len, 128)
    xnb, lw, a, gate, xlast = _mix_call(x2d, ext_x, p, sample=sample, seq_len=seq_len, tm=tm_mix)
    rkv = _rkv_call(xnb, p["w_in"], p["mu_rkv"], ext_p, tm=tm_big, tn=_col_tile(3 * g, 512), **big)
    tm_conv = _tile(rows if sample else seq_len, 512)
    ycv, cxlast = _convbr_call(xnb, p["w_in"], p["conv_w"], ce1, ce2, col0=3 * g, tm=tm_conv,
                               tn=_col_tile(gc, 512), **big)
    if sample:
        t_blk, chunk = WKV_SUB, seq_len
    else:
        chunk = WKV_SUB
        t_blk = _tile(seq_len, 4 * WKV_SUB)
    orw, s_new = _wkv_call(rkv, lw, a, gate, p, s_bd, t_blk=t_blk, chunk=chunk,
                           n_pairs=2 if npair % 2 == 0 else 1, **big)
    tm_e = _tile(rows, 512)
    x1 = _outproj_call(orw, ycv, p["conv_norm_g"], p["w_out"], x2d, tm=tm_e, tn=_col_tile(d, 512))
    hn = _norm_call(x1, p["norm2_g"], BF16, tm=_tile(rows, 256))
    h, ulast = _ffnup_call(hn, p["ffn_w13"], p["ffn_conv_w"], p["ffn_conv_b"], fe1, fe2,
                           tm=tm_big, tn=FFN_TN, **big)
    x2 = _ffndown_call(h, p["ffn_w2"], x1, tm=tm_e, tn=_col_tile(d, 256))

    def last_rows(arr, tile_rows, k):
        if sample:
            return arr.reshape(nseq, seq_len, -1)[:, seq_len - k:]
        per_seq = seq_len // tile_rows
        return arr.reshape(nseq, per_seq, SUBLANES, -1)[:, -1, SUBLANES - k:]
    new_shift = last_rows(xlast, tm_mix, 1)[:, 0]
    new_conv = last_rows(cxlast, tm_conv, 2)
    new_ffn = last_rows(ulast, tm_big, 2)
    new_wkv = jnp.stack([s_new[:, :, :HEAD_DIM, :HEAD_DIM], s_new[:, :, HEAD_DIM:, HEAD_DIM:]], axis=2)
    new_wkv = new_wkv.reshape(nseq, 2 * npair, HEAD_DIM, HEAD_DIM)
    return x2.reshape(nseq, seq_len, d), new_shift, new_wkv, new_conv, new_ffn


def _pad_to(a, axis, mult):
    pad = (-a.shape[axis]) % mult
    if not pad:
        return a
    widths = [(0, 0)] * a.ndim
    widths[axis] = (0, pad)
    return jnp.pad(a, widths)


def _interleave_cols(w1, w3, tn):
    d, n = w1.shape
    both = jnp.stack([w1.reshape(d, n // tn, tn), w3.reshape(d, n // tn, tn)], axis=2)
    return both.astype(BF16).reshape(d, 2 * n)


def kernel(x_prompt, x_sample, state_shift, state_wkv, state_conv, state_ffn, norm1_g, w_in, mu_rkv, mu_lora, decay_w0, decay_w1, decay_w2, aaa_a0, aaa_a1, aaa_a2, gate_g1, gate_g2, k_k, k_a, r_k, ln_x_w, ln_x_b, conv_w, conv_norm_g, w_out, norm2_g, ffn_w1, ffn_conv_w, ffn_conv_b, ffn_w3, ffn_w2, final_norm_g):
    depth = w_in.shape[0]
    d = x_prompt.shape[-1]
    row = lambda v: v.reshape(1, -1).astype(F32)
    yp, ys = x_prompt, x_sample
    outs_p, outs_s = [], []
    for l in range(depth):
        p = dict(
            norm1_g=row(norm1_g[l]), w_in=w_in[l].astype(BF16), mu_rkv=row(mu_rkv[l]), mu_lora=mu_lora[l],
            decay_w0=row(decay_w0[l]), aaa_a0=row(aaa_a0[l]),
            dw1=decay_w1[l].astype(BF16), dw2=decay_w2[l].astype(BF16),
            aw1=aaa_a1[l].astype(BF16), aw2=aaa_a2[l].astype(BF16),
            gw1=_pad_to(gate_g1[l], 1, LANES).astype(BF16), gw2=_pad_to(gate_g2[l], 0, LANES).astype(BF16),
            k_k=row(k_k[l]), k_a=row(k_a[l]), r_k=row(r_k[l]), ln_x_w=row(ln_x_w[l]), ln_x_b=row(ln_x_b[l]),
            conv_w=conv_w[l], conv_norm_g=row(conv_norm_g[l]), w_out=w_out[l].astype(BF16),
            norm2_g=row(norm2_g[l]), ffn_w13=_interleave_cols(ffn_w1[l], ffn_w3[l], FFN_TN),
            ffn_conv_w=ffn_conv_w[l], ffn_conv_b=row(ffn_conv_b[l]), ffn_w2=ffn_w2[l].astype(BF16))
        yp, *st_p = _layer(yp, None, p, sample=False)
        ys, *st_s = _layer(ys, (state_shift[l], state_wkv[l], state_conv[l], state_ffn[l]), p, sample=True)
        outs_p.append(st_p)
        outs_s.append(st_s)
    fin = row(final_norm_g)
    y_prompt = _norm_call(yp.reshape(-1, d), fin, F32, tm=_tile(yp.shape[0] * yp.shape[1], 256)).reshape(yp.shape)
    y_sample = _norm_call(ys.reshape(-1, d), fin, F32, tm=_tile(ys.shape[0] * ys.shape[1], 256)).reshape(ys.shape)
    stack = lambda outs, k: jnp.stack([o[k] for o in outs])
    return (y_prompt, y_sample,
            stack(outs_p, 0), stack(outs_p, 1), stack(outs_p, 2), stack(outs_p, 3),
            stack(outs_s, 0), stack(outs_s, 1), stack(outs_s, 2), stack(outs_s, 3))
```

```python
import functools
import math

import jax
import jax.numpy as jnp
from jax import lax
from jax.experimental import pallas as pl
from jax.experimental.pallas import tpu as pltpu

F32 = jnp.float32
BF16 = jnp.bfloat16

HEAD_DIM = 64
LANES = 128
SUBLANES = 8
WKV_SUB = 64
RMS_EPS = 1e-6
GN_EPS = 64e-5
VMEM_LIMIT_BYTES = 56 * 1024 * 1024


def _params(*sem):
    return pltpu.CompilerParams(dimension_semantics=sem, vmem_limit_bytes=VMEM_LIMIT_BYTES)


def _dot(a, b):
    return jnp.dot(a, b, preferred_element_type=F32)


def _dot_nt(a, b):
    return lax.dot_general(a, b, (((1,), (1,)), ((), ())), preferred_element_type=F32)


def _dot_tn(a, b):
    return lax.dot_general(a, b, (((0,), (0,)), ((), ())), preferred_element_type=F32)


def _sigmoid(z):
    return 1.0 / (1.0 + jnp.exp(-z))


def _split_dot(x, b_exact, terms, dot=_dot):
    acc = None
    rem = x
    for _ in range(terms):
        hi = rem.astype(BF16)
        part = dot(hi, b_exact)
        acc = part if acc is None else acc + part
        rem = rem - hi.astype(F32)
    return acc


def _row_iota(shape):
    return lax.broadcasted_iota(jnp.int32, shape, 0)


def _prev_rows(p, first1, first2, *, sample):
    row = _row_iota(p.shape)
    r1 = pltpu.roll(p, 1, 0)
    r2 = pltpu.roll(p, 2, 0)
    if sample:
        t = row % SUBLANES
        return jnp.where(t == 0, first1, r1), jnp.where(t < 2, first2, r2)
    prev1 = jnp.where(row == 0, first1, r1)
    prev2 = jnp.where(row == 0, first2, jnp.where(row == 1, first1, r2))
    return prev1, prev2


def _carry_rows(carry_ref, j, i, tiles_per_seq):
    c = carry_ref[j]
    keep = (i % tiles_per_seq) != 0
    first1 = jnp.where(keep, c[SUBLANES - 1:SUBLANES, :], 0.0)
    first2 = jnp.where(keep, c[SUBLANES - 2:SUBLANES - 1, :], 0.0)
    return first1, first2


def _expand_state_rows(st1, st0, tm):
    ns = st1.shape[0]
    row = _row_iota((tm, ns))
    col = lax.broadcasted_iota(jnp.int32, (tm, ns), 1) * SUBLANES
    place = lambda e, x: _split_dot(x, e.astype(BF16), 3, dot=lambda x_, b: _dot(b, x_))
    first1 = place(row == col, st1)
    if st0 is None:
        return first1, None
    return first1, place(row == col, st0) + place(row == col + 1, st1)


def _mix_kernel(*refs, sample, tiles_per_seq):
    if sample:
        (x_ref, ext_ref, g1_ref, mu_ref, w0_ref, a0_ref, dw1_ref, aw1_ref, gw1_ref, dw2_ref,
         aw2_ref, gw2_ref, xnb_ref, lw_ref, a_ref, g_ref, last_ref) = refs
    else:
        (x_ref, g1_ref, mu_ref, w0_ref, a0_ref, dw1_ref, aw1_ref, gw1_ref, dw2_ref,
         aw2_ref, gw2_ref, xnb_ref, lw_ref, a_ref, g_ref, last_ref, carry_ref) = refs
    i = pl.program_id(0)
    x = x_ref[...]
    xn = x * lax.rsqrt(jnp.mean(x * x, axis=-1, keepdims=True) + RMS_EPS) * g1_ref[...]
    rolled = pltpu.roll(xn, 1, 0)
    row = _row_iota(xn.shape)
    tm = xn.shape[0]
    if sample:
        first, _unused = _expand_state_rows(ext_ref[...], None, tm)
        prev = jnp.where(row % SUBLANES == 0, first, rolled)
        last_ref[...] = xn
    else:
        @pl.when(i == 0)
        def _():
            carry_ref[...] = jnp.zeros_like(carry_ref)
        first = jnp.where((i % tiles_per_seq) != 0, carry_ref[SUBLANES - 1:SUBLANES, :], 0.0)
        prev = jnp.where(row == 0, first, rolled)
        carry_ref[...] = xn[tm - SUBLANES:, :]
        last_ref[...] = xn[tm - SUBLANES:, :]
    dx = prev - xn
    xnb_ref[...] = xn.astype(BF16)
    xw = (xn + dx * mu_ref[0:1, :]).astype(BF16)
    hw = jnp.tanh(_dot(xw, dw1_ref[...]))
    wl = w0_ref[...] + _dot(hw.astype(BF16), dw2_ref[...])
    lw_ref[...] = -_sigmoid(wl) * math.exp(-0.5)
    xa = (xn + dx * mu_ref[1:2, :]).astype(BF16)
    ha = _dot(xa, aw1_ref[...])
    a_ref[...] = _sigmoid(a0_ref[...] + _dot(ha.astype(BF16), aw2_ref[...]))
    xg = (xn + dx * mu_ref[2:3, :]).astype(BF16)
    hg = _sigmoid(_dot(xg, gw1_ref[...]))
    g_ref[...] = _dot(hg.astype(BF16), gw2_ref[...])


def _mix_call(x2d, ext, p, *, sample, seq_len, tm):
    rows, d = x2d.shape
    g = p["dw2"].shape[1]
    n_i = rows // tm
    row_blk = lambda w: pl.BlockSpec((tm, w), lambda i: (i, 0))
    full = lambda a: pl.BlockSpec(a.shape, lambda i: (0, 0))
    weights = [p["norm1_g"], p["mu_lora"], p["decay_w0"], p["aaa_a0"], p["dw1"], p["aw1"], p["gw1"],
               p["dw2"], p["aw2"], p["gw2"]]
    state_blk = pl.BlockSpec((tm // seq_len, d), lambda i: (i, 0))
    in_specs = [row_blk(d)] + ([state_blk] if sample else []) + [full(w) for w in weights]
    args = [x2d] + ([ext] if sample else []) + weights
    last_rows = tm if sample else SUBLANES
    out_shape = [jax.ShapeDtypeStruct((rows, d), BF16)] + [jax.ShapeDtypeStruct((rows, g), F32)] * 3 + [
        jax.ShapeDtypeStruct((n_i * last_rows, d), F32)]
    out_specs = [row_blk(d), row_blk(g), row_blk(g), row_blk(g), pl.BlockSpec((last_rows, d), lambda i: (i, 0))]
    return pl.pallas_call(
        functools.partial(_mix_kernel, sample=sample, tiles_per_seq=max(seq_len // tm, 1)),
        grid=(n_i,), in_specs=in_specs, out_specs=out_specs, out_shape=out_shape,
        scratch_shapes=[] if sample else [pltpu.VMEM((SUBLANES, d), F32)],
        compiler_params=_params("arbitrary"), name="mix")(*args)


def _rkv_kernel(*refs, sample, tiles_per_seq):
    if sample:
        x_ref, w_ref, mu_ref, ext_ref, o_ref = refs
    else:
        x_ref, w_ref, mu_ref, o_ref, carry_ref = refs
    i, j = pl.program_id(0), pl.program_id(1)
    p = _dot(x_ref[...], w_ref[...])
    tm = p.shape[0]
    if sample:
        first, _unused = _expand_state_rows(ext_ref[...], None, tm)
        prev = jnp.where(_row_iota(p.shape) % SUBLANES == 0, first, pltpu.roll(p, 1, 0))
    else:
        @pl.when(i == 0)
        def _():
            carry_ref[j] = jnp.zeros(carry_ref.shape[1:], F32)
        first1, _unused = _carry_rows(carry_ref, j, i, tiles_per_seq)
        prev = jnp.where(_row_iota(p.shape) == 0, first1, pltpu.roll(p, 1, 0))
        carry_ref[j] = p[tm - SUBLANES:, :]
    o_ref[...] = p + mu_ref[...] * (prev - p)


def _rkv_call(xnb, w_in, mu, ext, *, sample, seq_len, tm, tn):
    rows, d = xnb.shape
    n = mu.shape[1]
    n_i, n_j = rows // tm, n // tn
    in_specs = [pl.BlockSpec((tm, d), lambda i, j: (i, 0)), pl.BlockSpec((d, tn), lambda i, j: (0, j)),
                pl.BlockSpec((1, tn), lambda i, j: (0, j))]
    args = [xnb, w_in, mu]
    if sample:
        in_specs.append(pl.BlockSpec((tm // seq_len, tn), lambda i, j: (i, j)))
        args.append(ext)
    return pl.pallas_call(
        functools.partial(_rkv_kernel, sample=sample, tiles_per_seq=max(seq_len // tm, 1)),
        grid=(n_i, n_j), in_specs=in_specs, out_specs=pl.BlockSpec((tm, tn), lambda i, j: (i, j)),
        out_shape=jax.ShapeDtypeStruct((rows, n), F32),
        scratch_shapes=[] if sample else [pltpu.VMEM((n_j, SUBLANES, tn), F32)],
        compiler_params=_params("arbitrary", "arbitrary"), name="rkv")(*args)


def _matmul_kernel(x_ref, w_ref, o_ref):
    o_ref[...] = _dot(x_ref[...], w_ref[...])


def _matmul_call(x, w, n, tn):
    rows, d = x.shape
    return pl.pallas_call(
        _matmul_kernel, grid=(n // tn,),
        in_specs=[pl.BlockSpec((rows, d), lambda j: (0, 0)), pl.BlockSpec((d, tn), lambda j: (0, j))],
        out_specs=pl.BlockSpec((rows, tn), lambda j: (0, j)),
        out_shape=jax.ShapeDtypeStruct((rows, n), F32),
        compiler_params=_params("arbitrary"), name="shift_proj")(x, w)


def _convbr_kernel(*refs, sample, tiles_per_seq):
    if sample:
        x_ref, wb_ref, wc_ref, wx_ref, cw_ref, st1_ref, st0_ref, y_ref, last_ref = refs
    else:
        x_ref, wb_ref, wc_ref, wx_ref, cw_ref, y_ref, last_ref, carry_ref = refs
    i, j = pl.program_id(0), pl.program_id(1)
    x = x_ref[...]
    cx = _dot(x, wc_ref[...]) * _dot(x, wx_ref[...])
    tm = cx.shape[0]
    if sample:
        first1, first2 = _expand_state_rows(st1_ref[...], st0_ref[...], tm)
        prev1, prev2 = _prev_rows(cx, first1, first2, sample=True)
        last_ref[...] = cx
    else:
        @pl.when(i == 0)
        def _():
            carry_ref[j] = jnp.zeros(carry_ref.shape[1:], F32)
        first1, first2 = _carry_rows(carry_ref, j, i, tiles_per_seq)
        prev1, prev2 = _prev_rows(cx, first1, first2, sample=False)
        carry_ref[j] = cx[tm - SUBLANES:, :]
        last_ref[...] = cx[tm - SUBLANES:, :]
    hconv = prev2 * cw_ref[0:1, :] + prev1 * cw_ref[1:2, :] + cx * cw_ref[2:3, :]
    y_ref[...] = _dot(x, wb_ref[...]) * hconv


def _convbr_call(xnb, w_in, conv_w, e1, e2, *, col0, sample, seq_len, tm, tn):
    rows, d = xnb.shape
    gc = conv_w.shape[1]
    n_i, n_j = rows // tm, gc // tn
    off = col0 // tn
    nb = gc // tn
    wspec = lambda k: pl.BlockSpec((d, tn), lambda i, j: (0, off + k * nb + j))
    tile = pl.BlockSpec((tm, tn), lambda i, j: (i, j))
    in_specs = [pl.BlockSpec((tm, d), lambda i, j: (i, 0)), wspec(0), wspec(1), wspec(2),
                pl.BlockSpec((3, tn), lambda i, j: (0, j))]
    args = [xnb, w_in, w_in, w_in, conv_w]
    if sample:
        sspec = pl.BlockSpec((tm // seq_len, tn), lambda i, j: (i, j))
        in_specs += [sspec, sspec]
        args += [e1, e2]
    last_rows = tm if sample else SUBLANES
    return pl.pallas_call(
        functools.partial(_convbr_kernel, sample=sample, tiles_per_seq=max(seq_len // tm, 1)),
        grid=(n_i, n_j), in_specs=in_specs,
        out_specs=[tile, pl.BlockSpec((last_rows, tn), lambda i, j: (i, j))],
        out_shape=[jax.ShapeDtypeStruct((rows, gc), F32), jax.ShapeDtypeStruct((n_i * last_rows, gc), F32)],
        scratch_shapes=[] if sample else [pltpu.VMEM((n_j, SUBLANES, tn), F32)],
        compiler_params=_params("arbitrary", "arbitrary"), name="convbr")(*args)


def _wkv_masks(c):
    m = 2 * WKV_SUB
    r2 = jnp.bitwise_and(_row_iota((m, m)), WKV_SUB - 1)
    c2 = jnp.bitwise_and(lax.broadcasted_iota(jnp.int32, (m, m), 1), WKV_SUB - 1)
    shift = int(math.log2(c))
    same = jnp.right_shift(r2, shift) == jnp.right_shift(c2, shift)
    eye = (_row_iota((m, m)) == lax.broadcasted_iota(jnp.int32, (m, m), 1)).astype(F32)
    lane_lo = lax.broadcasted_iota(jnp.int32, (WKV_SUB, LANES), 1) < HEAD_DIM
    return lane_lo, same & (c2 < r2), same & (c2 <= r2), eye


def _stack_heads(x, lane_lo):
    z = jnp.zeros_like(x)
    return jnp.concatenate([jnp.where(lane_lo, x, z), jnp.where(lane_lo, z, x)], axis=0)


def _fold_heads(x):
    return x[:WKV_SUB] + x[WKV_SUB:]


def _wkv_phase1(units, masks, c):
    lane_lo, strict, incl, eye = masks
    m = 2 * WKV_SUB
    phs = []
    for r, kt, v, kp, bt, lg, lw in units:
        e1 = jnp.exp(lg)
        e0 = jnp.exp(lg - lw)
        ei = jnp.exp(-lg)
        phs.append(dict(
            e1=e1, rt_st=_stack_heads(r * e1, lane_lo), kp_b=_stack_heads(kp * e0, lane_lo).astype(BF16),
            kh_st=_stack_heads(kt * ei, lane_lo), bh_st=_stack_heads(bt * ei, lane_lo),
            v_st=_stack_heads(v, lane_lo)))
    for ph in phs:
        ph["kh_b"], ph["bh_b"], ph["v_b"] = (ph[n].astype(BF16) for n in ("kh_st", "bh_st", "v_st"))
    gs = [_dot_nt(jnp.concatenate([ph["kp_b"], ph["rt_st"].astype(BF16)], axis=0),
                  jnp.concatenate([ph["bh_b"], ph["kh_b"]], axis=0)) for ph in phs]
    ps = [-jnp.where(strict, g[:m, :m], 0.0) for g in gs]
    ts = [eye + p for p in ps]
    n = 1
    while 2 * n < c:
        pbs = [p.astype(BF16) for p in ps]
        ps = [_dot(pb, pb) for pb in pbs]
        ts = [t + _dot(t.astype(BF16), p.astype(BF16)) for t, p in zip(ts, ps)]
        n *= 2
    abvs = [_dot(jnp.concatenate([jnp.where(strict, g[:m, m:], 0.0), jnp.where(incl, g[m:, m:], 0.0)],
                                 axis=0).astype(BF16), ph["v_b"]) for g, ph in zip(gs, phs)]
    tts = [_dot(t.astype(BF16), jnp.concatenate([ph["kp_b"], abv[:m].astype(BF16)], axis=1))
           for t, ph, abv in zip(ts, phs, abvs)]
    bbtts = [_dot(jnp.where(incl, g[m:, :m], 0.0).astype(BF16), tt.astype(BF16)) for g, tt in zip(gs, tts)]
    for ph, abv, tt, bbtt in zip(phs, abvs, tts, bbtts):
        ph["rq"] = _fold_heads(ph["rt_st"] - bbtt[:, :LANES])
        ph["ov"] = _fold_heads(abv[m:] - bbtt[:, LANES:])
        ph["tk_st"], ph["tav_st"] = tt[:, :LANES], tt[:, LANES:]
    return phs


def _wkv_transitions(phs):
    gams = [ph["e1"][WKV_SUB - 1:WKV_SUB, :] for ph in phs]
    kgs = [(_dot_tn(ph["tk_st"].astype(BF16), ph["bh_b"]) * gam).astype(BF16) for ph, gam in zip(phs, gams)]
    bcgs = [_dot_tn(jnp.concatenate([ph["v_b"], (-ph["tav_st"]).astype(BF16)], axis=0),
                    jnp.concatenate([ph["kh_b"], ph["bh_b"]], axis=0)) * gam for ph, gam in zip(phs, gams)]
    return gams, kgs, bcgs


def _wkv_units_small(states, units, c, lane_lo_c):
    rows = lambda u: slice(u * c, (u + 1) * c)
    rows_hi = lambda u: slice(WKV_SUB + u * c, WKV_SUB + (u + 1) * c)
    ous = [_dot_nt(jnp.concatenate([ph["rq"][rows(u)], ph["tkm"][rows(u)]], axis=0).astype(BF16), s.astype(BF16))
           for s, (ph, u) in zip(states, units)]
    outs, new_states = [], []
    for s, (ph, u), ou in zip(states, units, ous):
        pick = lambda x: jnp.concatenate([x[rows(u)], x[rows_hi(u)]], axis=0)
        uu = ou[c:] + ph["tav"][rows(u)]
        z = jnp.zeros_like(uu)
        u_st = jnp.concatenate([jnp.where(lane_lo_c, uu, z), jnp.where(lane_lo_c, z, uu)], axis=0)
        gam = ph["e1"][(u + 1) * c - 1:(u + 1) * c, :]
        lhs = jnp.concatenate([pick(ph["v_st"]), -u_st], axis=0).astype(BF16)
        rhs = (jnp.concatenate([pick(ph["kh_st"]), pick(ph["bh_st"])], axis=0) * gam).astype(BF16)
        outs.append(ou[:c] + ph["ov"][rows(u)])
        new_states.append(s * gam + _dot_tn(lhs, rhs))
    return outs, new_states


def _wkv_kernel(*refs, sample, chunk, n_pairs, n_sub):
    if sample:
        (r_ref, k_ref, v_ref, lw_ref, a_ref, g_ref, kk_ref, ka_ref, rk_ref, lnw_ref, lnb_ref, sin_ref,
         o_ref, sout_ref) = refs
    else:
        (r_ref, k_ref, v_ref, lw_ref, a_ref, g_ref, kk_ref, ka_ref, rk_ref, lnw_ref, lnb_ref,
         o_ref, sout_ref, s_s) = refs
    c = chunk
    lane128 = lax.broadcasted_iota(jnp.int32, (LANES, LANES), 1)
    row128 = _row_iota((LANES, LANES))
    blockdiag = (row128 < HEAD_DIM) == (lane128 < HEAD_DIM)
    block_ones = blockdiag.astype(BF16)
    seg_sum = lambda z: _split_dot(z, block_ones, 2)
    dup = (_row_iota((HEAD_DIM, LANES))
           == jnp.bitwise_and(lax.broadcasted_iota(jnp.int32, (HEAD_DIM, LANES), 1), HEAD_DIM - 1)).astype(BF16)
    fold = (jnp.bitwise_and(_row_iota((LANES, HEAD_DIM)), HEAD_DIM - 1)
            == lax.broadcasted_iota(jnp.int32, (LANES, HEAD_DIM), 1)).astype(BF16)
    to_blockdiag = lambda x: jnp.where(blockdiag, _split_dot(x, dup, 3), 0.0)
    from_blockdiag = lambda s_: _split_dot(s_, fold, 3)
    masks = _wkv_masks(c)
    r2 = jnp.bitwise_and(_row_iota((WKV_SUB, WKV_SUB)), WKV_SUB - 1)
    c2 = lax.broadcasted_iota(jnp.int32, (WKV_SUB, WKV_SUB), 1)
    shift = int(math.log2(c))
    tril_b = ((jnp.right_shift(r2, shift) == jnp.right_shift(c2, shift)) & (c2 <= r2)).astype(BF16)
    lane_lo_c = lax.broadcasted_iota(jnp.int32, (c, LANES), 1) < HEAD_DIM

    if not sample:
        @pl.when(pl.program_id(2) == 0)
        def _():
            s_s[...] = jnp.zeros_like(s_s)

    lgs = []
    for sb in range(n_sub):
        rows = slice(sb * WKV_SUB, (sb + 1) * WKV_SUB)
        lgs.append(_split_dot(lw_ref[rows, :], tril_b, 3, dot=lambda x, b: _dot(b, x)))

    pairs = range(n_pairs)
    lanes = [slice(q * LANES, (q + 1) * LANES) for q in pairs]
    r = [r_ref[:, l] for l in lanes]
    k = [k_ref[:, l] for l in lanes]
    v = [v_ref[:, l] for l in lanes]
    a = [a_ref[:, l] for l in lanes]
    lw = [lw_ref[:, l] for l in lanes]
    kk = [k[q] * kk_ref[:, lanes[q]] for q in pairs]
    norms = [seg_sum(x * x) for x in kk]
    kk = [x / jnp.maximum(jnp.sqrt(n2), 1e-12) for x, n2 in zip(kk, norms)]
    kmod = [k[q] * (1.0 + (a[q] - 1.0) * ka_ref[:, lanes[q]]) for q in pairs]
    bt = [kk[q] * a[q] for q in pairs]

    subs = [(q, sb) for sb in range(n_sub) for q in pairs]
    units = []
    for q, sb in subs:
        rows = slice(sb * WKV_SUB, (sb + 1) * WKV_SUB)
        units.append((r[q][rows], kmod[q][rows], v[q][rows], kk[q][rows], bt[q][rows],
                      lgs[sb][:, lanes[q]], lw[q][rows]))
    phs = dict(zip(subs, _wkv_phase1(units, masks, c)))

    outs = {q: [] for q in pairs}
    if sample:
        per_sub = WKV_SUB // c
        todo = []
        for (q, sb), ph in phs.items():
            ph["tkm"] = _fold_heads(ph["tk_st"])
            ph["tav"] = _fold_heads(ph["tav_st"])
            todo += [(q, sb * per_sub + u, ph, u) for u in range(per_sub)]
        o_units, new_states = _wkv_units_small([to_blockdiag(sin_ref[seq, q]) for q, seq, _, _ in todo],
                                               [(ph, u) for _, _, ph, u in todo], c, lane_lo_c)
        for (q, seq, _, _), o, s_new in zip(todo, o_units, new_states):
            sout_ref[seq, q] = from_blockdiag(s_new)
            outs[q].append(o)
    else:
        gams, kgs, bcgs = _wkv_transitions([phs[key] for key in subs])
        trans = dict(zip(subs, zip(gams, kgs, bcgs)))
        s = [s_s[q] for q in pairs]
        for sb in range(n_sub):
            sb16 = [x.astype(BF16) for x in s]
            for q in pairs:
                outs[q].append(_dot_nt(phs[q, sb]["rq"].astype(BF16), sb16[q]) + phs[q, sb]["ov"])
            s = [s[q] * trans[q, sb][0] - _dot(sb16[q], trans[q, sb][1]) + trans[q, sb][2] for q in pairs]
        for q in pairs:
            s_s[q] = s[q]

        @pl.when(pl.program_id(2) == pl.num_programs(2) - 1)
        def _():
            for q in pairs:
                sout_ref[0, q] = from_blockdiag(s[q])

    o = [jnp.concatenate(outs[q], axis=0) for q in pairs]
    inv_n = 1.0 / HEAD_DIM
    mu = [seg_sum(x) * inv_n for x in o]
    dev = [x - m_ for x, m_ in zip(o, mu)]
    var = [seg_sum(x * x) * inv_n for x in dev]
    bonus = [seg_sum(r[q] * kmod[q] * rk_ref[:, lanes[q]]) * v[q] for q in pairs]
    for q in pairs:
        on = dev[q] * lax.rsqrt(var[q] + GN_EPS) * lnw_ref[:, lanes[q]] + lnb_ref[:, lanes[q]]
        o_ref[:, lanes[q]] = ((on + bonus[q]) * g_ref[:, lanes[q]]).astype(BF16)


def _wkv_call(rkv, lw, a, g, p, s_bd, *, sample, seq_len, t_blk, chunk, n_pairs):
    rows, gdim = lw.shape
    npair = gdim // LANES
    nseq = rows // seq_len
    width = n_pairs * LANES
    pair_blocks = npair // n_pairs
    head = [p["k_k"], p["k_a"], p["r_k"], p["ln_x_w"], p["ln_x_b"]]
    if sample:
        seq_blk = t_blk // seq_len
        grid = (nseq // seq_blk, pair_blocks)
        blk = lambda off: pl.BlockSpec((t_blk, width), lambda b, q, off=off: (b, off + q))
        hspec = pl.BlockSpec((1, width), lambda b, q: (0, q))
        sspec = pl.BlockSpec((seq_blk, n_pairs, LANES, HEAD_DIM), lambda b, q: (b, q, 0, 0))
        extra_specs, extra_args = [sspec], [s_bd]
        sem = ("arbitrary", "arbitrary")
        scratch = []
    else:
        nblk = seq_len // t_blk
        grid = (nseq, pair_blocks, nblk)
        blk = lambda off: pl.BlockSpec((t_blk, width), lambda b, q, n, off=off: (b * nblk + n, off + q))
        hspec = pl.BlockSpec((1, width), lambda b, q, n: (0, q))
        sspec = pl.BlockSpec((1, n_pairs, LANES, HEAD_DIM), lambda b, q, n: (b, q, 0, 0))
        extra_specs, extra_args = [], []
        sem = ("arbitrary", "arbitrary", "arbitrary")
        scratch = [pltpu.VMEM((n_pairs, LANES, LANES), F32)]
    in_specs = [blk(0), blk(pair_blocks), blk(2 * pair_blocks), blk(0), blk(0), blk(0)] + [hspec] * 5 + extra_specs
    args = [rkv, rkv, rkv, lw, a, g] + head + extra_args
    return pl.pallas_call(
        functools.partial(_wkv_kernel, sample=sample, chunk=chunk, n_pairs=n_pairs, n_sub=t_blk // WKV_SUB),
        grid=grid, in_specs=in_specs, out_specs=[blk(0), sspec],
        out_shape=[jax.ShapeDtypeStruct((rows, gdim), BF16),
                   jax.ShapeDtypeStruct((nseq, npair, LANES, HEAD_DIM), F32)],
        scratch_shapes=scratch, compiler_params=_params(*sem), name="wkv")(*args)


def _outproj_kernel(orw_ref, ycv_ref, cg_ref, w_ref, x_ref, o_ref, lhs_s):
    g = orw_ref.shape[1]

    @pl.when(pl.program_id(1) == 0)
    def _():
        y = ycv_ref[...]
        yn = y * lax.rsqrt(jnp.mean(y * y, axis=-1, keepdims=True) + RMS_EPS) * cg_ref[...]
        lhs_s[:, :g] = orw_ref[...]
        lhs_s[:, g:] = yn.astype(BF16)

    o_ref[...] = x_ref[...] + _dot(lhs_s[...], w_ref[...])


def _outproj_call(orw, ycv, conv_g, w_out, x2d, *, tm, tn):
    rows, d = x2d.shape
    g, gc = orw.shape[1], ycv.shape[1]
    tile = pl.BlockSpec((tm, tn), lambda i, j: (i, j))
    return pl.pallas_call(
        _outproj_kernel, grid=(rows // tm, d // tn),
        in_specs=[pl.BlockSpec((tm, g), lambda i, j: (i, 0)), pl.BlockSpec((tm, gc), lambda i, j: (i, 0)),
                  pl.BlockSpec((1, gc), lambda i, j: (0, 0)), pl.BlockSpec((d, tn), lambda i, j: (0, j)), tile],
        out_specs=tile, out_shape=jax.ShapeDtypeStruct((rows, d), F32),
        scratch_shapes=[pltpu.VMEM((tm, d), BF16)],
        compiler_params=_params("arbitrary", "arbitrary"), name="outproj")(orw, ycv, conv_g, w_out, x2d)


def _norm_kernel(x_ref, g_ref, o_ref):
    x = x_ref[...]
    y = x * lax.rsqrt(jnp.mean(x * x, axis=-1, keepdims=True) + RMS_EPS) * g_ref[...]
    o_ref[...] = y.astype(o_ref.dtype)


def _norm_call(x2d, g, dtype, *, tm):
    rows, d = x2d.shape
    blk = pl.BlockSpec((tm, d), lambda i: (i, 0))
    return pl.pallas_call(
        _norm_kernel, grid=(rows // tm,), in_specs=[blk, pl.BlockSpec((1, d), lambda i: (0, 0))],
        out_specs=blk, out_shape=jax.ShapeDtypeStruct((rows, d), dtype),
        compiler_params=_params("arbitrary"), name="rmsnorm")(x2d, g)


def _ffnup_kernel(*refs, sample, tiles_per_seq):
    if sample:
        x_ref, w1_ref, w3_ref, cw_ref, cb_ref, st1_ref, st0_ref, h_ref, last_ref = refs
    else:
        x_ref, w1_ref, w3_ref, cw_ref, cb_ref, h_ref, last_ref, carry_ref = refs
    i, j = pl.program_id(0), pl.program_id(1)
    tn = h_ref.shape[1]
    uw = _dot(x_ref[...], jnp.concatenate([w1_ref[...], w3_ref[...]], axis=1))
    u = uw[:, :tn]
    tm = u.shape[0]
    if sample:
        first1, first2 = _expand_state_rows(st1_ref[...], st0_ref[...], tm)
        prev1, prev2 = _prev_rows(u, first1, first2, sample=True)
        last_ref[...] = u
    else:
        @pl.when(i == 0)
        def _():
            carry_ref[j] = jnp.zeros(carry_ref.shape[1:], F32)
        first1, first2 = _carry_rows(carry_ref, j, i, tiles_per_seq)
        prev1, prev2 = _prev_rows(u, first1, first2, sample=False)
        carry_ref[j] = u[tm - SUBLANES:, :]
        last_ref[...] = u[tm - SUBLANES:, :]
    z = prev2 * cw_ref[0:1, :] + prev1 * cw_ref[1:2, :] + u * cw_ref[2:3, :] + cb_ref[...]
    h_ref[...] = (z * _sigmoid(z) * uw[:, tn:]).astype(BF16)


def _ffnup_call(hn, w1, w3, conv_w, conv_b, st1, st0, *, sample, seq_len, tm, tn):
    rows, d = hn.shape
    dff = conv_w.shape[1]
    n_i, n_j = rows // tm, dff // tn
    tile = pl.BlockSpec((tm, tn), lambda i, j: (i, j))
    wspec = pl.BlockSpec((d, tn), lambda i, j: (0, j))
    in_specs = [pl.BlockSpec((tm, d), lambda i, j: (i, 0)), wspec, wspec,
                pl.BlockSpec((3, tn), lambda i, j: (0, j)), pl.BlockSpec((1, tn), lambda i, j: (0, j))]
    args = [hn, w1, w3, conv_w, conv_b]
    if sample:
        sspec = pl.BlockSpec((tm // seq_len, tn), lambda i, j: (i, j))
        in_specs += [sspec, sspec]
        args += [st1, st0]
    last_rows = tm if sample else SUBLANES
    return pl.pallas_call(
        functools.partial(_ffnup_kernel, sample=sample, tiles_per_seq=max(seq_len // tm, 1)),
        grid=(n_i, n_j), in_specs=in_specs,
        out_specs=[tile, pl.BlockSpec((last_rows, tn), lambda i, j: (i, j))],
        out_shape=[jax.ShapeDtypeStruct((rows, dff), BF16), jax.ShapeDtypeStruct((n_i * last_rows, dff), F32)],
        scratch_shapes=[] if sample else [pltpu.VMEM((n_j, SUBLANES, tn), F32)],
        compiler_params=_params("arbitrary", "arbitrary"), name="ffnup")(*args)


def _ffndown_kernel(h_ref, w_ref, x_ref, o_ref):
    o_ref[...] = x_ref[...] + _dot(h_ref[...], w_ref[...])


def _ffndown_call(h, w2, x1, *, tm, tn):
    rows, dff = h.shape
    d = w2.shape[1]
    tile = pl.BlockSpec((tm, tn), lambda i, j: (i, j))
    return pl.pallas_call(
        _ffndown_kernel, grid=(rows // tm, d // tn),
        in_specs=[pl.BlockSpec((tm, dff), lambda i, j: (i, 0)), pl.BlockSpec((dff, tn), lambda i, j: (0, j)), tile],
        out_specs=tile, out_shape=jax.ShapeDtypeStruct((rows, d), F32),
        compiler_params=_params("arbitrary", "arbitrary"), name="ffndown")(h, w2, x1)


def _tile(n, want):
    t = min(n, want)
    while n % t or (t % SUBLANES and t != n):
        t -= 1
    return t


def _col_tile(n, want):
    t = min(n, want)
    while n % t or t % LANES:
        t -= LANES
    return t


def _layer(x, states, p, *, sample):
    nseq, seq_len, d = x.shape
    rows = nseq * seq_len
    g = p["dw2"].shape[1]
    gc = d - g
    dff = p["ffn_conv_w"].shape[1]
    npair = g // LANES
    x2d = x.reshape(rows, d)
    big = dict(sample=sample, seq_len=seq_len)
    tm_big = rows if sample else _tile(seq_len, 1024)

    if sample:
        assert seq_len == SUBLANES, "the sample path shifts rows inside 8-row groups"
        shift, wkv, conv, ffn = states
        ext_x = shift
        ext_p = _matmul_call(shift.astype(BF16), p["w_in"], 3 * g, _col_tile(3 * g, 512))
        ce1, ce2 = conv[:, 1], conv[:, 0]
        fe1, fe2 = ffn[:, 1], ffn[:, 0]
        s_bd = wkv.reshape(nseq, npair, LANES, HEAD_DIM)
    else:
        ext_x = ext_p = ce1 = ce2 = fe1 = fe2 = s_bd = None

    tm_mix = _tile(rows if sample else seq_len, 128)
    xnb, lw, a, gate, xlast = _mix_call(x2d, ext_x, p, sample=sample, seq_len=seq_len, tm=tm_mix)
    rkv = _rkv_call(xnb, p["w_in"], p["mu_rkv"], ext_p, tm=tm_big, tn=_col_tile(3 * g, 512), **big)
    tm_conv = _tile(rows if sample else seq_len, 512)
    ycv, cxlast = _convbr_call(xnb, p["w_in"], p["conv_w"], ce1, ce2, col0=3 * g, tm=tm_conv,
                               tn=_col_tile(gc, 512), **big)
    if sample:
        t_blk, chunk = WKV_SUB, seq_len
    else:
        chunk = WKV_SUB
        t_blk = _tile(seq_len, 4 * WKV_SUB)
    orw, s_new = _wkv_call(rkv, lw, a, gate, p, s_bd, t_blk=t_blk, chunk=chunk,
                           n_pairs=math.gcd(npair, 4), **big)
    tm_e = _tile(rows, 512)
    x1 = _outproj_call(orw, ycv, p["conv_norm_g"], p["w_out"], x2d, tm=tm_e, tn=_col_tile(d, 512))
    hn = _norm_call(x1, p["norm2_g"], BF16, tm=_tile(rows, 256))
    h, ulast = _ffnup_call(hn, p["ffn_w1"], p["ffn_w3"], p["ffn_conv_w"], p["ffn_conv_b"], fe1, fe2,
                           tm=tm_big, tn=_col_tile(dff, 256), **big)
    x2 = _ffndown_call(h, p["ffn_w2"], x1, tm=tm_e, tn=_col_tile(d, 512))

    def last_rows(arr, tile_rows, k):
        if sample:
            return arr.reshape(nseq, seq_len, -1)[:, seq_len - k:]
        per_seq = seq_len // tile_rows
        return arr.reshape(nseq, per_seq, SUBLANES, -1)[:, -1, SUBLANES - k:]
    new_shift = last_rows(xlast, tm_mix, 1)[:, 0]
    new_conv = last_rows(cxlast, tm_conv, 2)
    new_ffn = last_rows(ulast, tm_big, 2)
    new_wkv = s_new.reshape(nseq, 2 * npair, HEAD_DIM, HEAD_DIM)
    return x2.reshape(nseq, seq_len, d), new_shift, new_wkv, new_conv, new_ffn


def _pad_to(a, axis, mult):
    pad = (-a.shape[axis]) % mult
    if not pad:
        return a
    widths = [(0, 0)] * a.ndim
    widths[axis] = (0, pad)
    return jnp.pad(a, widths)


def kernel(x_prompt, x_sample, state_shift, state_wkv, state_conv, state_ffn, norm1_g, w_in, mu_rkv, mu_lora, decay_w0, decay_w1, decay_w2, aaa_a0, aaa_a1, aaa_a2, gate_g1, gate_g2, k_k, k_a, r_k, ln_x_w, ln_x_b, conv_w, conv_norm_g, w_out, norm2_g, ffn_w1, ffn_conv_w, ffn_conv_b, ffn_w3, ffn_w2, final_norm_g):
    depth = w_in.shape[0]
    d = x_prompt.shape[-1]
    row = lambda v: v.reshape(1, -1).astype(F32)
    yp, ys = x_prompt, x_sample
    outs_p, outs_s = [], []
    for l in range(depth):
        p = dict(
            norm1_g=row(norm1_g[l]), w_in=w_in[l].astype(BF16), mu_rkv=row(mu_rkv[l]), mu_lora=mu_lora[l],
            decay_w0=row(decay_w0[l]), aaa_a0=row(aaa_a0[l]),
            dw1=decay_w1[l].astype(BF16), dw2=decay_w2[l].astype(BF16),
            aw1=aaa_a1[l].astype(BF16), aw2=aaa_a2[l].astype(BF16),
            gw1=_pad_to(gate_g1[l], 1, LANES).astype(BF16), gw2=_pad_to(gate_g2[l], 0, LANES).astype(BF16),
            k_k=row(k_k[l]), k_a=row(k_a[l]), r_k=row(r_k[l]), ln_x_w=row(ln_x_w[l]), ln_x_b=row(ln_x_b[l]),
            conv_w=conv_w[l], conv_norm_g=row(conv_norm_g[l]), w_out=w_out[l].astype(BF16),
            norm2_g=row(norm2_g[l]), ffn_w1=ffn_w1[l].astype(BF16), ffn_w3=ffn_w3[l].astype(BF16),
            ffn_conv_w=ffn_conv_w[l], ffn_conv_b=row(ffn_conv_b[l]), ffn_w2=ffn_w2[l].astype(BF16))
        yp, *st_p = _layer(yp, None, p, sample=False)
        ys, *st_s = _layer(ys, (state_shift[l], state_wkv[l], state_conv[l], state_ffn[l]), p, sample=True)
        outs_p.append(st_p)
        outs_s.append(st_s)
    fin = row(final_norm_g)
    y_prompt = _norm_call(yp.reshape(-1, d), fin, F32, tm=_tile(yp.shape[0] * yp.shape[1], 256)).reshape(yp.shape)
    y_sample = _norm_call(ys.reshape(-1, d), fin, F32, tm=_tile(ys.shape[0] * ys.shape[1], 256)).reshape(ys.shape)
    stack = lambda outs, k: jnp.stack([o[k] for o in outs])
    return (y_prompt, y_sample,
            stack(outs_p, 0), stack(outs_p, 1), stack(outs_p, 2), stack(outs_p, 3),
            stack(outs_s, 0), stack(outs_s, 1), stack(outs_s, 2), stack(outs_s, 3))
```

```python
import functools
import math

import jax
import jax.numpy as jnp
from jax import lax
from jax.experimental import pallas as pl
from jax.experimental.pallas import tpu as pltpu

F32 = jnp.float32
BF16 = jnp.bfloat16

HEAD_DIM = 64
LANES = 128
SUBLANES = 8
WKV_SUB = 64
RMS_EPS = 1e-6
GN_EPS = 64e-5
VMEM_LIMIT_BYTES = 56 * 1024 * 1024


def _params(*sem, flags=None):
    return pltpu.CompilerParams(dimension_semantics=sem, vmem_limit_bytes=VMEM_LIMIT_BYTES, flags=flags)


def _dot(a, b):
    return jnp.dot(a, b, preferred_element_type=F32)


def _dot_nt(a, b):
    return lax.dot_general(a, b, (((1,), (1,)), ((), ())), preferred_element_type=F32)


def _dot_tn(a, b):
    return lax.dot_general(a, b, (((0,), (0,)), ((), ())), preferred_element_type=F32)


def _sigmoid(z):
    return 1.0 / (1.0 + jnp.exp(-z))


def _split_dot(x, b_exact, terms, dot=_dot):
    acc = None
    rem = x
    for _ in range(terms):
        hi = rem.astype(BF16)
        part = dot(hi, b_exact)
        acc = part if acc is None else acc + part
        rem = rem - hi.astype(F32)
    return acc


def _row_iota(shape):
    return lax.broadcasted_iota(jnp.int32, shape, 0)


def _init_carry(carry_ref, j, i):
    @pl.when(i == 0)
    def _():
        carry_ref[j] = jnp.zeros(carry_ref.shape[1:], F32)


def _shifted_rows(u, carry, states):
    tm = u.shape[0]
    row = _row_iota(u.shape)
    r1 = pltpu.roll(u, 1, 0)
    r2 = pltpu.roll(u, 2, 0)
    if carry is not None:
        carry_ref, j, i, tiles_per_seq = carry

        c = carry_ref[j]
        keep = (i % tiles_per_seq) != 0
        first1 = jnp.where(keep, c[SUBLANES - 1:SUBLANES, :], 0.0)
        first2 = jnp.where(keep, c[SUBLANES - 2:SUBLANES - 1, :], 0.0)
        carry_ref[j] = u[tm - SUBLANES:, :]
        return jnp.where(row == 0, first1, r1), jnp.where(row == 0, first2, jnp.where(row == 1, first1, r2))
    first1, first2 = _expand_state_rows(states[0], states[1], tm)
    t = row % SUBLANES
    return jnp.where(t == 0, first1, r1), (None if first2 is None else jnp.where(t < 2, first2, r2))


def _expand_state_rows(st1, st0, tm):
    ns = st1.shape[0]
    row = _row_iota((tm, ns))
    col = lax.broadcasted_iota(jnp.int32, (tm, ns), 1) * SUBLANES
    place = lambda e, x: _split_dot(x, e.astype(BF16), 3, dot=lambda x_, b: _dot(b, x_))
    first1 = place(row == col, st1)
    if st0 is None:
        return first1, None
    return first1, place(row == col, st0) + place(row == col + 1, st1)


def _mix_kernel(*refs, sample, tiles_per_seq):
    if sample:
        (x_ref, ext_ref, g1_ref, mu_ref, w0_ref, a0_ref, dw1_ref, aw1_ref, gw1_ref, dw2_ref,
         aw2_ref, gw2_ref, xnb_ref, lw_ref, a_ref, g_ref, last_ref) = refs
    else:
        (x_ref, g1_ref, mu_ref, w0_ref, a0_ref, dw1_ref, aw1_ref, gw1_ref, dw2_ref,
         aw2_ref, gw2_ref, xnb_ref, lw_ref, a_ref, g_ref, last_ref, carry_ref) = refs
    if not sample:
        _init_carry(carry_ref, 0, pl.program_id(0))
    x = x_ref[...]
    xn = x * lax.rsqrt(jnp.mean(x * x, axis=-1, keepdims=True) + RMS_EPS) * g1_ref[...]
    tm = xn.shape[0]
    if sample:
        prev, _unused = _shifted_rows(xn, None, (ext_ref[...], None))
        last_ref[...] = xn
    else:
        prev, _unused = _shifted_rows(xn, (carry_ref, 0, pl.program_id(0), tiles_per_seq), None)
        last_ref[...] = xn[tm - SUBLANES:, :]
    dx = prev - xn
    xnb_ref[...] = xn.astype(BF16)
    xw = (xn + dx * mu_ref[0:1, :]).astype(BF16)
    hw = jnp.tanh(_dot(xw, dw1_ref[...]))
    wl = w0_ref[...] + _dot(hw.astype(BF16), dw2_ref[...])
    lw_ref[...] = -_sigmoid(wl) * math.exp(-0.5)
    xa = (xn + dx * mu_ref[1:2, :]).astype(BF16)
    ha = _dot(xa, aw1_ref[...])
    a_ref[...] = _sigmoid(a0_ref[...] + _dot(ha.astype(BF16), aw2_ref[...]))
    xg = (xn + dx * mu_ref[2:3, :]).astype(BF16)
    hg = _sigmoid(_dot(xg, gw1_ref[...]))
    g_ref[...] = _dot(hg.astype(BF16), gw2_ref[...])


def _mix_call(x2d, ext, p, *, sample, seq_len, tm):
    rows, d = x2d.shape
    g = p["dw2"].shape[1]
    n_i = rows // tm
    row_blk = lambda w: pl.BlockSpec((tm, w), lambda i: (i, 0))
    full = lambda a: pl.BlockSpec(a.shape, lambda i: (0, 0))
    weights = [p["norm1_g"], p["mu_lora"], p["decay_w0"], p["aaa_a0"], p["dw1"], p["aw1"], p["gw1"],
               p["dw2"], p["aw2"], p["gw2"]]
    state_blk = pl.BlockSpec((tm // seq_len, d), lambda i: (i, 0))
    in_specs = [row_blk(d)] + ([state_blk] if sample else []) + [full(w) for w in weights]
    args = [x2d] + ([ext] if sample else []) + weights
    last_rows = tm if sample else SUBLANES
    out_shape = [jax.ShapeDtypeStruct((rows, d), BF16)] + [jax.ShapeDtypeStruct((rows, g), F32)] * 3 + [
        jax.ShapeDtypeStruct((n_i * last_rows, d), F32)]
    out_specs = [row_blk(d), row_blk(g), row_blk(g), row_blk(g), pl.BlockSpec((last_rows, d), lambda i: (i, 0))]
    return pl.pallas_call(
        functools.partial(_mix_kernel, sample=sample, tiles_per_seq=max(seq_len // tm, 1)),
        grid=(n_i,), in_specs=in_specs, out_specs=out_specs, out_shape=out_shape,
        scratch_shapes=[] if sample else [pltpu.VMEM((1, SUBLANES, d), F32)],
        compiler_params=_params("arbitrary"), name="mix")(*args)


def _rkv_kernel(*refs, sample, tiles_per_seq):
    if sample:
        x_ref, w_ref, mu_ref, ext_ref, o_ref = refs
    else:
        x_ref, w_ref, mu_ref, o_ref, carry_ref = refs
    if not sample:
        _init_carry(carry_ref, pl.program_id(1), pl.program_id(0))
    p = _dot(x_ref[...], w_ref[...])
    if sample:
        prev, _unused = _shifted_rows(p, None, (ext_ref[...], None))
    else:
        prev, _unused = _shifted_rows(p, (carry_ref, pl.program_id(1), pl.program_id(0), tiles_per_seq), None)
    o_ref[...] = p + mu_ref[...] * (prev - p)


def _rkv_call(xnb, w_in, mu, ext, *, sample, seq_len, tm, tn):
    rows, d = xnb.shape
    n = mu.shape[1]
    n_i, n_j = rows // tm, n // tn
    in_specs = [pl.BlockSpec((tm, d), lambda i, j: (i, 0)), pl.BlockSpec((d, tn), lambda i, j: (0, j)),
                pl.BlockSpec((1, tn), lambda i, j: (0, j))]
    args = [xnb, w_in, mu]
    if sample:
        in_specs.append(pl.BlockSpec((tm // seq_len, tn), lambda i, j: (i, j)))
        args.append(ext)
    return pl.pallas_call(
        functools.partial(_rkv_kernel, sample=sample, tiles_per_seq=max(seq_len // tm, 1)),
        grid=(n_i, n_j), in_specs=in_specs, out_specs=pl.BlockSpec((tm, tn), lambda i, j: (i, j)),
        out_shape=jax.ShapeDtypeStruct((rows, n), F32),
        scratch_shapes=[] if sample else [pltpu.VMEM((n_j, SUBLANES, tn), F32)],
        compiler_params=_params("arbitrary", "arbitrary"), name="rkv")(*args)


def _matmul_kernel(x_ref, w_ref, o_ref):
    o_ref[...] = _dot(x_ref[...], w_ref[...])


def _matmul_call(x, w, n, tn):
    rows, d = x.shape
    return pl.pallas_call(
        _matmul_kernel, grid=(n // tn,),
        in_specs=[pl.BlockSpec((rows, d), lambda j: (0, 0)), pl.BlockSpec((d, tn), lambda j: (0, j))],
        out_specs=pl.BlockSpec((rows, tn), lambda j: (0, j)),
        out_shape=jax.ShapeDtypeStruct((rows, n), F32),
        compiler_params=_params("arbitrary"), name="shift_proj")(x, w)


def _convbr_kernel(*refs, sample, tiles_per_seq):
    if sample:
        x_ref, wb_ref, wc_ref, wx_ref, cw_ref, st1_ref, st0_ref, y_ref, last_ref = refs
    else:
        x_ref, wb_ref, wc_ref, wx_ref, cw_ref, y_ref, last_ref, carry_ref = refs
    if not sample:
        _init_carry(carry_ref, pl.program_id(1), pl.program_id(0))
    x = x_ref[...]
    cx = _dot(x, wc_ref[...]) * _dot(x, wx_ref[...])
    tm = cx.shape[0]
    if sample:
        prev1, prev2 = _shifted_rows(cx, None, (st1_ref[...], st0_ref[...]))
        last_ref[...] = cx
    else:
        prev1, prev2 = _shifted_rows(cx, (carry_ref, pl.program_id(1), pl.program_id(0), tiles_per_seq), None)
        last_ref[...] = cx[tm - SUBLANES:, :]
    hconv = prev2 * cw_ref[0:1, :] + prev1 * cw_ref[1:2, :] + cx * cw_ref[2:3, :]
    y_ref[...] = _dot(x, wb_ref[...]) * hconv


def _convbr_call(xnb, w_in, conv_w, e1, e2, *, col0, sample, seq_len, tm, tn):
    rows, d = xnb.shape
    gc = conv_w.shape[1]
    n_i, n_j = rows // tm, gc // tn
    off = col0 // tn
    nb = gc // tn
    wspec = lambda k: pl.BlockSpec((d, tn), lambda i, j: (0, off + k * nb + j))
    tile = pl.BlockSpec((tm, tn), lambda i, j: (i, j))
    in_specs = [pl.BlockSpec((tm, d), lambda i, j: (i, 0)), wspec(0), wspec(1), wspec(2),
                pl.BlockSpec((3, tn), lambda i, j: (0, j))]
    args = [xnb, w_in, w_in, w_in, conv_w]
    if sample:
        sspec = pl.BlockSpec((tm // seq_len, tn), lambda i, j: (i, j))
        in_specs += [sspec, sspec]
        args += [e1, e2]
    last_rows = tm if sample else SUBLANES
    return pl.pallas_call(
        functools.partial(_convbr_kernel, sample=sample, tiles_per_seq=max(seq_len // tm, 1)),
        grid=(n_i, n_j), in_specs=in_specs,
        out_specs=[tile, pl.BlockSpec((last_rows, tn), lambda i, j: (i, j))],
        out_shape=[jax.ShapeDtypeStruct((rows, gc), F32), jax.ShapeDtypeStruct((n_i * last_rows, gc), F32)],
        scratch_shapes=[] if sample else [pltpu.VMEM((n_j, SUBLANES, tn), F32)],
        compiler_params=_params("arbitrary", "arbitrary"), name="convbr")(*args)


def _wkv_masks(c):
    m = 2 * WKV_SUB
    r2 = jnp.bitwise_and(_row_iota((m, m)), WKV_SUB - 1)
    c2 = jnp.bitwise_and(lax.broadcasted_iota(jnp.int32, (m, m), 1), WKV_SUB - 1)
    shift = int(math.log2(c))
    same = jnp.right_shift(r2, shift) == jnp.right_shift(c2, shift)
    eye = (_row_iota((m, m)) == lax.broadcasted_iota(jnp.int32, (m, m), 1)).astype(F32)
    lane_lo = lax.broadcasted_iota(jnp.int32, (WKV_SUB, LANES), 1) < HEAD_DIM
    return lane_lo, same & (c2 < r2), same & (c2 <= r2), eye


def _stack_heads(x, lane_lo):
    z = jnp.zeros_like(x)
    return jnp.concatenate([jnp.where(lane_lo, x, z), jnp.where(lane_lo, z, x)], axis=0)


def _fold_heads(x):
    return x[:WKV_SUB] + x[WKV_SUB:]


def _wkv_phase1(units, masks, c):
    lane_lo, strict, incl, eye = masks
    m = 2 * WKV_SUB
    phs = []
    for r, kt, v, kp, bt, lg, lw in units:
        e1 = jnp.exp(lg)
        e0 = jnp.exp(lg - lw)
        ei = jnp.exp(-lg)
        phs.append(dict(
            e1=e1, rt_st=_stack_heads(r * e1, lane_lo), kp_b=_stack_heads(kp * e0, lane_lo).astype(BF16),
            kh_st=_stack_heads(kt * ei, lane_lo), bh_st=_stack_heads(bt * ei, lane_lo),
            v_st=_stack_heads(v, lane_lo)))
    for ph in phs:
        ph["kh_b"], ph["bh_b"], ph["v_b"] = (ph[n].astype(BF16) for n in ("kh_st", "bh_st", "v_st"))
    gs = [_dot_nt(jnp.concatenate([ph["kp_b"], ph["rt_st"].astype(BF16)], axis=0),
                  jnp.concatenate([ph["bh_b"], ph["kh_b"]], axis=0)) for ph in phs]
    ps = [-jnp.where(strict, g[:m, :m], 0.0) for g in gs]
    ts = [eye + p for p in ps]
    n = 1
    while 2 * n < c:
        pbs = [p.astype(BF16) for p in ps]
        ps = [_dot(pb, pb) for pb in pbs]
        ts = [t + _dot(t.astype(BF16), p.astype(BF16)) for t, p in zip(ts, ps)]
        n *= 2
    abvs = [_dot(jnp.concatenate([jnp.where(strict, g[:m, m:], 0.0), jnp.where(incl, g[m:, m:], 0.0)],
                                 axis=0).astype(BF16), ph["v_b"]) for g, ph in zip(gs, phs)]
    tts = [_dot(t.astype(BF16), jnp.concatenate([ph["kp_b"], abv[:m].astype(BF16)], axis=1))
           for t, ph, abv in zip(ts, phs, abvs)]
    bbtts = [_dot(jnp.where(incl, g[m:, :m], 0.0).astype(BF16), tt.astype(BF16)) for g, tt in zip(gs, tts)]
    for ph, abv, tt, bbtt in zip(phs, abvs, tts, bbtts):
        ph["rq"] = _fold_heads(ph["rt_st"] - bbtt[:, :LANES])
        ph["ov"] = _fold_heads(abv[m:] - bbtt[:, LANES:])
        ph["tk_st"], ph["tav_st"] = tt[:, :LANES], tt[:, LANES:]
    return phs


def _wkv_transitions(phs):
    gams = [ph["e1"][WKV_SUB - 1:WKV_SUB, :] for ph in phs]
    kgs = [(_dot_tn(ph["tk_st"].astype(BF16), ph["bh_b"]) * gam).astype(BF16) for ph, gam in zip(phs, gams)]
    bcgs = [_dot_tn(jnp.concatenate([ph["v_b"], (-ph["tav_st"]).astype(BF16)], axis=0),
                    jnp.concatenate([ph["kh_b"], ph["bh_b"]], axis=0)) * gam for ph, gam in zip(phs, gams)]
    return gams, kgs, bcgs


def _wkv_units_small(states, units, c, lane_lo_c):
    rows = lambda u: slice(u * c, (u + 1) * c)
    rows_hi = lambda u: slice(WKV_SUB + u * c, WKV_SUB + (u + 1) * c)
    ous = [_dot_nt(jnp.concatenate([ph["rq"][rows(u)], ph["tkm"][rows(u)]], axis=0).astype(BF16), s.astype(BF16))
           for s, (ph, u) in zip(states, units)]
    outs, new_states = [], []
    for s, (ph, u), ou in zip(states, units, ous):
        pick = lambda x: jnp.concatenate([x[rows(u)], x[rows_hi(u)]], axis=0)
        uu = ou[c:] + ph["tav"][rows(u)]
        z = jnp.zeros_like(uu)
        u_st = jnp.concatenate([jnp.where(lane_lo_c, uu, z), jnp.where(lane_lo_c, z, uu)], axis=0)
        gam = ph["e1"][(u + 1) * c - 1:(u + 1) * c, :]
        lhs = jnp.concatenate([pick(ph["v_st"]), -u_st], axis=0).astype(BF16)
        rhs = (jnp.concatenate([pick(ph["kh_st"]), pick(ph["bh_st"])], axis=0) * gam).astype(BF16)
        outs.append(ou[:c] + ph["ov"][rows(u)])
        new_states.append(s * gam + _dot_tn(lhs, rhs))
    return outs, new_states


def _wkv_kernel(*refs, sample, chunk, n_pairs, n_sub):
    if sample:
        (r_ref, k_ref, v_ref, lw_ref, a_ref, g_ref, kk_ref, ka_ref, rk_ref, lnw_ref, lnb_ref, sin_ref,
         o_ref, sout_ref) = refs
    else:
        (r_ref, k_ref, v_ref, lw_ref, a_ref, g_ref, kk_ref, ka_ref, rk_ref, lnw_ref, lnb_ref,
         o_ref, sout_ref, s_s) = refs
    c = chunk
    lane128 = lax.broadcasted_iota(jnp.int32, (LANES, LANES), 1)
    row128 = _row_iota((LANES, LANES))
    blockdiag = (row128 < HEAD_DIM) == (lane128 < HEAD_DIM)
    block_ones = blockdiag.astype(BF16)
    seg_sum = lambda z: _split_dot(z, block_ones, 2)
    dup = (_row_iota((HEAD_DIM, LANES))
           == jnp.bitwise_and(lax.broadcasted_iota(jnp.int32, (HEAD_DIM, LANES), 1), HEAD_DIM - 1)).astype(BF16)
    fold = (jnp.bitwise_and(_row_iota((LANES, HEAD_DIM)), HEAD_DIM - 1)
            == lax.broadcasted_iota(jnp.int32, (LANES, HEAD_DIM), 1)).astype(BF16)
    to_blockdiag = lambda x: jnp.where(blockdiag, _split_dot(x, dup, 3), 0.0)
    from_blockdiag = lambda s_: _split_dot(s_, fold, 3)
    masks = _wkv_masks(c)
    r2 = jnp.bitwise_and(_row_iota((WKV_SUB, WKV_SUB)), WKV_SUB - 1)
    c2 = lax.broadcasted_iota(jnp.int32, (WKV_SUB, WKV_SUB), 1)
    shift = int(math.log2(c))
    tril_b = ((jnp.right_shift(r2, shift) == jnp.right_shift(c2, shift)) & (c2 <= r2)).astype(BF16)
    lane_lo_c = lax.broadcasted_iota(jnp.int32, (c, LANES), 1) < HEAD_DIM

    if not sample:
        @pl.when(pl.program_id(2) == 0)
        def _():
            s_s[...] = jnp.zeros_like(s_s)

    lgs = []
    for sb in range(n_sub):
        rows = slice(sb * WKV_SUB, (sb + 1) * WKV_SUB)
        lgs.append(_split_dot(lw_ref[rows, :], tril_b, 3, dot=lambda x, b: _dot(b, x)))

    pairs = range(n_pairs)
    lanes = [slice(q * LANES, (q + 1) * LANES) for q in pairs]
    r = [r_ref[:, l] for l in lanes]
    k = [k_ref[:, l] for l in lanes]
    v = [v_ref[:, l] for l in lanes]
    a = [a_ref[:, l] for l in lanes]
    lw = [lw_ref[:, l] for l in lanes]
    kk = [k[q] * kk_ref[:, lanes[q]] for q in pairs]
    norms = [seg_sum(x * x) for x in kk]
    kk = [x / jnp.maximum(jnp.sqrt(n2), 1e-12) for x, n2 in zip(kk, norms)]
    kmod = [k[q] * (1.0 + (a[q] - 1.0) * ka_ref[:, lanes[q]]) for q in pairs]
    bt = [kk[q] * a[q] for q in pairs]

    subs = [(q, sb) for sb in range(n_sub) for q in pairs]
    units = []
    for q, sb in subs:
        rows = slice(sb * WKV_SUB, (sb + 1) * WKV_SUB)
        units.append((r[q][rows], kmod[q][rows], v[q][rows], kk[q][rows], bt[q][rows],
                      lgs[sb][:, lanes[q]], lw[q][rows]))
    phs = dict(zip(subs, _wkv_phase1(units, masks, c)))

    outs = {q: [] for q in pairs}
    if sample:
        per_sub = WKV_SUB // c
        todo = []
        for (q, sb), ph in phs.items():
            ph["tkm"] = _fold_heads(ph["tk_st"])
            ph["tav"] = _fold_heads(ph["tav_st"])
            todo += [(q, sb * per_sub + u, ph, u) for u in range(per_sub)]
        o_units, new_states = _wkv_units_small([to_blockdiag(sin_ref[seq, q]) for q, seq, _, _ in todo],
                                               [(ph, u) for _, _, ph, u in todo], c, lane_lo_c)
        for (q, seq, _, _), o, s_new in zip(todo, o_units, new_states):
            sout_ref[seq, q] = from_blockdiag(s_new)
            outs[q].append(o)
    else:
        gams, kgs, bcgs = _wkv_transitions([phs[key] for key in subs])
        trans = dict(zip(subs, zip(gams, kgs, bcgs)))
        s = [s_s[q] for q in pairs]
        for sb in range(n_sub):
            sb16 = [x.astype(BF16) for x in s]
            for q in pairs:
                outs[q].append(_dot_nt(phs[q, sb]["rq"].astype(BF16), sb16[q]) + phs[q, sb]["ov"])
            s = [s[q] * trans[q, sb][0] - _dot(sb16[q], trans[q, sb][1]) + trans[q, sb][2] for q in pairs]
        for q in pairs:
            s_s[q] = s[q]

        @pl.when(pl.program_id(2) == pl.num_programs(2) - 1)
        def _():
            for q in pairs:
                sout_ref[0, q] = from_blockdiag(s[q])

    o = [jnp.concatenate(outs[q], axis=0) for q in pairs]
    inv_n = 1.0 / HEAD_DIM
    mu = [_split_dot(x, block_ones, 1) * inv_n for x in o]
    dev = [x - m_ for x, m_ in zip(o, mu)]
    var = [_split_dot(x * x, block_ones, 1) * inv_n for x in dev]
    bonus = [seg_sum(r[q] * kmod[q] * rk_ref[:, lanes[q]]) * v[q] for q in pairs]
    for q in pairs:
        on = dev[q] * lax.rsqrt(var[q] + GN_EPS) * lnw_ref[:, lanes[q]] + lnb_ref[:, lanes[q]]
        o_ref[:, lanes[q]] = ((on + bonus[q]) * g_ref[:, lanes[q]]).astype(BF16)


def _wkv_call(rkv, lw, a, g, p, s_bd, *, sample, seq_len, t_blk, chunk, n_pairs):
    rows, gdim = lw.shape
    npair = gdim // LANES
    nseq = rows // seq_len
    width = n_pairs * LANES
    pair_blocks = npair // n_pairs
    head = [p["k_k"], p["k_a"], p["r_k"], p["ln_x_w"], p["ln_x_b"]]
    if sample:
        seq_blk = t_blk // seq_len
        grid = (nseq // seq_blk, pair_blocks)
        blk = lambda off: pl.BlockSpec((t_blk, width), lambda b, q, off=off: (b, off + q))
        hspec = pl.BlockSpec((1, width), lambda b, q: (0, q))
        sspec = pl.BlockSpec((seq_blk, n_pairs, LANES, HEAD_DIM), lambda b, q: (b, q, 0, 0))
        extra_specs, extra_args = [sspec], [s_bd]
        sem = ("arbitrary", "arbitrary")
        scratch = []
    else:
        nblk = seq_len // t_blk
        grid = (nseq, pair_blocks, nblk)
        blk = lambda off: pl.BlockSpec((t_blk, width), lambda b, q, n, off=off: (b * nblk + n, off + q))
        hspec = pl.BlockSpec((1, width), lambda b, q, n: (0, q))
        sspec = pl.BlockSpec((1, n_pairs, LANES, HEAD_DIM), lambda b, q, n: (b, q, 0, 0))
        extra_specs, extra_args = [], []
        sem = ("arbitrary", "arbitrary", "arbitrary")
        scratch = [pltpu.VMEM((n_pairs, LANES, LANES), F32)]
    in_specs = [blk(0), blk(pair_blocks), blk(2 * pair_blocks), blk(0), blk(0), blk(0)] + [hspec] * 5 + extra_specs
    args = [rkv, rkv, rkv, lw, a, g] + head + extra_args
    return pl.pallas_call(
        functools.partial(_wkv_kernel, sample=sample, chunk=chunk, n_pairs=n_pairs, n_sub=t_blk // WKV_SUB),
        grid=grid, in_specs=in_specs, out_specs=[blk(0), sspec],
        out_shape=[jax.ShapeDtypeStruct((rows, gdim), BF16),
                   jax.ShapeDtypeStruct((nseq, npair, LANES, HEAD_DIM), F32)],
        scratch_shapes=scratch, compiler_params=_params(*sem), name="wkv")(*args)


def _outproj_kernel(orw_ref, ycv_ref, cg_ref, w_ref, x_ref, o_ref, lhs_s):
    g = orw_ref.shape[1]

    @pl.when(pl.program_id(1) == 0)
    def _():
        y = ycv_ref[...]
        yn = y * lax.rsqrt(jnp.mean(y * y, axis=-1, keepdims=True) + RMS_EPS) * cg_ref[...]
        lhs_s[:, :g] = orw_ref[...]
        lhs_s[:, g:] = yn.astype(BF16)

    o_ref[...] = x_ref[...] + _dot(lhs_s[...], w_ref[...])


def _outproj_call(orw, ycv, conv_g, w_out, x2d, *, tm, tn):
    rows, d = x2d.shape
    g, gc = orw.shape[1], ycv.shape[1]
    tile = pl.BlockSpec((tm, tn), lambda i, j: (i, j))
    return pl.pallas_call(
        _outproj_kernel, grid=(rows // tm, d // tn),
        in_specs=[pl.BlockSpec((tm, g), lambda i, j: (i, 0)), pl.BlockSpec((tm, gc), lambda i, j: (i, 0)),
                  pl.BlockSpec((1, gc), lambda i, j: (0, 0)), pl.BlockSpec((d, tn), lambda i, j: (0, j)), tile],
        out_specs=tile, out_shape=jax.ShapeDtypeStruct((rows, d), F32),
        scratch_shapes=[pltpu.VMEM((tm, d), BF16)],
        compiler_params=_params("arbitrary", "arbitrary"), name="outproj")(orw, ycv, conv_g, w_out, x2d)


def _norm_kernel(x_ref, g_ref, o_ref):
    x = x_ref[...]
    y = x * lax.rsqrt(jnp.mean(x * x, axis=-1, keepdims=True) + RMS_EPS) * g_ref[...]
    o_ref[...] = y.astype(o_ref.dtype)


def _norm_call(x2d, g, dtype, *, tm):
    rows, d = x2d.shape
    blk = pl.BlockSpec((tm, d), lambda i: (i, 0))
    return pl.pallas_call(
        _norm_kernel, grid=(rows // tm,), in_specs=[blk, pl.BlockSpec((1, d), lambda i: (0, 0))],
        out_specs=blk, out_shape=jax.ShapeDtypeStruct((rows, d), dtype),
        compiler_params=_params("arbitrary"), name="rmsnorm")(x2d, g)


def _ffnup_kernel(*refs, sample, tiles_per_seq):
    if sample:
        x_ref, w1_ref, w3_ref, cw_ref, cb_ref, st1_ref, st0_ref, h_ref, last_ref = refs
        carry = None
    else:
        x_ref, w1_ref, w3_ref, cw_ref, cb_ref, h_ref, last_ref, carry_ref = refs
        carry = (carry_ref, pl.program_id(1), pl.program_id(0), tiles_per_seq)
        _init_carry(*carry[:3])
    tn = h_ref.shape[1]
    uw = _dot(x_ref[...], jnp.concatenate([w1_ref[...], w3_ref[...]], axis=1))
    u = uw[:, :tn]
    tm = u.shape[0]
    states = (st1_ref[...], st0_ref[...]) if sample else None
    prev1, prev2 = _shifted_rows(u, carry, states)
    last_ref[...] = u if sample else u[tm - SUBLANES:, :]
    z = prev2 * cw_ref[0:1, :] + prev1 * cw_ref[1:2, :] + u * cw_ref[2:3, :] + cb_ref[...]
    hz = 0.5 * z
    h_ref[...] = ((hz + hz * jnp.tanh(hz)) * uw[:, tn:]).astype(BF16)


def _ffnup_call(hn, w1, w3, conv_w, conv_b, st1, st0, *, sample, seq_len, tm, tn):
    rows, d = hn.shape
    dff = conv_w.shape[1]
    n_i, n_j = rows // tm, dff // tn
    tile = pl.BlockSpec((tm, tn), lambda i, j: (i, j))
    wspec = pl.BlockSpec((d, tn), lambda i, j: (0, j))
    in_specs = [pl.BlockSpec((tm, d), lambda i, j: (i, 0)), wspec, wspec,
                pl.BlockSpec((3, tn), lambda i, j: (0, j)), pl.BlockSpec((1, tn), lambda i, j: (0, j))]
    args = [hn, w1, w3, conv_w, conv_b]
    if sample:
        sspec = pl.BlockSpec((tm // seq_len, tn), lambda i, j: (i, j))
        in_specs += [sspec, sspec]
        args += [st1, st0]
    last_rows = tm if sample else SUBLANES
    return pl.pallas_call(
        functools.partial(_ffnup_kernel, sample=sample, tiles_per_seq=max(seq_len // tm, 1)),
        grid=(n_i, n_j), in_specs=in_specs,
        out_specs=[tile, pl.BlockSpec((last_rows, tn), lambda i, j: (i, j))],
        out_shape=[jax.ShapeDtypeStruct((rows, dff), BF16), jax.ShapeDtypeStruct((n_i * last_rows, dff), F32)],
        scratch_shapes=[] if sample else [pltpu.VMEM((n_j, SUBLANES, tn), F32)],
        compiler_params=_params("arbitrary", "arbitrary"), name="ffnup")(*args)


def _ffndown_kernel(h_ref, w_ref, x_ref, o_ref):
    o_ref[...] = x_ref[...] + _dot(h_ref[...], w_ref[...])


def _ffndown_call(h, w2, x1, *, tm, tn):
    rows, dff = h.shape
    d = w2.shape[1]
    tile = pl.BlockSpec((tm, tn), lambda i, j: (i, j))
    return pl.pallas_call(
        _ffndown_kernel, grid=(rows // tm, d // tn),
        in_specs=[pl.BlockSpec((tm, dff), lambda i, j: (i, 0)), pl.BlockSpec((dff, tn), lambda i, j: (0, j)), tile],
        out_specs=tile, out_shape=jax.ShapeDtypeStruct((rows, d), F32),
        compiler_params=_params("arbitrary", "arbitrary"), name="ffndown")(h, w2, x1)


def _tile(n, want):
    t = min(n, want)
    while n % t or (t % SUBLANES and t != n):
        t -= 1
    return t


def _col_tile(n, want):
    t = min(n, want)
    while n % t or t % LANES:
        t -= LANES
    return t


def _layer(x, states, p, *, sample):
    nseq, seq_len, d = x.shape
    rows = nseq * seq_len
    g = p["dw2"].shape[1]
    gc = d - g
    dff = p["ffn_conv_w"].shape[1]
    npair = g // LANES
    x2d = x.reshape(rows, d)
    big = dict(sample=sample, seq_len=seq_len)
    tm_big = rows if sample else _tile(seq_len, 1024)

    if sample:
        assert seq_len == SUBLANES, "the sample path shifts rows inside 8-row groups"
        shift, wkv, conv, ffn = states
        ext_x = shift
        ext_p = _matmul_call(shift.astype(BF16), p["w_in"], 3 * g, _col_tile(3 * g, 512))
        ce1, ce2 = conv[:, 1], conv[:, 0]
        fe1, fe2 = ffn[:, 1], ffn[:, 0]
        s_bd = wkv.reshape(nseq, npair, LANES, HEAD_DIM)
    else:
        ext_x = ext_p = ce1 = ce2 = fe1 = fe2 = s_bd = None

    tm_mix = _tile(rows if sample else seq_len, 128)
    xnb, lw, a, gate, xlast = _mix_call(x2d, ext_x, p, sample=sample, seq_len=seq_len, tm=tm_mix)
    rkv = _rkv_call(xnb, p["w_in"], p["mu_rkv"], ext_p, tm=tm_big, tn=_col_tile(3 * g, 512), **big)
    tm_conv = _tile(rows if sample else seq_len, 512)
    ycv, cxlast = _convbr_call(xnb, p["w_in"], p["conv_w"], ce1, ce2, col0=3 * g, tm=tm_conv,
                               tn=_col_tile(gc, 512), **big)
    if sample:
        t_blk, chunk = WKV_SUB, seq_len
    else:
        chunk = WKV_SUB
        t_blk = _tile(seq_len, 4 * WKV_SUB)
    orw, s_new = _wkv_call(rkv, lw, a, gate, p, s_bd, t_blk=t_blk, chunk=chunk,
                           n_pairs=math.gcd(npair, 4), **big)
    tm_e = _tile(rows, 512)
    x1 = _outproj_call(orw, ycv, p["conv_norm_g"], p["w_out"], x2d, tm=tm_e, tn=_col_tile(d, 512))
    hn = _norm_call(x1, p["norm2_g"], BF16, tm=_tile(rows, 256))
    h, ulast = _ffnup_call(hn, p["ffn_w1"], p["ffn_w3"], p["ffn_conv_w"], p["ffn_conv_b"], fe1, fe2,
                           tm=tm_big, tn=_col_tile(dff, 256), **big)
    x2 = _ffndown_call(h, p["ffn_w2"], x1, tm=tm_e, tn=_col_tile(d, 512))

    def last_rows(arr, tile_rows, k):
        if sample:
            return arr.reshape(nseq, seq_len, -1)[:, seq_len - k:]
        per_seq = seq_len // tile_rows
        return arr.reshape(nseq, per_seq, SUBLANES, -1)[:, -1, SUBLANES - k:]
    new_shift = last_rows(xlast, tm_mix, 1)[:, 0]
    new_conv = last_rows(cxlast, tm_conv, 2)
    new_ffn = last_rows(ulast, tm_big, 2)
    new_wkv = s_new.reshape(nseq, 2 * npair, HEAD_DIM, HEAD_DIM)
    return x2.reshape(nseq, seq_len, d), new_shift, new_wkv, new_conv, new_ffn


def _pad_to(a, axis, mult):
    pad = (-a.shape[axis]) % mult
    if not pad:
        return a
    widths = [(0, 0)] * a.ndim
    widths[axis] = (0, pad)
    return jnp.pad(a, widths)


def kernel(x_prompt, x_sample, state_shift, state_wkv, state_conv, state_ffn, norm1_g, w_in, mu_rkv, mu_lora, decay_w0, decay_w1, decay_w2, aaa_a0, aaa_a1, aaa_a2, gate_g1, gate_g2, k_k, k_a, r_k, ln_x_w, ln_x_b, conv_w, conv_norm_g, w_out, norm2_g, ffn_w1, ffn_conv_w, ffn_conv_b, ffn_w3, ffn_w2, final_norm_g):
    depth = w_in.shape[0]
    d = x_prompt.shape[-1]
    row = lambda v: v.reshape(1, -1).astype(F32)
    yp, ys = x_prompt, x_sample
    outs_p, outs_s = [], []
    for l in range(depth):
        p = dict(
            norm1_g=row(norm1_g[l]), w_in=w_in[l].astype(BF16), mu_rkv=row(mu_rkv[l]), mu_lora=mu_lora[l],
            decay_w0=row(decay_w0[l]), aaa_a0=row(aaa_a0[l]),
            dw1=decay_w1[l].astype(BF16), dw2=decay_w2[l].astype(BF16),
            aw1=aaa_a1[l].astype(BF16), aw2=aaa_a2[l].astype(BF16),
            gw1=_pad_to(gate_g1[l], 1, LANES).astype(BF16), gw2=_pad_to(gate_g2[l], 0, LANES).astype(BF16),
            k_k=row(k_k[l]), k_a=row(k_a[l]), r_k=row(r_k[l]), ln_x_w=row(ln_x_w[l]), ln_x_b=row(ln_x_b[l]),
            conv_w=conv_w[l], conv_norm_g=row(conv_norm_g[l]), w_out=w_out[l].astype(BF16),
            norm2_g=row(norm2_g[l]), ffn_w1=ffn_w1[l].astype(BF16), ffn_w3=ffn_w3[l].astype(BF16),
            ffn_conv_w=ffn_conv_w[l], ffn_conv_b=row(ffn_conv_b[l]), ffn_w2=ffn_w2[l].astype(BF16))
        yp, *st_p = _layer(yp, None, p, sample=False)
        ys, *st_s = _layer(ys, (state_shift[l], state_wkv[l], state_conv[l], state_ffn[l]), p, sample=True)
        outs_p.append(st_p)
        outs_s.append(st_s)
    fin = row(final_norm_g)
    y_prompt = _norm_call(yp.reshape(-1, d), fin, F32, tm=_tile(yp.shape[0] * yp.shape[1], 256)).reshape(yp.shape)
    y_sample = _norm_call(ys.reshape(-1, d), fin, F32, tm=_tile(ys.shape[0] * ys.shape[1], 256)).reshape(ys.shape)
    stack = lambda outs, k: jnp.stack([o[k] for o in outs])
    return (y_prompt, y_sample,
            stack(outs_p, 0), stack(outs_p, 1), stack(outs_p, 2), stack(outs_p, 3),
            stack(outs_s, 0), stack(outs_s, 1), stack(outs_s, 2), stack(outs_s, 3))
```

```python
import functools
import math

import jax
import jax.numpy as jnp
from jax import lax
from jax.experimental import pallas as pl
from jax.experimental.pallas import tpu as pltpu

F32 = jnp.float32
BF16 = jnp.bfloat16

HEAD_DIM = 64
LANES = 128
SUBLANES = 8
WKV_SUB = 64
RMS_EPS = 1e-6
GN_EPS = 64e-5
VMEM_LIMIT_BYTES = 56 * 1024 * 1024


def _params(*sem, flags=None):
    return pltpu.CompilerParams(dimension_semantics=sem, vmem_limit_bytes=VMEM_LIMIT_BYTES, flags=flags)


def _dot(a, b):
    return jnp.dot(a, b, preferred_element_type=F32)


def _dot_nt(a, b):
    return lax.dot_general(a, b, (((1,), (1,)), ((), ())), preferred_element_type=F32)


def _dot_tn(a, b):
    return lax.dot_general(a, b, (((0,), (0,)), ((), ())), preferred_element_type=F32)


def _sigmoid(z):
    return 1.0 / (1.0 + jnp.exp(-z))


def _split_dot(x, b_exact, terms, dot=_dot):
    acc = None
    rem = x
    for _ in range(terms):
        hi = rem.astype(BF16)
        part = dot(hi, b_exact)
        acc = part if acc is None else acc + part
        rem = rem - hi.astype(F32)
    return acc


def _row_iota(shape):
    return lax.broadcasted_iota(jnp.int32, shape, 0)


def _init_carry(carry_ref, j, i):
    @pl.when(i == 0)
    def _():
        carry_ref[j] = jnp.zeros(carry_ref.shape[1:], F32)


def _shifted_rows(u, carry, states):
    tm = u.shape[0]
    row = _row_iota(u.shape)
    r1 = pltpu.roll(u, 1, 0)
    r2 = pltpu.roll(u, 2, 0)
    if carry is not None:
        carry_ref, j, i, tiles_per_seq = carry

        c = carry_ref[j]
        keep = (i % tiles_per_seq) != 0
        first1 = jnp.where(keep, c[SUBLANES - 1:SUBLANES, :], 0.0)
        first2 = jnp.where(keep, c[SUBLANES - 2:SUBLANES - 1, :], 0.0)
        carry_ref[j] = u[tm - SUBLANES:, :]
        return jnp.where(row == 0, first1, r1), jnp.where(row == 0, first2, jnp.where(row == 1, first1, r2))
    first1, first2 = _expand_state_rows(states[0], states[1], tm)
    t = row % SUBLANES
    return jnp.where(t == 0, first1, r1), (None if first2 is None else jnp.where(t < 2, first2, r2))


def _expand_state_rows(st1, st0, tm):
    ns, n = st1.shape
    spread = lambda x: jnp.broadcast_to(x[:, None, :], (ns, SUBLANES, n)).reshape(tm, n)
    first1 = spread(st1)
    if st0 is None:
        return first1, None
    t = _row_iota((tm, n)) % SUBLANES
    return first1, jnp.where(t == 0, spread(st0), first1)


def _mix_kernel(*refs, sample, tiles_per_seq):
    if sample:
        (x_ref, ext_ref, g1_ref, mu_ref, w0_ref, a0_ref, dw1_ref, aw1_ref, gw1_ref, dw2_ref,
         aw2_ref, gw2_ref, xnb_ref, lw_ref, a_ref, g_ref, last_ref) = refs
    else:
        (x_ref, g1_ref, mu_ref, w0_ref, a0_ref, dw1_ref, aw1_ref, gw1_ref, dw2_ref,
         aw2_ref, gw2_ref, xnb_ref, lw_ref, a_ref, g_ref, last_ref, carry_ref) = refs
    if not sample:
        _init_carry(carry_ref, 0, pl.program_id(0))
    x = x_ref[...]
    xn = x * lax.rsqrt(jnp.mean(x * x, axis=-1, keepdims=True) + RMS_EPS) * g1_ref[...]
    tm = xn.shape[0]
    if sample:
        prev, _unused = _shifted_rows(xn, None, (ext_ref[...], None))
        last_ref[...] = xn
    else:
        prev, _unused = _shifted_rows(xn, (carry_ref, 0, pl.program_id(0), tiles_per_seq), None)
        last_ref[...] = xn[tm - SUBLANES:, :]
    dx = prev - xn
    xnb_ref[...] = xn.astype(BF16)
    xw = (xn + dx * mu_ref[0:1, :]).astype(BF16)
    hw = jnp.tanh(_dot(xw, dw1_ref[...]))
    wl = w0_ref[...] + _dot(hw.astype(BF16), dw2_ref[...])
    lw_ref[...] = -_sigmoid(wl) * math.exp(-0.5)
    xa = (xn + dx * mu_ref[1:2, :]).astype(BF16)
    ha = _dot(xa, aw1_ref[...])
    a_ref[...] = _sigmoid(a0_ref[...] + _dot(ha.astype(BF16), aw2_ref[...]))
    xg = (xn + dx * mu_ref[2:3, :]).astype(BF16)
    hg = _sigmoid(_dot(xg, gw1_ref[...]))
    g_ref[...] = _dot(hg.astype(BF16), gw2_ref[...])


def _mix_call(x2d, ext, p, *, sample, seq_len, tm):
    rows, d = x2d.shape
    g = p["dw2"].shape[1]
    n_i = rows // tm
    row_blk = lambda w: pl.BlockSpec((tm, w), lambda i: (i, 0))
    full = lambda a: pl.BlockSpec(a.shape, lambda i: (0, 0))
    weights = [p["norm1_g"], p["mu_lora"], p["decay_w0"], p["aaa_a0"], p["dw1"], p["aw1"], p["gw1"],
               p["dw2"], p["aw2"], p["gw2"]]
    state_blk = pl.BlockSpec((tm // seq_len, d), lambda i: (i, 0))
    in_specs = [row_blk(d)] + ([state_blk] if sample else []) + [full(w) for w in weights]
    args = [x2d] + ([ext] if sample else []) + weights
    last_rows = tm if sample else SUBLANES
    out_shape = [jax.ShapeDtypeStruct((rows, d), BF16)] + [jax.ShapeDtypeStruct((rows, g), F32)] * 3 + [
        jax.ShapeDtypeStruct((n_i * last_rows, d), F32)]
    out_specs = [row_blk(d), row_blk(g), row_blk(g), row_blk(g), pl.BlockSpec((last_rows, d), lambda i: (i, 0))]
    return pl.pallas_call(
        functools.partial(_mix_kernel, sample=sample, tiles_per_seq=max(seq_len // tm, 1)),
        grid=(n_i,), in_specs=in_specs, out_specs=out_specs, out_shape=out_shape,
        scratch_shapes=[] if sample else [pltpu.VMEM((1, SUBLANES, d), F32)],
        compiler_params=_params("arbitrary"), name="mix")(*args)


def _rkv_kernel(*refs, sample, tiles_per_seq):
    if sample:
        x_ref, w_ref, mu_ref, ext_ref, o_ref = refs
    else:
        x_ref, w_ref, mu_ref, o_ref, carry_ref = refs
    if not sample:
        _init_carry(carry_ref, pl.program_id(1), pl.program_id(0))
    p = _dot(x_ref[...], w_ref[...].astype(BF16))
    if sample:
        prev, _unused = _shifted_rows(p, None, (ext_ref[...], None))
    else:
        prev, _unused = _shifted_rows(p, (carry_ref, pl.program_id(1), pl.program_id(0), tiles_per_seq), None)
    o_ref[...] = p + mu_ref[...] * (prev - p)


def _rkv_call(xnb, w_in, mu, ext, *, sample, seq_len, tm, tn):
    rows, d = xnb.shape
    n = mu.shape[1]
    n_i, n_j = rows // tm, n // tn
    in_specs = [pl.BlockSpec((tm, d), lambda i, j: (i, 0)), pl.BlockSpec((d, tn), lambda i, j: (0, j)),
                pl.BlockSpec((1, tn), lambda i, j: (0, j))]
    args = [xnb, w_in, mu]
    if sample:
        in_specs.append(pl.BlockSpec((tm // seq_len, tn), lambda i, j: (i, j)))
        args.append(ext)
    return pl.pallas_call(
        functools.partial(_rkv_kernel, sample=sample, tiles_per_seq=max(seq_len // tm, 1)),
        grid=(n_i, n_j), in_specs=in_specs, out_specs=pl.BlockSpec((tm, tn), lambda i, j: (i, j)),
        out_shape=jax.ShapeDtypeStruct((rows, n), F32),
        scratch_shapes=[] if sample else [pltpu.VMEM((n_j, SUBLANES, tn), F32)],
        compiler_params=_params("arbitrary", "arbitrary"), name="rkv")(*args)


def _matmul_kernel(x_ref, w_ref, o_ref):
    o_ref[...] = _dot(x_ref[...], w_ref[...].astype(BF16))


def _matmul_call(x, w, n, tn):
    rows, d = x.shape
    return pl.pallas_call(
        _matmul_kernel, grid=(n // tn,),
        in_specs=[pl.BlockSpec((rows, d), lambda j: (0, 0)), pl.BlockSpec((d, tn), lambda j: (0, j))],
        out_specs=pl.BlockSpec((rows, tn), lambda j: (0, j)),
        out_shape=jax.ShapeDtypeStruct((rows, n), F32),
        compiler_params=_params("arbitrary"), name="shift_proj")(x, w)


def _convbr_kernel(*refs, sample, tiles_per_seq):
    if sample:
        x_ref, wb_ref, wc_ref, wx_ref, cw_ref, st1_ref, st0_ref, y_ref, last_ref = refs
    else:
        x_ref, wb_ref, wc_ref, wx_ref, cw_ref, y_ref, last_ref, carry_ref = refs
    if not sample:
        _init_carry(carry_ref, pl.program_id(1), pl.program_id(0))
    x = x_ref[...]
    cx = _dot(x, wc_ref[...]) * _dot(x, wx_ref[...])
    tm = cx.shape[0]
    if sample:
        prev1, prev2 = _shifted_rows(cx, None, (st1_ref[...], st0_ref[...]))
        last_ref[...] = cx
    else:
        prev1, prev2 = _shifted_rows(cx, (carry_ref, pl.program_id(1), pl.program_id(0), tiles_per_seq), None)
        last_ref[...] = cx[tm - SUBLANES:, :]
    hconv = prev2 * cw_ref[0:1, :] + prev1 * cw_ref[1:2, :] + cx * cw_ref[2:3, :]
    y_ref[...] = _dot(x, wb_ref[...]) * hconv


def _convbr_call(xnb, w_in, conv_w, e1, e2, *, col0, sample, seq_len, tm, tn):
    rows, d = xnb.shape
    gc = conv_w.shape[1]
    n_i, n_j = rows // tm, gc // tn
    off = col0 // tn
    nb = gc // tn
    wspec = lambda k: pl.BlockSpec((d, tn), lambda i, j: (0, off + k * nb + j))
    tile = pl.BlockSpec((tm, tn), lambda i, j: (i, j))
    in_specs = [pl.BlockSpec((tm, d), lambda i, j: (i, 0)), wspec(0), wspec(1), wspec(2),
                pl.BlockSpec((3, tn), lambda i, j: (0, j))]
    args = [xnb, w_in, w_in, w_in, conv_w]
    if sample:
        sspec = pl.BlockSpec((tm // seq_len, tn), lambda i, j: (i, j))
        in_specs += [sspec, sspec]
        args += [e1, e2]
    last_rows = tm if sample else SUBLANES
    return pl.pallas_call(
        functools.partial(_convbr_kernel, sample=sample, tiles_per_seq=max(seq_len // tm, 1)),
        grid=(n_i, n_j), in_specs=in_specs,
        out_specs=[tile, pl.BlockSpec((last_rows, tn), lambda i, j: (i, j))],
        out_shape=[jax.ShapeDtypeStruct((rows, gc), F32), jax.ShapeDtypeStruct((n_i * last_rows, gc), F32)],
        scratch_shapes=[] if sample else [pltpu.VMEM((n_j, SUBLANES, tn), F32)],
        compiler_params=_params("arbitrary", "arbitrary"), name="convbr")(*args)


def _wkv_masks(c):
    m = 2 * WKV_SUB
    r2 = jnp.bitwise_and(_row_iota((m, m)), WKV_SUB - 1)
    c2 = jnp.bitwise_and(lax.broadcasted_iota(jnp.int32, (m, m), 1), WKV_SUB - 1)
    shift = int(math.log2(c))
    same = jnp.right_shift(r2, shift) == jnp.right_shift(c2, shift)
    eye = (_row_iota((m, m)) == lax.broadcasted_iota(jnp.int32, (m, m), 1)).astype(F32)
    lane_lo = lax.broadcasted_iota(jnp.int32, (WKV_SUB, LANES), 1) < HEAD_DIM
    return lane_lo, same & (c2 < r2), same & (c2 <= r2), eye


def _stack_heads(x, lane_lo):
    z = jnp.zeros_like(x)
    return jnp.concatenate([jnp.where(lane_lo, x, z), jnp.where(lane_lo, z, x)], axis=0)


def _fold_heads(x):
    return x[:WKV_SUB] + x[WKV_SUB:]


def _wkv_phase1(units, masks, c):
    lane_lo, strict, incl, eye = masks
    m = 2 * WKV_SUB
    phs = []
    for r, kt, v, kp, bt, lg, lw in units:
        e1 = jnp.exp(lg)
        e0 = jnp.exp(lg - lw)
        ei = jnp.exp(-lg)
        phs.append(dict(
            e1=e1, rt_st=_stack_heads(r * e1, lane_lo), kp_b=_stack_heads(kp * e0, lane_lo).astype(BF16),
            kh_st=_stack_heads(kt * ei, lane_lo), bh_st=_stack_heads(bt * ei, lane_lo),
            v_st=_stack_heads(v, lane_lo)))
    for ph in phs:
        ph["kh_b"], ph["bh_b"], ph["v_b"] = (ph[n].astype(BF16) for n in ("kh_st", "bh_st", "v_st"))
    gs = [_dot_nt(jnp.concatenate([ph["kp_b"], ph["rt_st"].astype(BF16)], axis=0),
                  jnp.concatenate([ph["bh_b"], ph["kh_b"]], axis=0)) for ph in phs]
    ps = [-jnp.where(strict, g[:m, :m], 0.0) for g in gs]
    ts = [eye + p for p in ps]
    n = 1
    while 2 * n < c:
        pbs = [p.astype(BF16) for p in ps]
        ps = [_dot(pb, pb) for pb in pbs]
        ts = [t + _dot(t.astype(BF16), p.astype(BF16)) for t, p in zip(ts, ps)]
        n *= 2
    abvs = [_dot(jnp.concatenate([jnp.where(strict, g[:m, m:], 0.0), jnp.where(incl, g[m:, m:], 0.0)],
                                 axis=0).astype(BF16), ph["v_b"]) for g, ph in zip(gs, phs)]
    tts = [_dot(t.astype(BF16), jnp.concatenate([ph["kp_b"], abv[:m].astype(BF16)], axis=1))
           for t, ph, abv in zip(ts, phs, abvs)]
    bbtts = [_dot(jnp.where(incl, g[m:, :m], 0.0).astype(BF16), tt.astype(BF16)) for g, tt in zip(gs, tts)]
    for ph, abv, tt, bbtt in zip(phs, abvs, tts, bbtts):
        ph["rq"] = _fold_heads(ph["rt_st"] - bbtt[:, :LANES])
        ph["ov"] = _fold_heads(abv[m:] - bbtt[:, LANES:])
        ph["tk_st"], ph["tav_st"] = tt[:, :LANES], tt[:, LANES:]
    return phs


def _wkv_transitions(phs):
    gams = [ph["e1"][WKV_SUB - 1:WKV_SUB, :] for ph in phs]
    kgs = [(_dot_tn(ph["tk_st"].astype(BF16), ph["bh_b"]) * gam).astype(BF16) for ph, gam in zip(phs, gams)]
    bcgs = [_dot_tn(jnp.concatenate([ph["v_b"], (-ph["tav_st"]).astype(BF16)], axis=0),
                    jnp.concatenate([ph["kh_b"], ph["bh_b"]], axis=0)) * gam for ph, gam in zip(phs, gams)]
    return gams, kgs, bcgs


def _wkv_units_small(states, units, c, lane_lo_c):
    rows = lambda u: slice(u * c, (u + 1) * c)
    rows_hi = lambda u: slice(WKV_SUB + u * c, WKV_SUB + (u + 1) * c)
    ous = [_dot_nt(jnp.concatenate([ph["rq"][rows(u)], ph["tkm"][rows(u)]], axis=0).astype(BF16), s.astype(BF16))
           for s, (ph, u) in zip(states, units)]
    outs, new_states = [], []
    for s, (ph, u), ou in zip(states, units, ous):
        pick = lambda x: jnp.concatenate([x[rows(u)], x[rows_hi(u)]], axis=0)
        uu = ou[c:] + ph["tav"][rows(u)]
        z = jnp.zeros_like(uu)
        u_st = jnp.concatenate([jnp.where(lane_lo_c, uu, z), jnp.where(lane_lo_c, z, uu)], axis=0)
        gam = ph["e1"][(u + 1) * c - 1:(u + 1) * c, :]
        lhs = jnp.concatenate([pick(ph["v_st"]), -u_st], axis=0).astype(BF16)
        rhs = (jnp.concatenate([pick(ph["kh_st"]), pick(ph["bh_st"])], axis=0) * gam).astype(BF16)
        outs.append(ou[:c] + ph["ov"][rows(u)])
        new_states.append(s * gam + _dot_tn(lhs, rhs))
    return outs, new_states


def _wkv_kernel(*refs, sample, chunk, n_pairs, n_sub):
    if sample:
        (r_ref, k_ref, v_ref, lw_ref, a_ref, g_ref, kk_ref, ka_ref, rk_ref, lnw_ref, lnb_ref, sin_ref,
         o_ref, sout_ref) = refs
    else:
        (r_ref, k_ref, v_ref, lw_ref, a_ref, g_ref, kk_ref, ka_ref, rk_ref, lnw_ref, lnb_ref,
         o_ref, sout_ref, s_s) = refs
    c = chunk
    lane128 = lax.broadcasted_iota(jnp.int32, (LANES, LANES), 1)
    row128 = _row_iota((LANES, LANES))
    blockdiag = (row128 < HEAD_DIM) == (lane128 < HEAD_DIM)
    block_ones = blockdiag.astype(BF16)
    seg_sum = lambda z: _split_dot(z, block_ones, 2)
    to_blockdiag = lambda x: jnp.where(blockdiag, jnp.concatenate([x, x], axis=1), 0.0)
    from_blockdiag = lambda s_: s_[:, :HEAD_DIM] + s_[:, HEAD_DIM:]
    masks = _wkv_masks(c)
    r2 = jnp.bitwise_and(_row_iota((WKV_SUB, WKV_SUB)), WKV_SUB - 1)
    c2 = lax.broadcasted_iota(jnp.int32, (WKV_SUB, WKV_SUB), 1)
    shift = int(math.log2(c))
    tril_b = ((jnp.right_shift(r2, shift) == jnp.right_shift(c2, shift)) & (c2 <= r2)).astype(BF16)
    lane_lo_c = lax.broadcasted_iota(jnp.int32, (c, LANES), 1) < HEAD_DIM

    if not sample:
        @pl.when(pl.program_id(2) == 0)
        def _():
            s_s[...] = jnp.zeros_like(s_s)

    lgs = []
    for sb in range(n_sub):
        rows = slice(sb * WKV_SUB, (sb + 1) * WKV_SUB)
        lgs.append(_split_dot(lw_ref[rows, :], tril_b, 3, dot=lambda x, b: _dot(b, x)))

    pairs = range(n_pairs)
    lanes = [slice(q * LANES, (q + 1) * LANES) for q in pairs]
    r = [r_ref[:, l] for l in lanes]
    k = [k_ref[:, l] for l in lanes]
    v = [v_ref[:, l] for l in lanes]
    a = [a_ref[:, l] for l in lanes]
    lw = [lw_ref[:, l] for l in lanes]
    kk = [k[q] * kk_ref[:, lanes[q]] for q in pairs]
    norms = [seg_sum(x * x) for x in kk]
    kk = [x / jnp.maximum(jnp.sqrt(n2), 1e-12) for x, n2 in zip(kk, norms)]
    kmod = [k[q] * (1.0 + (a[q] - 1.0) * ka_ref[:, lanes[q]]) for q in pairs]
    bt = [kk[q] * a[q] for q in pairs]

    subs = [(q, sb) for sb in range(n_sub) for q in pairs]
    units = []
    for q, sb in subs:
        rows = slice(sb * WKV_SUB, (sb + 1) * WKV_SUB)
        units.append((r[q][rows], kmod[q][rows], v[q][rows], kk[q][rows], bt[q][rows],
                      lgs[sb][:, lanes[q]], lw[q][rows]))
    phs = dict(zip(subs, _wkv_phase1(units, masks, c)))

    outs = {q: [] for q in pairs}
    if sample:
        per_sub = WKV_SUB // c
        todo = []
        for (q, sb), ph in phs.items():
            ph["tkm"] = _fold_heads(ph["tk_st"])
            ph["tav"] = _fold_heads(ph["tav_st"])
            todo += [(q, sb * per_sub + u, ph, u) for u in range(per_sub)]
        o_units, new_states = _wkv_units_small([to_blockdiag(sin_ref[seq, q]) for q, seq, _, _ in todo],
                                               [(ph, u) for _, _, ph, u in todo], c, lane_lo_c)
        for (q, seq, _, _), o, s_new in zip(todo, o_units, new_states):
            sout_ref[seq, q] = from_blockdiag(s_new)
            outs[q].append(o)
    else:
        gams, kgs, bcgs = _wkv_transitions([phs[key] for key in subs])
        trans = dict(zip(subs, zip(gams, kgs, bcgs)))
        s = [s_s[q] for q in pairs]
        for sb in range(n_sub):
            sb16 = [x.astype(BF16) for x in s]
            for q in pairs:
                outs[q].append(_dot_nt(phs[q, sb]["rq"].astype(BF16), sb16[q]) + phs[q, sb]["ov"])
            s = [s[q] * trans[q, sb][0] - _dot(sb16[q], trans[q, sb][1]) + trans[q, sb][2] for q in pairs]
        for q in pairs:
            s_s[q] = s[q]

        @pl.when(pl.program_id(2) == pl.num_programs(2) - 1)
        def _():
            for q in pairs:
                sout_ref[0, q] = from_blockdiag(s[q])

    o = [jnp.concatenate(outs[q], axis=0) for q in pairs]
    inv_n = 1.0 / HEAD_DIM
    mu = [_split_dot(x, block_ones, 1) * inv_n for x in o]
    dev = [x - m_ for x, m_ in zip(o, mu)]
    var = [_split_dot(x * x, block_ones, 1) * inv_n for x in dev]
    bonus = [seg_sum(r[q] * kmod[q] * rk_ref[:, lanes[q]]) * v[q] for q in pairs]
    for q in pairs:
        on = dev[q] * lax.rsqrt(var[q] + GN_EPS) * lnw_ref[:, lanes[q]] + lnb_ref[:, lanes[q]]
        o_ref[:, lanes[q]] = ((on + bonus[q]) * g_ref[:, lanes[q]]).astype(BF16)


def _wkv_call(rkv, lw, a, g, p, s_bd, *, sample, seq_len, t_blk, chunk, n_pairs):
    rows, gdim = lw.shape
    npair = gdim // LANES
    nseq = rows // seq_len
    width = n_pairs * LANES
    pair_blocks = npair // n_pairs
    head = [p["k_k"], p["k_a"], p["r_k"], p["ln_x_w"], p["ln_x_b"]]
    if sample:
        seq_blk = t_blk // seq_len
        grid = (nseq // seq_blk, pair_blocks)
        blk = lambda off: pl.BlockSpec((t_blk, width), lambda b, q, off=off: (b, off + q))
        hspec = pl.BlockSpec((1, width), lambda b, q: (0, q))
        sspec = pl.BlockSpec((seq_blk, n_pairs, LANES, HEAD_DIM), lambda b, q: (b, q, 0, 0))
        extra_specs, extra_args = [sspec], [s_bd]
        sem = ("arbitrary", "arbitrary")
        scratch = []
    else:
        nblk = seq_len // t_blk
        grid = (nseq, pair_blocks, nblk)
        blk = lambda off: pl.BlockSpec((t_blk, width), lambda b, q, n, off=off: (b * nblk + n, off + q))
        hspec = pl.BlockSpec((1, width), lambda b, q, n: (0, q))
        sspec = pl.BlockSpec((1, n_pairs, LANES, HEAD_DIM), lambda b, q, n: (b, q, 0, 0))
        extra_specs, extra_args = [], []
        sem = ("arbitrary", "arbitrary", "arbitrary")
        scratch = [pltpu.VMEM((n_pairs, LANES, LANES), F32)]
    in_specs = [blk(0), blk(pair_blocks), blk(2 * pair_blocks), blk(0), blk(0), blk(0)] + [hspec] * 5 + extra_specs
    args = [rkv, rkv, rkv, lw, a, g] + head + extra_args
    return pl.pallas_call(
        functools.partial(_wkv_kernel, sample=sample, chunk=chunk, n_pairs=n_pairs, n_sub=t_blk // WKV_SUB),
        grid=grid, in_specs=in_specs, out_specs=[blk(0), sspec],
        out_shape=[jax.ShapeDtypeStruct((rows, gdim), BF16),
                   jax.ShapeDtypeStruct((nseq, npair, LANES, HEAD_DIM), F32)],
        scratch_shapes=scratch, compiler_params=_params(*sem), name="wkv")(*args)


def _outproj_kernel(orw_ref, ocv_ref, w_ref, x_ref, o_ref):
    lhs = jnp.concatenate([orw_ref[...], ocv_ref[...]], axis=1)
    o_ref[...] = x_ref[...] + _dot(lhs, w_ref[...])


def _outproj_call(orw, ocv, w_out, x2d, *, tm, tn):
    rows, d = x2d.shape
    g, gc = orw.shape[1], ocv.shape[1]
    tile = pl.BlockSpec((tm, tn), lambda i, j: (i, j))
    return pl.pallas_call(
        _outproj_kernel, grid=(rows // tm, d // tn),
        in_specs=[pl.BlockSpec((tm, g), lambda i, j: (i, 0)), pl.BlockSpec((tm, gc), lambda i, j: (i, 0)),
                  pl.BlockSpec((d, tn), lambda i, j: (0, j)), tile],
        out_specs=tile, out_shape=jax.ShapeDtypeStruct((rows, d), F32),
        compiler_params=_params("arbitrary", "arbitrary"), name="outproj")(orw, ocv, w_out, x2d)


def _norm_kernel(x_ref, g_ref, o_ref):
    x = x_ref[...]
    y = x * lax.rsqrt(jnp.mean(x * x, axis=-1, keepdims=True) + RMS_EPS) * g_ref[...]
    o_ref[...] = y.astype(o_ref.dtype)


def _norm_call(x2d, g, dtype, *, tm):
    rows, d = x2d.shape
    blk = pl.BlockSpec((tm, d), lambda i: (i, 0))
    return pl.pallas_call(
        _norm_kernel, grid=(rows // tm,), in_specs=[blk, pl.BlockSpec((1, d), lambda i: (0, 0))],
        out_specs=blk, out_shape=jax.ShapeDtypeStruct((rows, d), dtype),
        compiler_params=_params("arbitrary"), name="rmsnorm")(x2d, g)


def _ffnup_kernel(*refs, sample, tiles_per_seq):
    if sample:
        x_ref, w1_ref, w3_ref, cw_ref, cb_ref, st1_ref, st0_ref, h_ref, last_ref = refs
        carry = None
    else:
        x_ref, w1_ref, w3_ref, cw_ref, cb_ref, h_ref, last_ref, carry_ref = refs
        carry = (carry_ref, pl.program_id(1), pl.program_id(0), tiles_per_seq)
        _init_carry(*carry[:3])
    tn = h_ref.shape[1]
    w13 = jnp.concatenate([w1_ref[...].astype(BF16), w3_ref[...].astype(BF16)], axis=1)
    uw = _dot(x_ref[...], w13)
    u = uw[:, :tn]
    tm = u.shape[0]
    states = (st1_ref[...], st0_ref[...]) if sample else None
    prev1, prev2 = _shifted_rows(u, carry, states)
    last_ref[...] = u if sample else u[tm - SUBLANES:, :]
    z = prev2 * cw_ref[0:1, :] + prev1 * cw_ref[1:2, :] + u * cw_ref[2:3, :] + cb_ref[...]
    hz = 0.5 * z
    h_ref[...] = ((hz + hz * jnp.tanh(hz)) * uw[:, tn:]).astype(BF16)


def _ffnup_call(hn, w1, w3, conv_w, conv_b, st1, st0, *, sample, seq_len, tm, tn):
    rows, d = hn.shape
    dff = conv_w.shape[1]
    n_i, n_j = rows // tm, dff // tn
    tile = pl.BlockSpec((tm, tn), lambda i, j: (i, j))
    wspec = pl.BlockSpec((d, tn), lambda i, j: (0, j))
    in_specs = [pl.BlockSpec((tm, d), lambda i, j: (i, 0)), wspec, wspec,
                pl.BlockSpec((3, tn), lambda i, j: (0, j)), pl.BlockSpec((1, tn), lambda i, j: (0, j))]
    args = [hn, w1, w3, conv_w, conv_b]
    if sample:
        sspec = pl.BlockSpec((tm // seq_len, tn), lambda i, j: (i, j))
        in_specs += [sspec, sspec]
        args += [st1, st0]
    last_rows = tm if sample else SUBLANES
    return pl.pallas_call(
        functools.partial(_ffnup_kernel, sample=sample, tiles_per_seq=max(seq_len // tm, 1)),
        grid=(n_i, n_j), in_specs=in_specs,
        out_specs=[tile, pl.BlockSpec((last_rows, tn), lambda i, j: (i, j))],
        out_shape=[jax.ShapeDtypeStruct((rows, dff), BF16), jax.ShapeDtypeStruct((n_i * last_rows, dff), F32)],
        scratch_shapes=[] if sample else [pltpu.VMEM((n_j, SUBLANES, tn), F32)],
        compiler_params=_params("arbitrary", "arbitrary"), name="ffnup")(*args)


def _ffndown_kernel(h_ref, w_ref, x_ref, o_ref):
    o_ref[...] = x_ref[...] + _dot(h_ref[...], w_ref[...])


def _ffndown_call(h, w2, x1, *, tm, tn):
    rows, dff = h.shape
    d = w2.shape[1]
    tile = pl.BlockSpec((tm, tn), lambda i, j: (i, j))
    return pl.pallas_call(
        _ffndown_kernel, grid=(rows // tm, d // tn),
        in_specs=[pl.BlockSpec((tm, dff), lambda i, j: (i, 0)), pl.BlockSpec((dff, tn), lambda i, j: (0, j)), tile],
        out_specs=tile, out_shape=jax.ShapeDtypeStruct((rows, d), F32),
        compiler_params=_params("arbitrary", "arbitrary"), name="ffndown")(h, w2, x1)


def _tile(n, want):
    t = min(n, want)
    while n % t or (t % SUBLANES and t != n):
        t -= 1
    return t


def _col_tile(n, want):
    t = min(n, want)
    while n % t or t % LANES:
        t -= LANES
    return t


def _layer(x, states, p, *, sample):
    nseq, seq_len, d = x.shape
    rows = nseq * seq_len
    g = p["dw2"].shape[1]
    gc = d - g
    dff = p["ffn_conv_w"].shape[1]
    npair = g // LANES
    x2d = x.reshape(rows, d)
    big = dict(sample=sample, seq_len=seq_len)
    tm_big = rows if sample else _tile(seq_len, 1024)

    if sample:
        assert seq_len == SUBLANES, "the sample path shifts rows inside 8-row groups"
        shift, wkv, conv, ffn = states
        ext_x = shift
        ext_p = _matmul_call(shift.astype(BF16), p["w_in"], 3 * g, _col_tile(3 * g, 512))
        ce1, ce2 = conv[:, 1], conv[:, 0]
        fe1, fe2 = ffn[:, 1], ffn[:, 0]
        s_bd = wkv.reshape(nseq, npair, LANES, HEAD_DIM)
    else:
        ext_x = ext_p = ce1 = ce2 = fe1 = fe2 = s_bd = None

    tm_mix = _tile(rows if sample else seq_len, 128)
    xnb, lw, a, gate, xlast = _mix_call(x2d, ext_x, p, sample=sample, seq_len=seq_len, tm=tm_mix)
    rkv = _rkv_call(xnb, p["w_in"], p["mu_rkv"], ext_p, tm=tm_big, tn=_col_tile(3 * g, 512), **big)
    tm_conv = _tile(rows if sample else seq_len, 512)
    ycv, cxlast = _convbr_call(xnb, p["w_conv"], p["conv_w"], ce1, ce2, col0=0, tm=tm_conv,
                               tn=_col_tile(gc, 512), **big)
    if sample:
        t_blk, chunk = WKV_SUB, seq_len
    else:
        chunk = WKV_SUB
        t_blk = _tile(seq_len, 4 * WKV_SUB)
    orw, s_new = _wkv_call(rkv, lw, a, gate, p, s_bd, t_blk=t_blk, chunk=chunk,
                           n_pairs=math.gcd(npair, 4), **big)
    tm_e = _tile(rows, 512)
    ocv = _norm_call(ycv, p["conv_norm_g"], BF16, tm=_tile(rows, 512))
    x1 = _outproj_call(orw, ocv, p["w_out"], x2d, tm=_tile(rows, 1024), tn=_col_tile(d, 512))
    hn = _norm_call(x1, p["norm2_g"], BF16, tm=_tile(rows, 256))
    h, ulast = _ffnup_call(hn, p["ffn_w1"], p["ffn_w3"], p["ffn_conv_w"], p["ffn_conv_b"], fe1, fe2,
                           tm=tm_big, tn=_col_tile(dff, 256), **big)
    x2 = _ffndown_call(h, p["ffn_w2"], x1, tm=tm_e, tn=_col_tile(d, 512))

    def last_rows(arr, tile_rows, k):
        if sample:
            return arr.reshape(nseq, seq_len, -1)[:, seq_len - k:]
        per_seq = seq_len // tile_rows
        return arr.reshape(nseq, per_seq, SUBLANES, -1)[:, -1, SUBLANES - k:]
    new_shift = last_rows(xlast, tm_mix, 1)[:, 0]
    new_conv = last_rows(cxlast, tm_conv, 2)
    new_ffn = last_rows(ulast, tm_big, 2)
    new_wkv = s_new.reshape(nseq, 2 * npair, HEAD_DIM, HEAD_DIM)
    return x2.reshape(nseq, seq_len, d), new_shift, new_wkv, new_conv, new_ffn


def _pad_to(a, axis, mult):
    pad = (-a.shape[axis]) % mult
    if not pad:
        return a
    widths = [(0, 0)] * a.ndim
    widths[axis] = (0, pad)
    return jnp.pad(a, widths)


def kernel(x_prompt, x_sample, state_shift, state_wkv, state_conv, state_ffn, norm1_g, w_in, mu_rkv, mu_lora, decay_w0, decay_w1, decay_w2, aaa_a0, aaa_a1, aaa_a2, gate_g1, gate_g2, k_k, k_a, r_k, ln_x_w, ln_x_b, conv_w, conv_norm_g, w_out, norm2_g, ffn_w1, ffn_conv_w, ffn_conv_b, ffn_w3, ffn_w2, final_norm_g):
    depth = w_in.shape[0]
    d = x_prompt.shape[-1]
    row = lambda v: v.reshape(1, -1).astype(F32)
    yp, ys = x_prompt, x_sample
    outs_p, outs_s = [], []
    for l in range(depth):
        p = dict(
            norm1_g=row(norm1_g[l]), w_in=w_in[l], w_conv=w_in[l][:, 3 * (d // 2):].astype(BF16),
            mu_rkv=row(mu_rkv[l]), mu_lora=mu_lora[l],
            decay_w0=row(decay_w0[l]), aaa_a0=row(aaa_a0[l]),
            dw1=decay_w1[l].astype(BF16), dw2=decay_w2[l].astype(BF16),
            aw1=aaa_a1[l].astype(BF16), aw2=aaa_a2[l].astype(BF16),
            gw1=_pad_to(gate_g1[l], 1, LANES).astype(BF16), gw2=_pad_to(gate_g2[l], 0, LANES).astype(BF16),
            k_k=row(k_k[l]), k_a=row(k_a[l]), r_k=row(r_k[l]), ln_x_w=row(ln_x_w[l]), ln_x_b=row(ln_x_b[l]),
            conv_w=conv_w[l], conv_norm_g=row(conv_norm_g[l]), w_out=w_out[l].astype(BF16),
            norm2_g=row(norm2_g[l]), ffn_w1=ffn_w1[l], ffn_w3=ffn_w3[l],
            ffn_conv_w=ffn_conv_w[l], ffn_conv_b=row(ffn_conv_b[l]), ffn_w2=ffn_w2[l].astype(BF16))
        yp, *st_p = _layer(yp, None, p, sample=False)
        ys, *st_s = _layer(ys, (state_shift[l], state_wkv[l], state_conv[l], state_ffn[l]), p, sample=True)
        outs_p.append(st_p)
        outs_s.append(st_s)
    fin = row(final_norm_g)
    y_prompt = _norm_call(yp.reshape(-1, d), fin, F32, tm=_tile(yp.shape[0] * yp.shape[1], 256)).reshape(yp.shape)
    y_sample = _norm_call(ys.reshape(-1, d), fin, F32, tm=_tile(ys.shape[0] * ys.shape[1], 256)).reshape(ys.shape)
    stack = lambda outs, k: jnp.stack([o[k] for o in outs])
    return (y_prompt, y_sample,
            stack(outs_p, 0), stack(outs_p, 1), stack(outs_p, 2), stack(outs_p, 3),
            stack(outs_s, 0), stack(outs_s, 1), stack(outs_s, 2), stack(outs_s, 3))
```

```python
import functools
import math

import jax
import jax.numpy as jnp
from jax import lax
from jax.experimental import pallas as pl
from jax.experimental.pallas import tpu as pltpu

F32 = jnp.float32
BF16 = jnp.bfloat16

HEAD_DIM = 64
LANES = 128
SUBLANES = 8
WKV_SUB = 64
RMS_EPS = 1e-6
GN_EPS = 64e-5
VMEM_LIMIT_BYTES = 56 * 1024 * 1024


def _params(*sem, flags=None):
    return pltpu.CompilerParams(dimension_semantics=sem, vmem_limit_bytes=VMEM_LIMIT_BYTES, flags=flags)


def _dot(a, b):
    return jnp.dot(a, b, preferred_element_type=F32)


def _dot_nt(a, b):
    return lax.dot_general(a, b, (((1,), (1,)), ((), ())), preferred_element_type=F32)


def _dot_tn(a, b):
    return lax.dot_general(a, b, (((0,), (0,)), ((), ())), preferred_element_type=F32)


def _sigmoid(z):
    return 1.0 / (1.0 + jnp.exp(-z))


def _split_dot(x, b_exact, terms, dot=_dot):
    acc = None
    rem = x
    for _ in range(terms):
        hi = rem.astype(BF16)
        part = dot(hi, b_exact)
        acc = part if acc is None else acc + part
        rem = rem - hi.astype(F32)
    return acc


def _row_iota(shape):
    return lax.broadcasted_iota(jnp.int32, shape, 0)


def _init_carry(carry_ref, j, i):
    @pl.when(i == 0)
    def _():
        carry_ref[j] = jnp.zeros(carry_ref.shape[1:], F32)


def _shifted_rows(u, carry, states):
    tm = u.shape[0]
    row = _row_iota(u.shape)
    r1 = pltpu.roll(u, 1, 0)
    r2 = pltpu.roll(u, 2, 0)
    if carry is not None:
        carry_ref, j, i, tiles_per_seq = carry

        c = carry_ref[j]
        keep = (i % tiles_per_seq) != 0
        first1 = jnp.where(keep, c[SUBLANES - 1:SUBLANES, :], 0.0)
        first2 = jnp.where(keep, c[SUBLANES - 2:SUBLANES - 1, :], 0.0)
        carry_ref[j] = u[tm - SUBLANES:, :]
        return jnp.where(row == 0, first1, r1), jnp.where(row == 0, first2, jnp.where(row == 1, first1, r2))
    first1, first2 = _expand_state_rows(states[0], states[1], tm)
    t = row % SUBLANES
    return jnp.where(t == 0, first1, r1), (None if first2 is None else jnp.where(t < 2, first2, r2))


def _expand_state_rows(st1, st0, tm):
    ns, n = st1.shape
    spread = lambda x: jnp.broadcast_to(x[:, None, :], (ns, SUBLANES, n)).reshape(tm, n)
    first1 = spread(st1)
    if st0 is None:
        return first1, None
    t = _row_iota((tm, n)) % SUBLANES
    return first1, jnp.where(t == 0, spread(st0), first1)


def _mix_kernel(*refs, sample, tiles_per_seq):
    if sample:
        (x_ref, ext_ref, g1_ref, mu_ref, w0_ref, a0_ref, dw1_ref, aw1_ref, gw1_ref, dw2_ref,
         aw2_ref, gw2_ref, xnb_ref, lw_ref, a_ref, g_ref, last_ref) = refs
    else:
        (x_ref, g1_ref, mu_ref, w0_ref, a0_ref, dw1_ref, aw1_ref, gw1_ref, dw2_ref,
         aw2_ref, gw2_ref, xnb_ref, lw_ref, a_ref, g_ref, last_ref, carry_ref) = refs
    if not sample:
        _init_carry(carry_ref, 0, pl.program_id(0))
    x = x_ref[...]
    xn = x * lax.rsqrt(jnp.mean(x * x, axis=-1, keepdims=True) + RMS_EPS) * g1_ref[...]
    tm = xn.shape[0]
    if sample:
        prev, _unused = _shifted_rows(xn, None, (ext_ref[...], None))
        last_ref[...] = xn
    else:
        prev, _unused = _shifted_rows(xn, (carry_ref, 0, pl.program_id(0), tiles_per_seq), None)
        last_ref[...] = xn[tm - SUBLANES:, :]
    dx = prev - xn
    xnb_ref[...] = xn.astype(BF16)
    xw = (xn + dx * mu_ref[0:1, :]).astype(BF16)
    hw = jnp.tanh(_dot(xw, dw1_ref[...]))
    wl = w0_ref[...] + _dot(hw.astype(BF16), dw2_ref[...])
    lw_ref[...] = -_sigmoid(wl) * math.exp(-0.5)
    xa = (xn + dx * mu_ref[1:2, :]).astype(BF16)
    ha = _dot(xa, aw1_ref[...])
    a_ref[...] = _sigmoid(a0_ref[...] + _dot(ha.astype(BF16), aw2_ref[...]))
    xg = (xn + dx * mu_ref[2:3, :]).astype(BF16)
    hg = _sigmoid(_dot(xg, gw1_ref[...]))
    g_ref[...] = _dot(hg.astype(BF16), gw2_ref[...])


def _mix_call(x2d, ext, p, *, sample, seq_len, tm):
    rows, d = x2d.shape
    g = p["dw2"].shape[1]
    n_i = rows // tm
    row_blk = lambda w: pl.BlockSpec((tm, w), lambda i: (i, 0))
    full = lambda a: pl.BlockSpec(a.shape, lambda i: (0, 0))
    weights = [p["norm1_g"], p["mu_lora"], p["decay_w0"], p["aaa_a0"], p["dw1"], p["aw1"], p["gw1"],
               p["dw2"], p["aw2"], p["gw2"]]
    state_blk = pl.BlockSpec((tm // seq_len, d), lambda i: (i, 0))
    in_specs = [row_blk(d)] + ([state_blk] if sample else []) + [full(w) for w in weights]
    args = [x2d] + ([ext] if sample else []) + weights
    last_rows = tm if sample else SUBLANES
    out_shape = [jax.ShapeDtypeStruct((rows, d), BF16)] + [jax.ShapeDtypeStruct((rows, g), F32)] * 3 + [
        jax.ShapeDtypeStruct((n_i * last_rows, d), F32)]
    out_specs = [row_blk(d), row_blk(g), row_blk(g), row_blk(g), pl.BlockSpec((last_rows, d), lambda i: (i, 0))]
    return pl.pallas_call(
        functools.partial(_mix_kernel, sample=sample, tiles_per_seq=max(seq_len // tm, 1)),
        grid=(n_i,), in_specs=in_specs, out_specs=out_specs, out_shape=out_shape,
        scratch_shapes=[] if sample else [pltpu.VMEM((1, SUBLANES, d), F32)],
        compiler_params=_params("arbitrary"), name="mix")(*args)


def _rkv_kernel(*refs, sample, tiles_per_seq):
    if sample:
        x_ref, w_ref, mu_ref, shift_ref, o_ref = refs
    else:
        x_ref, w_ref, mu_ref, o_ref, carry_ref = refs
    if not sample:
        _init_carry(carry_ref, pl.program_id(1), pl.program_id(0))
    w = w_ref[...].astype(BF16)
    p = _dot(x_ref[...], w)
    if sample:
        prev, _unused = _shifted_rows(p, None, (_dot(shift_ref[...], w), None))
    else:
        prev, _unused = _shifted_rows(p, (carry_ref, pl.program_id(1), pl.program_id(0), tiles_per_seq), None)
    o_ref[...] = p + mu_ref[...] * (prev - p)


def _rkv_call(xnb, w_in, mu, ext, *, sample, seq_len, tm, tn):
    rows, d = xnb.shape
    n = mu.shape[1]
    n_i, n_j = rows // tm, n // tn
    in_specs = [pl.BlockSpec((tm, d), lambda i, j: (i, 0)), pl.BlockSpec((d, tn), lambda i, j: (0, j)),
                pl.BlockSpec((1, tn), lambda i, j: (0, j))]
    args = [xnb, w_in, mu]
    if sample:
        in_specs.append(pl.BlockSpec((tm // seq_len, d), lambda i, j: (i, 0)))
        args.append(ext)
    return pl.pallas_call(
        functools.partial(_rkv_kernel, sample=sample, tiles_per_seq=max(seq_len // tm, 1)),
        grid=(n_i, n_j), in_specs=in_specs, out_specs=pl.BlockSpec((tm, tn), lambda i, j: (i, j)),
        out_shape=jax.ShapeDtypeStruct((rows, n), F32),
        scratch_shapes=[] if sample else [pltpu.VMEM((n_j, SUBLANES, tn), F32)],
        compiler_params=_params("arbitrary", "arbitrary"), name="rkv")(*args)


def _cast_kernel(w_ref, o_ref):
    o_ref[...] = w_ref[...].astype(BF16)


def _cast_cols_call(w, col0, n, tn):
    d = w.shape[0]
    off = col0 // tn
    return pl.pallas_call(
        _cast_kernel, grid=(n // tn,),
        in_specs=[pl.BlockSpec((d, tn), lambda j: (0, off + j))],
        out_specs=pl.BlockSpec((d, tn), lambda j: (0, j)),
        out_shape=jax.ShapeDtypeStruct((d, n), BF16),
        compiler_params=_params("arbitrary"), name="cast_cols")(w)


def _convbr_kernel(*refs, sample, tiles_per_seq):
    if sample:
        x_ref, wb_ref, wc_ref, wx_ref, cw_ref, st1_ref, st0_ref, y_ref, last_ref = refs
    else:
        x_ref, wb_ref, wc_ref, wx_ref, cw_ref, y_ref, last_ref, carry_ref = refs
    if not sample:
        _init_carry(carry_ref, pl.program_id(1), pl.program_id(0))
    x = x_ref[...]
    cx = _dot(x, wc_ref[...]) * _dot(x, wx_ref[...])
    tm = cx.shape[0]
    if sample:
        prev1, prev2 = _shifted_rows(cx, None, (st1_ref[...], st0_ref[...]))
        last_ref[...] = cx
    else:
        prev1, prev2 = _shifted_rows(cx, (carry_ref, pl.program_id(1), pl.program_id(0), tiles_per_seq), None)
        last_ref[...] = cx[tm - SUBLANES:, :]
    hconv = prev2 * cw_ref[0:1, :] + prev1 * cw_ref[1:2, :] + cx * cw_ref[2:3, :]
    y_ref[...] = _dot(x, wb_ref[...]) * hconv


def _convbr_call(xnb, w_in, conv_w, e1, e2, *, col0, sample, seq_len, tm, tn):
    rows, d = xnb.shape
    gc = conv_w.shape[1]
    n_i, n_j = rows // tm, gc // tn
    off = col0 // tn
    nb = gc // tn
    wspec = lambda k: pl.BlockSpec((d, tn), lambda i, j: (0, off + k * nb + j))
    tile = pl.BlockSpec((tm, tn), lambda i, j: (i, j))
    in_specs = [pl.BlockSpec((tm, d), lambda i, j: (i, 0)), wspec(0), wspec(1), wspec(2),
                pl.BlockSpec((3, tn), lambda i, j: (0, j))]
    args = [xnb, w_in, w_in, w_in, conv_w]
    if sample:
        sspec = pl.BlockSpec((tm // seq_len, tn), lambda i, j: (i, j))
        in_specs += [sspec, sspec]
        args += [e1, e2]
    last_rows = tm if sample else SUBLANES
    return pl.pallas_call(
        functools.partial(_convbr_kernel, sample=sample, tiles_per_seq=max(seq_len // tm, 1)),
        grid=(n_i, n_j), in_specs=in_specs,
        out_specs=[tile, pl.BlockSpec((last_rows, tn), lambda i, j: (i, j))],
        out_shape=[jax.ShapeDtypeStruct((rows, gc), F32), jax.ShapeDtypeStruct((n_i * last_rows, gc), F32)],
        scratch_shapes=[] if sample else [pltpu.VMEM((n_j, SUBLANES, tn), F32)],
        compiler_params=_params("arbitrary", "arbitrary"), name="convbr")(*args)


def _wkv_masks(c):
    m = 2 * WKV_SUB
    r2 = jnp.bitwise_and(_row_iota((m, m)), WKV_SUB - 1)
    c2 = jnp.bitwise_and(lax.broadcasted_iota(jnp.int32, (m, m), 1), WKV_SUB - 1)
    shift = int(math.log2(c))
    same = jnp.right_shift(r2, shift) == jnp.right_shift(c2, shift)
    eye = (_row_iota((m, m)) == lax.broadcasted_iota(jnp.int32, (m, m), 1)).astype(F32)
    lane_lo = lax.broadcasted_iota(jnp.int32, (WKV_SUB, LANES), 1) < HEAD_DIM
    return lane_lo, same & (c2 < r2), same & (c2 <= r2), eye


def _stack_heads(x, lane_lo):
    z = jnp.zeros_like(x)
    return jnp.concatenate([jnp.where(lane_lo, x, z), jnp.where(lane_lo, z, x)], axis=0)


def _fold_heads(x):
    return x[:WKV_SUB] + x[WKV_SUB:]


def _wkv_phase1(units, masks, c):
    lane_lo, strict, incl, eye = masks
    m = 2 * WKV_SUB
    phs = []
    for r, kt, v, kp, bt, lg, lw in units:
        e1 = jnp.exp(lg)
        e0 = jnp.exp(lg - lw)
        ei = jnp.exp(-lg)
        phs.append(dict(
            e1=e1, rt_st=_stack_heads(r * e1, lane_lo), kp_b=_stack_heads(kp * e0, lane_lo).astype(BF16),
            kh_st=_stack_heads(kt * ei, lane_lo), bh_st=_stack_heads(bt * ei, lane_lo),
            v_st=_stack_heads(v, lane_lo)))
    for ph in phs:
        ph["kh_b"], ph["bh_b"], ph["v_b"] = (ph[n].astype(BF16) for n in ("kh_st", "bh_st", "v_st"))
    gs = [_dot_nt(jnp.concatenate([ph["kp_b"], ph["rt_st"].astype(BF16)], axis=0),
                  jnp.concatenate([ph["bh_b"], ph["kh_b"]], axis=0)) for ph in phs]
    ps = [-jnp.where(strict, g[:m, :m], 0.0) for g in gs]
    ts = [eye + p for p in ps]
    n = 1
    while 2 * n < c:
        pbs = [p.astype(BF16) for p in ps]
        ps = [_dot(pb, pb) for pb in pbs]
        ts = [t + _dot(t.astype(BF16), p.astype(BF16)) for t, p in zip(ts, ps)]
        n *= 2
    abvs = [_dot(jnp.concatenate([jnp.where(strict, g[:m, m:], 0.0), jnp.where(incl, g[m:, m:], 0.0)],
                                 axis=0).astype(BF16), ph["v_b"]) for g, ph in zip(gs, phs)]
    tts = [_dot(t.astype(BF16), jnp.concatenate([ph["kp_b"], abv[:m].astype(BF16)], axis=1))
           for t, ph, abv in zip(ts, phs, abvs)]
    bbtts = [_dot(jnp.where(incl, g[m:, :m], 0.0).astype(BF16), tt.astype(BF16)) for g, tt in zip(gs, tts)]
    for ph, abv, tt, bbtt in zip(phs, abvs, tts, bbtts):
        ph["rq"] = _fold_heads(ph["rt_st"] - bbtt[:, :LANES])
        ph["ov"] = _fold_heads(abv[m:] - bbtt[:, LANES:])
        ph["tk_st"], ph["tav_st"] = tt[:, :LANES], tt[:, LANES:]
    return phs


def _wkv_transitions(phs):
    gams = [ph["e1"][WKV_SUB - 1:WKV_SUB, :] for ph in phs]
    kgs = [(_dot_tn(ph["tk_st"].astype(BF16), ph["bh_b"]) * gam).astype(BF16) for ph, gam in zip(phs, gams)]
    bcgs = [_dot_tn(jnp.concatenate([ph["v_b"], (-ph["tav_st"]).astype(BF16)], axis=0),
                    jnp.concatenate([ph["kh_b"], ph["bh_b"]], axis=0)) * gam for ph, gam in zip(phs, gams)]
    return gams, kgs, bcgs


def _wkv_units_small(states, units, c, lane_lo_c):
    rows = lambda u: slice(u * c, (u + 1) * c)
    rows_hi = lambda u: slice(WKV_SUB + u * c, WKV_SUB + (u + 1) * c)
    ous = [_dot_nt(jnp.concatenate([ph["rq"][rows(u)], ph["tkm"][rows(u)]], axis=0).astype(BF16), s.astype(BF16))
           for s, (ph, u) in zip(states, units)]
    outs, new_states = [], []
    for s, (ph, u), ou in zip(states, units, ous):
        pick = lambda x: jnp.concatenate([x[rows(u)], x[rows_hi(u)]], axis=0)
        uu = ou[c:] + ph["tav"][rows(u)]
        z = jnp.zeros_like(uu)
        u_st = jnp.concatenate([jnp.where(lane_lo_c, uu, z), jnp.where(lane_lo_c, z, uu)], axis=0)
        gam = ph["e1"][(u + 1) * c - 1:(u + 1) * c, :]
        lhs = jnp.concatenate([pick(ph["v_st"]), -u_st], axis=0).astype(BF16)
        rhs = (jnp.concatenate([pick(ph["kh_st"]), pick(ph["bh_st"])], axis=0) * gam).astype(BF16)
        outs.append(ou[:c] + ph["ov"][rows(u)])
        new_states.append(s * gam + _dot_tn(lhs, rhs))
    return outs, new_states


def _wkv_kernel(*refs, sample, chunk, n_pairs, n_sub):
    if sample:
        (r_ref, k_ref, v_ref, lw_ref, a_ref, g_ref, kk_ref, ka_ref, rk_ref, lnw_ref, lnb_ref, sin_ref,
         o_ref, sout_ref) = refs
    else:
        (r_ref, k_ref, v_ref, lw_ref, a_ref, g_ref, kk_ref, ka_ref, rk_ref, lnw_ref, lnb_ref,
         o_ref, sout_ref, s_s) = refs
    c = chunk
    lane128 = lax.broadcasted_iota(jnp.int32, (LANES, LANES), 1)
    row128 = _row_iota((LANES, LANES))
    blockdiag = (row128 < HEAD_DIM) == (lane128 < HEAD_DIM)
    block_ones = blockdiag.astype(BF16)
    seg_sum = lambda z: _split_dot(z, block_ones, 1)
    to_blockdiag = lambda x: jnp.where(blockdiag, jnp.concatenate([x, x], axis=1), 0.0)
    from_blockdiag = lambda s_: s_[:, :HEAD_DIM] + s_[:, HEAD_DIM:]
    masks = _wkv_masks(c)
    r2 = jnp.bitwise_and(_row_iota((WKV_SUB, WKV_SUB)), WKV_SUB - 1)
    c2 = lax.broadcasted_iota(jnp.int32, (WKV_SUB, WKV_SUB), 1)
    shift = int(math.log2(c))
    tril_b = ((jnp.right_shift(r2, shift) == jnp.right_shift(c2, shift)) & (c2 <= r2)).astype(BF16)
    lane_lo_c = lax.broadcasted_iota(jnp.int32, (c, LANES), 1) < HEAD_DIM

    if not sample:
        @pl.when(pl.program_id(2) == 0)
        def _():
            s_s[...] = jnp.zeros_like(s_s)

    lgs = []
    for sb in range(n_sub):
        rows = slice(sb * WKV_SUB, (sb + 1) * WKV_SUB)
        lgs.append(_split_dot(lw_ref[rows, :], tril_b, 3, dot=lambda x, b: _dot(b, x)))

    pairs = range(n_pairs)
    lanes = [slice(q * LANES, (q + 1) * LANES) for q in pairs]
    r = [r_ref[:, l] for l in lanes]
    k = [k_ref[:, l] for l in lanes]
    v = [v_ref[:, l] for l in lanes]
    a = [a_ref[:, l] for l in lanes]
    lw = [lw_ref[:, l] for l in lanes]
    kk = [k[q] * kk_ref[:, lanes[q]] for q in pairs]
    norms = [seg_sum(x * x) for x in kk]
    kk = [x / jnp.maximum(jnp.sqrt(n2), 1e-12) for x, n2 in zip(kk, norms)]
    kmod = [k[q] * (1.0 + (a[q] - 1.0) * ka_ref[:, lanes[q]]) for q in pairs]
    bt = [kk[q] * a[q] for q in pairs]

    subs = [(q, sb) for sb in range(n_sub) for q in pairs]
    units = []
    for q, sb in subs:
        rows = slice(sb * WKV_SUB, (sb + 1) * WKV_SUB)
        units.append((r[q][rows], kmod[q][rows], v[q][rows], kk[q][rows], bt[q][rows],
                      lgs[sb][:, lanes[q]], lw[q][rows]))
    phs = dict(zip(subs, _wkv_phase1(units, masks, c)))

    outs = {q: [] for q in pairs}
    if sample:
        per_sub = WKV_SUB // c
        todo = []
        for (q, sb), ph in phs.items():
            ph["tkm"] = _fold_heads(ph["tk_st"])
            ph["tav"] = _fold_heads(ph["tav_st"])
            todo += [(q, sb * per_sub + u, ph, u) for u in range(per_sub)]
        o_units, new_states = _wkv_units_small([to_blockdiag(sin_ref[seq, q]) for q, seq, _, _ in todo],
                                               [(ph, u) for _, _, ph, u in todo], c, lane_lo_c)
        for (q, seq, _, _), o, s_new in zip(todo, o_units, new_states):
            sout_ref[seq, q] = from_blockdiag(s_new)
            outs[q].append(o)
    else:
        gams, kgs, bcgs = _wkv_transitions([phs[key] for key in subs])
        trans = dict(zip(subs, zip(gams, kgs, bcgs)))
        s = [s_s[q] for q in pairs]
        for sb in range(n_sub):
            sb16 = [x.astype(BF16) for x in s]
            for q in pairs:
                outs[q].append(_dot_nt(phs[q, sb]["rq"].astype(BF16), sb16[q]) + phs[q, sb]["ov"])
            s = [s[q] * trans[q, sb][0] - _dot(sb16[q], trans[q, sb][1]) + trans[q, sb][2] for q in pairs]
        for q in pairs:
            s_s[q] = s[q]

        @pl.when(pl.program_id(2) == pl.num_programs(2) - 1)
        def _():
            for q in pairs:
                sout_ref[0, q] = from_blockdiag(s[q])

    o = [jnp.concatenate(outs[q], axis=0) for q in pairs]
    inv_n = 1.0 / HEAD_DIM
    mu = [_split_dot(x, block_ones, 1) * inv_n for x in o]
    dev = [x - m_ for x, m_ in zip(o, mu)]
    var = [_split_dot(x * x, block_ones, 1) * inv_n for x in dev]
    bonus = [seg_sum(r[q] * kmod[q] * rk_ref[:, lanes[q]]) * v[q] for q in pairs]
    for q in pairs:
        on = dev[q] * lax.rsqrt(var[q] + GN_EPS) * lnw_ref[:, lanes[q]] + lnb_ref[:, lanes[q]]
        o_ref[:, lanes[q]] = ((on + bonus[q]) * g_ref[:, lanes[q]]).astype(BF16)


def _wkv_call(rkv, lw, a, g, p, s_bd, *, sample, seq_len, t_blk, chunk, n_pairs):
    rows, gdim = lw.shape
    npair = gdim // LANES
    nseq = rows // seq_len
    width = n_pairs * LANES
    pair_blocks = npair // n_pairs
    head = [p["k_k"], p["k_a"], p["r_k"], p["ln_x_w"], p["ln_x_b"]]
    if sample:
        seq_blk = t_blk // seq_len
        grid = (nseq // seq_blk, pair_blocks)
        blk = lambda off: pl.BlockSpec((t_blk, width), lambda b, q, off=off: (b, off + q))
        hspec = pl.BlockSpec((1, width), lambda b, q: (0, q))
        sspec = pl.BlockSpec((seq_blk, n_pairs, LANES, HEAD_DIM), lambda b, q: (b, q, 0, 0))
        extra_specs, extra_args = [sspec], [s_bd]
        sem = ("arbitrary", "arbitrary")
        scratch = []
    else:
        nblk = seq_len // t_blk
        grid = (nseq, pair_blocks, nblk)
        blk = lambda off: pl.BlockSpec((t_blk, width), lambda b, q, n, off=off: (b * nblk + n, off + q))
        hspec = pl.BlockSpec((1, width), lambda b, q, n: (0, q))
        sspec = pl.BlockSpec((1, n_pairs, LANES, HEAD_DIM), lambda b, q, n: (b, q, 0, 0))
        extra_specs, extra_args = [], []
        sem = ("arbitrary", "arbitrary", "arbitrary")
        scratch = [pltpu.VMEM((n_pairs, LANES, LANES), F32)]
    in_specs = [blk(0), blk(pair_blocks), blk(2 * pair_blocks), blk(0), blk(0), blk(0)] + [hspec] * 5 + extra_specs
    args = [rkv, rkv, rkv, lw, a, g] + head + extra_args
    return pl.pallas_call(
        functools.partial(_wkv_kernel, sample=sample, chunk=chunk, n_pairs=n_pairs, n_sub=t_blk // WKV_SUB),
        grid=grid, in_specs=in_specs, out_specs=[blk(0), sspec],
        out_shape=[jax.ShapeDtypeStruct((rows, gdim), BF16),
                   jax.ShapeDtypeStruct((nseq, npair, LANES, HEAD_DIM), F32)],
        scratch_shapes=scratch, compiler_params=_params(*sem), name="wkv")(*args)


def _outproj_kernel(orw_ref, ocv_ref, w_ref, x_ref, o_ref):
    lhs = jnp.concatenate([orw_ref[...], ocv_ref[...]], axis=1)
    o_ref[...] = x_ref[...] + _dot(lhs, w_ref[...])


def _outproj_call(orw, ocv, w_out, x2d, *, tm, tn):
    rows, d = x2d.shape
    g, gc = orw.shape[1], ocv.shape[1]
    tile = pl.BlockSpec((tm, tn), lambda i, j: (i, j))
    return pl.pallas_call(
        _outproj_kernel, grid=(rows // tm, d // tn),
        in_specs=[pl.BlockSpec((tm, g), lambda i, j: (i, 0)), pl.BlockSpec((tm, gc), lambda i, j: (i, 0)),
                  pl.BlockSpec((d, tn), lambda i, j: (0, j)), tile],
        out_specs=tile, out_shape=jax.ShapeDtypeStruct((rows, d), F32),
        compiler_params=_params("arbitrary", "arbitrary"), name="outproj")(orw, ocv, w_out, x2d)


def _norm_kernel(x_ref, g_ref, o_ref):
    x = x_ref[...]
    y = x * lax.rsqrt(jnp.mean(x * x, axis=-1, keepdims=True) + RMS_EPS) * g_ref[...]
    o_ref[...] = y.astype(o_ref.dtype)


def _norm_call(x2d, g, dtype, *, tm):
    rows, d = x2d.shape
    blk = pl.BlockSpec((tm, d), lambda i: (i, 0))
    return pl.pallas_call(
        _norm_kernel, grid=(rows // tm,), in_specs=[blk, pl.BlockSpec((1, d), lambda i: (0, 0))],
        out_specs=blk, out_shape=jax.ShapeDtypeStruct((rows, d), dtype),
        compiler_params=_params("arbitrary"), name="rmsnorm")(x2d, g)


def _ffnup_kernel(*refs, sample, tiles_per_seq):
    if sample:
        x_ref, w1_ref, w3_ref, cw_ref, cb_ref, st1_ref, st0_ref, h_ref, last_ref = refs
        carry = None
    else:
        x_ref, w1_ref, w3_ref, cw_ref, cb_ref, h_ref, last_ref, carry_ref = refs
        carry = (carry_ref, pl.program_id(1), pl.program_id(0), tiles_per_seq)
        _init_carry(*carry[:3])
    tn = h_ref.shape[1]
    w13 = jnp.concatenate([w1_ref[...].astype(BF16), w3_ref[...].astype(BF16)], axis=1)
    uw = _dot(x_ref[...], w13)
    u = uw[:, :tn]
    tm = u.shape[0]
    states = (st1_ref[...], st0_ref[...]) if sample else None
    prev1, prev2 = _shifted_rows(u, carry, states)
    if sample:
        last_ref[...] = u.reshape(tm // SUBLANES, SUBLANES, tn)[:, SUBLANES - 2:, :]
    else:
        last_ref[...] = u[tm - SUBLANES:, :]
    z = prev2 * cw_ref[0:1, :] + prev1 * cw_ref[1:2, :] + u * cw_ref[2:3, :] + cb_ref[...]
    hz = 0.5 * z
    h_ref[...] = ((hz + hz * jnp.tanh(hz)) * uw[:, tn:]).astype(BF16)


def _ffnup_call(hn, w1, w3, conv_w, conv_b, st1, st0, *, sample, seq_len, tm, tn):
    rows, d = hn.shape
    dff = conv_w.shape[1]
    n_i, n_j = rows // tm, dff // tn
    tile = pl.BlockSpec((tm, tn), lambda i, j: (i, j))
    wspec = pl.BlockSpec((d, tn), lambda i, j: (0, j))
    in_specs = [pl.BlockSpec((tm, d), lambda i, j: (i, 0)), wspec, wspec,
                pl.BlockSpec((3, tn), lambda i, j: (0, j)), pl.BlockSpec((1, tn), lambda i, j: (0, j))]
    args = [hn, w1, w3, conv_w, conv_b]
    if sample:
        sspec = pl.BlockSpec((tm // seq_len, tn), lambda i, j: (i, j))
        in_specs += [sspec, sspec]
        args += [st1, st0]
    if sample:
        last_spec = pl.BlockSpec((tm // seq_len, 2, tn), lambda i, j: (i, 0, j))
        last_shape = jax.ShapeDtypeStruct((rows // seq_len, 2, dff), F32)
    else:
        last_spec = pl.BlockSpec((SUBLANES, tn), lambda i, j: (i, j))
        last_shape = jax.ShapeDtypeStruct((n_i * SUBLANES, dff), F32)
    return pl.pallas_call(
        functools.partial(_ffnup_kernel, sample=sample, tiles_per_seq=max(seq_len // tm, 1)),
        grid=(n_i, n_j), in_specs=in_specs,
        out_specs=[tile, last_spec], out_shape=[jax.ShapeDtypeStruct((rows, dff), BF16), last_shape],
        scratch_shapes=[] if sample else [pltpu.VMEM((n_j, SUBLANES, tn), F32)],
        compiler_params=_params("arbitrary", "arbitrary"), name="ffnup")(*args)


def _ffndown_kernel(h_ref, w_ref, x_ref, o_ref):
    o_ref[...] = x_ref[...] + _dot(h_ref[...], w_ref[...])


def _ffndown_call(h, w2, x1, *, tm, tn):
    rows, dff = h.shape
    d = w2.shape[1]
    tile = pl.BlockSpec((tm, tn), lambda i, j: (i, j))
    return pl.pallas_call(
        _ffndown_kernel, grid=(rows // tm, d // tn),
        in_specs=[pl.BlockSpec((tm, dff), lambda i, j: (i, 0)), pl.BlockSpec((dff, tn), lambda i, j: (0, j)), tile],
        out_specs=tile, out_shape=jax.ShapeDtypeStruct((rows, d), F32),
        compiler_params=_params("arbitrary", "arbitrary"), name="ffndown")(h, w2, x1)


def _tile(n, want):
    t = min(n, want)
    while n % t or (t % SUBLANES and t != n):
        t -= 1
    return t


def _col_tile(n, want):
    t = min(n, want)
    while n % t or t % LANES:
        t -= LANES
    return t


def _layer(x, states, p, *, sample):
    nseq, seq_len, d = x.shape
    rows = nseq * seq_len
    g = p["dw2"].shape[1]
    gc = d - g
    dff = p["ffn_conv_w"].shape[1]
    npair = g // LANES
    x2d = x.reshape(rows, d)
    big = dict(sample=sample, seq_len=seq_len)
    tm_big = rows if sample else _tile(seq_len, 1024)

    if sample:
        assert seq_len == SUBLANES, "the sample path shifts rows inside 8-row groups"
        shift, wkv, conv, ffn = states
        ext_x = shift
        ext_p = shift.astype(BF16)
        ce1, ce2 = conv[:, 1], conv[:, 0]
        fe1, fe2 = ffn[:, 1], ffn[:, 0]
        s_bd = wkv.reshape(nseq, npair, LANES, HEAD_DIM)
    else:
        ext_x = ext_p = ce1 = ce2 = fe1 = fe2 = s_bd = None

    tm_mix = _tile(rows if sample else seq_len, 128)
    xnb, lw, a, gate, xlast = _mix_call(x2d, ext_x, p, sample=sample, seq_len=seq_len, tm=tm_mix)
    rkv = _rkv_call(xnb, p["w_in"], p["mu_rkv"], ext_p, tm=tm_big, tn=_col_tile(3 * g, 512), **big)
    tm_conv = _tile(rows if sample else seq_len, 512)
    ycv, cxlast = _convbr_call(xnb, p["w_conv"], p["conv_w"], ce1, ce2, col0=0, tm=tm_conv,
                               tn=_col_tile(gc, 512), **big)
    if sample:
        t_blk, chunk = WKV_SUB, seq_len
    else:
        chunk = WKV_SUB
        t_blk = _tile(seq_len, 4 * WKV_SUB)
    orw, s_new = _wkv_call(rkv, lw, a, gate, p, s_bd, t_blk=t_blk, chunk=chunk,
                           n_pairs=math.gcd(npair, 4), **big)
    tm_e = _tile(rows, 512)
    ocv = _norm_call(ycv, p["conv_norm_g"], BF16, tm=_tile(rows, 512))
    x1 = _outproj_call(orw, ocv, p["w_out"], x2d, tm=_tile(rows, 1024), tn=_col_tile(d, 512))
    hn = _norm_call(x1, p["norm2_g"], BF16, tm=_tile(rows, 256))
    h, ulast = _ffnup_call(hn, p["ffn_w1"], p["ffn_w3"], p["ffn_conv_w"], p["ffn_conv_b"], fe1, fe2,
                           tm=tm_big, tn=_col_tile(dff, 256), **big)
    x2 = _ffndown_call(h, p["ffn_w2"], x1, tm=tm_e, tn=_col_tile(d, 512))

    def last_rows(arr, tile_rows, k):
        if sample:
            return arr.reshape(nseq, seq_len, -1)[:, seq_len - k:]
        per_seq = seq_len // tile_rows
        return arr.reshape(nseq, per_seq, SUBLANES, -1)[:, -1, SUBLANES - k:]
    new_shift = last_rows(xlast, tm_mix, 1)[:, 0]
    new_conv = last_rows(cxlast, tm_conv, 2)
    new_ffn = ulast if sample else last_rows(ulast, tm_big, 2)
    new_wkv = s_new.reshape(nseq, 2 * npair, HEAD_DIM, HEAD_DIM)
    return x2.reshape(nseq, seq_len, d), new_shift, new_wkv, new_conv, new_ffn


def _pad_to(a, axis, mult):
    pad = (-a.shape[axis]) % mult
    if not pad:
        return a
    widths = [(0, 0)] * a.ndim
    widths[axis] = (0, pad)
    return jnp.pad(a, widths)


def kernel(x_prompt, x_sample, state_shift, state_wkv, state_conv, state_ffn, norm1_g, w_in, mu_rkv, mu_lora, decay_w0, decay_w1, decay_w2, aaa_a0, aaa_a1, aaa_a2, gate_g1, gate_g2, k_k, k_a, r_k, ln_x_w, ln_x_b, conv_w, conv_norm_g, w_out, norm2_g, ffn_w1, ffn_conv_w, ffn_conv_b, ffn_w3, ffn_w2, final_norm_g):
    depth = w_in.shape[0]
    d = x_prompt.shape[-1]
    row = lambda v: v.reshape(1, -1).astype(F32)
    yp, ys = x_prompt, x_sample
    outs_p, outs_s = [], []
    for l in range(depth):
        p = dict(
            norm1_g=row(norm1_g[l]), w_in=w_in[l],
            w_conv=_cast_cols_call(w_in[l], 3 * (d // 2), w_in.shape[2] - 3 * (d // 2), _col_tile(3 * (d // 2), 512)),
            mu_rkv=row(mu_rkv[l]), mu_lora=mu_lora[l],
            decay_w0=row(decay_w0[l]), aaa_a0=row(aaa_a0[l]),
            dw1=decay_w1[l].astype(BF16), dw2=decay_w2[l].astype(BF16),
            aw1=aaa_a1[l].astype(BF16), aw2=aaa_a2[l].astype(BF16),
            gw1=_pad_to(gate_g1[l], 1, LANES).astype(BF16), gw2=_pad_to(gate_g2[l], 0, LANES).astype(BF16),
            k_k=row(k_k[l]), k_a=row(k_a[l]), r_k=row(r_k[l]), ln_x_w=row(ln_x_w[l]), ln_x_b=row(ln_x_b[l]),
            conv_w=conv_w[l], conv_norm_g=row(conv_norm_g[l]), w_out=w_out[l].astype(BF16),
            norm2_g=row(norm2_g[l]), ffn_w1=ffn_w1[l], ffn_w3=ffn_w3[l],
            ffn_conv_w=ffn_conv_w[l], ffn_conv_b=row(ffn_conv_b[l]), ffn_w2=ffn_w2[l].astype(BF16))
        yp, *st_p = _layer(yp, None, p, sample=False)
        ys, *st_s = _layer(ys, (state_shift[l], state_wkv[l], state_conv[l], state_ffn[l]), p, sample=True)
        outs_p.append(st_p)
        outs_s.append(st_s)
    fin = row(final_norm_g)
    y_prompt = _norm_call(yp.reshape(-1, d), fin, F32, tm=_tile(yp.shape[0] * yp.shape[1], 256)).reshape(yp.shape)
    y_sample = _norm_call(ys.reshape(-1, d), fin, F32, tm=_tile(ys.shape[0] * ys.shape[1], 256)).reshape(ys.shape)
    stack = lambda outs, k: jnp.stack([o[k] for o in outs])
    return (y_prompt, y_sample,
            stack(outs_p, 0), stack(outs_p, 1), stack(outs_p, 2), stack(outs_p, 3),
            stack(outs_s, 0), stack(outs_s, 1), stack(outs_s, 2), stack(outs_s, 3))
```

```python
import functools
import math

import jax
import jax.numpy as jnp
from jax import lax
from jax.experimental import pallas as pl
from jax.experimental.pallas import tpu as pltpu

F32 = jnp.float32
BF16 = jnp.bfloat16

HEAD_DIM = 64
LANES = 128
SUBLANES = 8
WKV_SUB = 64
RMS_EPS = 1e-6
GN_EPS = 64e-5
VMEM_LIMIT_BYTES = 56 * 1024 * 1024


def _params(*sem, flags=None):
    return pltpu.CompilerParams(dimension_semantics=sem, vmem_limit_bytes=VMEM_LIMIT_BYTES, flags=flags)


def _dot(a, b):
    return jnp.dot(a, b, preferred_element_type=F32)


def _dot_nt(a, b):
    return lax.dot_general(a, b, (((1,), (1,)), ((), ())), preferred_element_type=F32)


def _dot_tn(a, b):
    return lax.dot_general(a, b, (((0,), (0,)), ((), ())), preferred_element_type=F32)


def _sigmoid(z):
    return 1.0 / (1.0 + jnp.exp(-z))


def _split_dot(x, b_exact, terms, dot=_dot):
    acc = None
    rem = x
    for _ in range(terms):
        hi = rem.astype(BF16)
        part = dot(hi, b_exact)
        acc = part if acc is None else acc + part
        rem = rem - hi.astype(F32)
    return acc


def _row_iota(shape):
    return lax.broadcasted_iota(jnp.int32, shape, 0)


def _init_carry(carry_ref, j, i):
    @pl.when(i == 0)
    def _():
        carry_ref[j] = jnp.zeros(carry_ref.shape[1:], F32)


def _shifted_rows(u, carry, states):
    tm = u.shape[0]
    row = _row_iota(u.shape)
    r1 = pltpu.roll(u, 1, 0)
    r2 = pltpu.roll(u, 2, 0)
    if carry is not None:
        carry_ref, j, i, tiles_per_seq = carry

        c = carry_ref[j]
        keep = (i % tiles_per_seq) != 0
        first1 = jnp.where(keep, c[SUBLANES - 1:SUBLANES, :], 0.0)
        first2 = jnp.where(keep, c[SUBLANES - 2:SUBLANES - 1, :], 0.0)
        carry_ref[j] = u[tm - SUBLANES:, :]
        return jnp.where(row == 0, first1, r1), jnp.where(row == 0, first2, jnp.where(row == 1, first1, r2))
    first1, first2 = _expand_state_rows(states[0], states[1], tm)
    t = row % SUBLANES
    return jnp.where(t == 0, first1, r1), (None if first2 is None else jnp.where(t < 2, first2, r2))


def _expand_state_rows(st1, st0, tm):
    ns, n = st1.shape
    spread = lambda x: jnp.broadcast_to(x[:, None, :], (ns, SUBLANES, n)).reshape(tm, n)
    first1 = spread(st1)
    if st0 is None:
        return first1, None
    t = _row_iota((tm, n)) % SUBLANES
    return first1, jnp.where(t == 0, spread(st0), first1)


def _mix_kernel(*refs, sample, tiles_per_seq):
    if sample:
        (x_ref, ext_ref, g1_ref, mu_ref, w0_ref, a0_ref, dw1_ref, aw1_ref, gw1_ref, dw2_ref,
         aw2_ref, gw2_ref, xnb_ref, lw_ref, a_ref, g_ref, last_ref) = refs
    else:
        (x_ref, g1_ref, mu_ref, w0_ref, a0_ref, dw1_ref, aw1_ref, gw1_ref, dw2_ref,
         aw2_ref, gw2_ref, xnb_ref, lw_ref, a_ref, g_ref, last_ref, carry_ref) = refs
    if not sample:
        _init_carry(carry_ref, 0, pl.program_id(0))
    x = x_ref[...]
    xn = x * lax.rsqrt(jnp.mean(x * x, axis=-1, keepdims=True) + RMS_EPS) * g1_ref[...]
    tm = xn.shape[0]
    if sample:
        prev, _unused = _shifted_rows(xn, None, (ext_ref[...], None))
        last_ref[...] = xn
    else:
        prev, _unused = _shifted_rows(xn, (carry_ref, 0, pl.program_id(0), tiles_per_seq), None)
        last_ref[...] = xn[tm - SUBLANES:, :]
    dx = prev - xn
    xnb_ref[...] = xn.astype(BF16)
    xw = (xn + dx * mu_ref[0:1, :]).astype(BF16)
    hw = jnp.tanh(_dot(xw, dw1_ref[...]))
    wl = w0_ref[...] + _dot(hw.astype(BF16), dw2_ref[...])
    lw_ref[...] = -_sigmoid(wl) * math.exp(-0.5)
    xa = (xn + dx * mu_ref[1:2, :]).astype(BF16)
    ha = _dot(xa, aw1_ref[...])
    a_ref[...] = _sigmoid(a0_ref[...] + _dot(ha.astype(BF16), aw2_ref[...]))
    xg = (xn + dx * mu_ref[2:3, :]).astype(BF16)
    hg = _sigmoid(_dot(xg, gw1_ref[...]))
    g_ref[...] = _dot(hg.astype(BF16), gw2_ref[...])


def _mix_call(x2d, ext, p, *, sample, seq_len, tm):
    rows, d = x2d.shape
    g = p["dw2"].shape[1]
    n_i = rows // tm
    row_blk = lambda w: pl.BlockSpec((tm, w), lambda i: (i, 0))
    full = lambda a: pl.BlockSpec(a.shape, lambda i: (0, 0))
    weights = [p["norm1_g"], p["mu_lora"], p["decay_w0"], p["aaa_a0"], p["dw1"], p["aw1"], p["gw1"],
               p["dw2"], p["aw2"], p["gw2"]]
    state_blk = pl.BlockSpec((tm // seq_len, d), lambda i: (i, 0))
    in_specs = [row_blk(d)] + ([state_blk] if sample else []) + [full(w) for w in weights]
    args = [x2d] + ([ext] if sample else []) + weights
    last_rows = tm if sample else SUBLANES
    out_shape = [jax.ShapeDtypeStruct((rows, d), BF16)] + [jax.ShapeDtypeStruct((rows, g), F32)] * 3 + [
        jax.ShapeDtypeStruct((n_i * last_rows, d), F32)]
    out_specs = [row_blk(d), row_blk(g), row_blk(g), row_blk(g), pl.BlockSpec((last_rows, d), lambda i: (i, 0))]
    return pl.pallas_call(
        functools.partial(_mix_kernel, sample=sample, tiles_per_seq=max(seq_len // tm, 1)),
        grid=(n_i,), in_specs=in_specs, out_specs=out_specs, out_shape=out_shape,
        scratch_shapes=[] if sample else [pltpu.VMEM((1, SUBLANES, d), F32)],
        compiler_params=_params("arbitrary"), name="mix")(*args)


def _rkv_kernel(*refs, sample, tiles_per_seq):
    if sample:
        x_ref, w_ref, mu_ref, shift_ref, o_ref = refs
    else:
        x_ref, w_ref, mu_ref, o_ref, carry_ref = refs
    if not sample:
        _init_carry(carry_ref, pl.program_id(1), pl.program_id(0))
    w = w_ref[...].astype(BF16)
    p = _dot(x_ref[...], w)
    if sample:
        prev, _unused = _shifted_rows(p, None, (_dot(shift_ref[...], w), None))
    else:
        prev, _unused = _shifted_rows(p, (carry_ref, pl.program_id(1), pl.program_id(0), tiles_per_seq), None)
    o_ref[...] = p + mu_ref[...] * (prev - p)


def _rkv_call(xnb, w_in, mu, ext, *, sample, seq_len, tm, tn):
    rows, d = xnb.shape
    n = mu.shape[1]
    n_i, n_j = rows // tm, n // tn
    in_specs = [pl.BlockSpec((tm, d), lambda i, j: (i, 0)), pl.BlockSpec((d, tn), lambda i, j: (0, j)),
                pl.BlockSpec((1, tn), lambda i, j: (0, j))]
    args = [xnb, w_in, mu]
    if sample:
        in_specs.append(pl.BlockSpec((tm // seq_len, d), lambda i, j: (i, 0)))
        args.append(ext)
    return pl.pallas_call(
        functools.partial(_rkv_kernel, sample=sample, tiles_per_seq=max(seq_len // tm, 1)),
        grid=(n_i, n_j), in_specs=in_specs, out_specs=pl.BlockSpec((tm, tn), lambda i, j: (i, j)),
        out_shape=jax.ShapeDtypeStruct((rows, n), F32),
        scratch_shapes=[] if sample else [pltpu.VMEM((n_j, SUBLANES, tn), F32)],
        compiler_params=_params("arbitrary", "arbitrary"), name="rkv")(*args)


def _cast_kernel(w_ref, o_ref):
    o_ref[...] = w_ref[...].astype(BF16)


def _cast_cols_call(w, col0, n, tr):
    d = w.shape[0]
    assert col0 % n == 0
    return pl.pallas_call(
        _cast_kernel, grid=(d // tr,),
        in_specs=[pl.BlockSpec((tr, n), lambda i: (i, col0 // n))],
        out_specs=pl.BlockSpec((tr, n), lambda i: (i, 0)),
        out_shape=jax.ShapeDtypeStruct((d, n), BF16),
        compiler_params=_params("arbitrary"), name="cast_cols")(w)


def _convbr_kernel(*refs, sample, tiles_per_seq):
    if sample:
        x_ref, wb_ref, wc_ref, wx_ref, cw_ref, st1_ref, st0_ref, y_ref, last_ref = refs
    else:
        x_ref, wb_ref, wc_ref, wx_ref, cw_ref, y_ref, last_ref, carry_ref = refs
    if not sample:
        _init_carry(carry_ref, pl.program_id(1), pl.program_id(0))
    x = x_ref[...]
    cx = _dot(x, wc_ref[...]) * _dot(x, wx_ref[...])
    tm = cx.shape[0]
    if sample:
        prev1, prev2 = _shifted_rows(cx, None, (st1_ref[...], st0_ref[...]))
        last_ref[...] = cx
    else:
        prev1, prev2 = _shifted_rows(cx, (carry_ref, pl.program_id(1), pl.program_id(0), tiles_per_seq), None)
        last_ref[...] = cx[tm - SUBLANES:, :]
    hconv = prev2 * cw_ref[0:1, :] + prev1 * cw_ref[1:2, :] + cx * cw_ref[2:3, :]
    y_ref[...] = _dot(x, wb_ref[...]) * hconv


def _convbr_call(xnb, w_in, conv_w, e1, e2, *, col0, sample, seq_len, tm, tn):
    rows, d = xnb.shape
    gc = conv_w.shape[1]
    n_i, n_j = rows // tm, gc // tn
    off = col0 // tn
    nb = gc // tn
    wspec = lambda k: pl.BlockSpec((d, tn), lambda i, j: (0, off + k * nb + j))
    tile = pl.BlockSpec((tm, tn), lambda i, j: (i, j))
    in_specs = [pl.BlockSpec((tm, d), lambda i, j: (i, 0)), wspec(0), wspec(1), wspec(2),
                pl.BlockSpec((3, tn), lambda i, j: (0, j))]
    args = [xnb, w_in, w_in, w_in, conv_w]
    if sample:
        sspec = pl.BlockSpec((tm // seq_len, tn), lambda i, j: (i, j))
        in_specs += [sspec, sspec]
        args += [e1, e2]
    last_rows = tm if sample else SUBLANES
    return pl.pallas_call(
        functools.partial(_convbr_kernel, sample=sample, tiles_per_seq=max(seq_len // tm, 1)),
        grid=(n_i, n_j), in_specs=in_specs,
        out_specs=[tile, pl.BlockSpec((last_rows, tn), lambda i, j: (i, j))],
        out_shape=[jax.ShapeDtypeStruct((rows, gc), F32), jax.ShapeDtypeStruct((n_i * last_rows, gc), F32)],
        scratch_shapes=[] if sample else [pltpu.VMEM((n_j, SUBLANES, tn), F32)],
        compiler_params=_params("arbitrary", "arbitrary"), name="convbr")(*args)


def _wkv_masks(c):
    m = 2 * WKV_SUB
    r2 = jnp.bitwise_and(_row_iota((m, m)), WKV_SUB - 1)
    c2 = jnp.bitwise_and(lax.broadcasted_iota(jnp.int32, (m, m), 1), WKV_SUB - 1)
    shift = int(math.log2(c))
    same = jnp.right_shift(r2, shift) == jnp.right_shift(c2, shift)
    eye = (_row_iota((m, m)) == lax.broadcasted_iota(jnp.int32, (m, m), 1)).astype(F32)
    lane_lo = lax.broadcasted_iota(jnp.int32, (WKV_SUB, LANES), 1) < HEAD_DIM
    return lane_lo, same & (c2 < r2), same & (c2 <= r2), eye


def _stack_heads(x, lane_lo):
    z = jnp.zeros_like(x)
    return jnp.concatenate([jnp.where(lane_lo, x, z), jnp.where(lane_lo, z, x)], axis=0)


def _fold_heads(x):
    return x[:WKV_SUB] + x[WKV_SUB:]


def _wkv_phase1(units, masks, c):
    lane_lo, strict, incl, eye = masks
    m = 2 * WKV_SUB
    phs = []
    for r, kt, v, kp, bt, lg, lw in units:
        e1 = jnp.exp(lg)
        e0 = jnp.exp(lg - lw)
        ei = jnp.exp(-lg)
        phs.append(dict(
            e1=e1, rt_st=_stack_heads(r * e1, lane_lo), kp_b=_stack_heads(kp * e0, lane_lo).astype(BF16),
            kh_st=_stack_heads(kt * ei, lane_lo), bh_st=_stack_heads(bt * ei, lane_lo),
            v_st=_stack_heads(v, lane_lo)))
    for ph in phs:
        ph["kh_b"], ph["bh_b"], ph["v_b"] = (ph[n].astype(BF16) for n in ("kh_st", "bh_st", "v_st"))
    gs = [_dot_nt(jnp.concatenate([ph["kp_b"], ph["rt_st"].astype(BF16)], axis=0),
                  jnp.concatenate([ph["bh_b"], ph["kh_b"]], axis=0)) for ph in phs]
    ps = [-jnp.where(strict, g[:m, :m], 0.0) for g in gs]
    ts = [eye + p for p in ps]
    n = 1
    while 2 * n < c:
        pbs = [p.astype(BF16) for p in ps]
        ps = [_dot(pb, pb) for pb in pbs]
        ts = [t + _dot(t.astype(BF16), p.astype(BF16)) for t, p in zip(ts, ps)]
        n *= 2
    abvs = [_dot(jnp.concatenate([jnp.where(strict, g[:m, m:], 0.0), jnp.where(incl, g[m:, m:], 0.0)],
                                 axis=0).astype(BF16), ph["v_b"]) for g, ph in zip(gs, phs)]
    tts = [_dot(t.astype(BF16), jnp.concatenate([ph["kp_b"], abv[:m].astype(BF16)], axis=1))
           for t, ph, abv in zip(ts, phs, abvs)]
    bbtts = [_dot(jnp.where(incl, g[m:, :m], 0.0).astype(BF16), tt.astype(BF16)) for g, tt in zip(gs, tts)]
    for ph, abv, tt, bbtt in zip(phs, abvs, tts, bbtts):
        ph["rq"] = _fold_heads(ph["rt_st"] - bbtt[:, :LANES])
        ph["ov"] = _fold_heads(abv[m:] - bbtt[:, LANES:])
        ph["tk_st"], ph["tav_st"] = tt[:, :LANES], tt[:, LANES:]
    return phs


def _wkv_transitions(phs):
    gams = [ph["e1"][WKV_SUB - 1:WKV_SUB, :] for ph in phs]
    kgs = [(_dot_tn(ph["tk_st"].astype(BF16), ph["bh_b"]) * gam).astype(BF16) for ph, gam in zip(phs, gams)]
    bcgs = [_dot_tn(jnp.concatenate([ph["v_b"], (-ph["tav_st"]).astype(BF16)], axis=0),
                    jnp.concatenate([ph["kh_b"], ph["bh_b"]], axis=0)) * gam for ph, gam in zip(phs, gams)]
    return gams, kgs, bcgs


def _wkv_units_small(states, units, c, lane_lo_c):
    rows = lambda u: slice(u * c, (u + 1) * c)
    rows_hi = lambda u: slice(WKV_SUB + u * c, WKV_SUB + (u + 1) * c)
    ous = [_dot_nt(jnp.concatenate([ph["rq"][rows(u)], ph["tkm"][rows(u)]], axis=0).astype(BF16), s.astype(BF16))
           for s, (ph, u) in zip(states, units)]
    outs, new_states = [], []
    for s, (ph, u), ou in zip(states, units, ous):
        pick = lambda x: jnp.concatenate([x[rows(u)], x[rows_hi(u)]], axis=0)
        uu = ou[c:] + ph["tav"][rows(u)]
        z = jnp.zeros_like(uu)
        u_st = jnp.concatenate([jnp.where(lane_lo_c, uu, z), jnp.where(lane_lo_c, z, uu)], axis=0)
        gam = ph["e1"][(u + 1) * c - 1:(u + 1) * c, :]
        lhs = jnp.concatenate([pick(ph["v_st"]), -u_st], axis=0).astype(BF16)
        rhs = (jnp.concatenate([pick(ph["kh_st"]), pick(ph["bh_st"])], axis=0) * gam).astype(BF16)
        outs.append(ou[:c] + ph["ov"][rows(u)])
        new_states.append(s * gam + _dot_tn(lhs, rhs))
    return outs, new_states


def _wkv_kernel(*refs, sample, chunk, n_pairs, n_sub):
    if sample:
        (r_ref, k_ref, v_ref, lw_ref, a_ref, g_ref, kk_ref, ka_ref, rk_ref, lnw_ref, lnb_ref, sin_ref,
         o_ref, sout_ref) = refs
    else:
        (r_ref, k_ref, v_ref, lw_ref, a_ref, g_ref, kk_ref, ka_ref, rk_ref, lnw_ref, lnb_ref,
         o_ref, sout_ref, s_s) = refs
    c = chunk
    lane128 = lax.broadcasted_iota(jnp.int32, (LANES, LANES), 1)
    row128 = _row_iota((LANES, LANES))
    blockdiag = (row128 < HEAD_DIM) == (lane128 < HEAD_DIM)
    block_ones = blockdiag.astype(BF16)
    seg_sum = lambda z: _split_dot(z, block_ones, 2)
    to_blockdiag = lambda x: jnp.where(blockdiag, jnp.concatenate([x, x], axis=1), 0.0)
    from_blockdiag = lambda s_: s_[:, :HEAD_DIM] + s_[:, HEAD_DIM:]
    masks = _wkv_masks(c)
    r2 = jnp.bitwise_and(_row_iota((WKV_SUB, WKV_SUB)), WKV_SUB - 1)
    c2 = lax.broadcasted_iota(jnp.int32, (WKV_SUB, WKV_SUB), 1)
    shift = int(math.log2(c))
    tril_b = ((jnp.right_shift(r2, shift) == jnp.right_shift(c2, shift)) & (c2 <= r2)).astype(BF16)
    lane_lo_c = lax.broadcasted_iota(jnp.int32, (c, LANES), 1) < HEAD_DIM

    if not sample:
        @pl.when(pl.program_id(2) == 0)
        def _():
            s_s[...] = jnp.zeros_like(s_s)

    lgs = []
    for sb in range(n_sub):
        rows = slice(sb * WKV_SUB, (sb + 1) * WKV_SUB)
        lgs.append(_split_dot(lw_ref[rows, :], tril_b, 3, dot=lambda x, b: _dot(b, x)))

    pairs = range(n_pairs)
    lanes = [slice(q * LANES, (q + 1) * LANES) for q in pairs]
    r = [r_ref[:, l] for l in lanes]
    k = [k_ref[:, l] for l in lanes]
    v = [v_ref[:, l] for l in lanes]
    a = [a_ref[:, l] for l in lanes]
    lw = [lw_ref[:, l] for l in lanes]
    kk = [k[q] * kk_ref[:, lanes[q]] for q in pairs]
    norms = [seg_sum(x * x) for x in kk]
    kk = [x / jnp.maximum(jnp.sqrt(n2), 1e-12) for x, n2 in zip(kk, norms)]
    kmod = [k[q] * (1.0 + (a[q] - 1.0) * ka_ref[:, lanes[q]]) for q in pairs]
    bt = [kk[q] * a[q] for q in pairs]

    subs = [(q, sb) for sb in range(n_sub) for q in pairs]
    units = []
    for q, sb in subs:
        rows = slice(sb * WKV_SUB, (sb + 1) * WKV_SUB)
        units.append((r[q][rows], kmod[q][rows], v[q][rows], kk[q][rows], bt[q][rows],
                      lgs[sb][:, lanes[q]], lw[q][rows]))
    phs = dict(zip(subs, _wkv_phase1(units, masks, c)))

    outs = {q: [] for q in pairs}
    if sample:
        per_sub = WKV_SUB // c
        todo = []
        for (q, sb), ph in phs.items():
            ph["tkm"] = _fold_heads(ph["tk_st"])
            ph["tav"] = _fold_heads(ph["tav_st"])
            todo += [(q, sb * per_sub + u, ph, u) for u in range(per_sub)]
        o_units, new_states = _wkv_units_small([to_blockdiag(sin_ref[seq, q]) for q, seq, _, _ in todo],
                                               [(ph, u) for _, _, ph, u in todo], c, lane_lo_c)
        for (q, seq, _, _), o, s_new in zip(todo, o_units, new_states):
            sout_ref[seq, q] = from_blockdiag(s_new)
            outs[q].append(o)
    else:
        gams, kgs, bcgs = _wkv_transitions([phs[key] for key in subs])
        trans = dict(zip(subs, zip(gams, kgs, bcgs)))
        s = [s_s[q] for q in pairs]
        for sb in range(n_sub):
            sb16 = [x.astype(BF16) for x in s]
            for q in pairs:
                outs[q].append(_dot_nt(phs[q, sb]["rq"].astype(BF16), sb16[q]) + phs[q, sb]["ov"])
            s = [s[q] * trans[q, sb][0] - _dot(sb16[q], trans[q, sb][1]) + trans[q, sb][2] for q in pairs]
        for q in pairs:
            s_s[q] = s[q]

        @pl.when(pl.program_id(2) == pl.num_programs(2) - 1)
        def _():
            for q in pairs:
                sout_ref[0, q] = from_blockdiag(s[q])

    o = [jnp.concatenate(outs[q], axis=0) for q in pairs]
    inv_n = 1.0 / HEAD_DIM
    mu = [_split_dot(x, block_ones, 1) * inv_n for x in o]
    dev = [x - m_ for x, m_ in zip(o, mu)]
    var = [_split_dot(x * x, block_ones, 1) * inv_n for x in dev]
    bonus = [seg_sum(r[q] * kmod[q] * rk_ref[:, lanes[q]]) * v[q] for q in pairs]
    for q in pairs:
        on = dev[q] * lax.rsqrt(var[q] + GN_EPS) * lnw_ref[:, lanes[q]] + lnb_ref[:, lanes[q]]
        o_ref[:, lanes[q]] = ((on + bonus[q]) * g_ref[:, lanes[q]]).astype(BF16)


def _wkv_call(rkv, lw, a, g, p, s_bd, *, sample, seq_len, t_blk, chunk, n_pairs):
    rows, gdim = lw.shape
    npair = gdim // LANES
    nseq = rows // seq_len
    width = n_pairs * LANES
    pair_blocks = npair // n_pairs
    head = [p["k_k"], p["k_a"], p["r_k"], p["ln_x_w"], p["ln_x_b"]]
    if sample:
        seq_blk = t_blk // seq_len
        grid = (nseq // seq_blk, pair_blocks)
        blk = lambda off: pl.BlockSpec((t_blk, width), lambda b, q, off=off: (b, off + q))
        hspec = pl.BlockSpec((1, width), lambda b, q: (0, q))
        sspec = pl.BlockSpec((seq_blk, n_pairs, LANES, HEAD_DIM), lambda b, q: (b, q, 0, 0))
        extra_specs, extra_args = [sspec], [s_bd]
        sem = ("arbitrary", "arbitrary")
        scratch = []
    else:
        nblk = seq_len // t_blk
        grid = (nseq, pair_blocks, nblk)
        blk = lambda off: pl.BlockSpec((t_blk, width), lambda b, q, n, off=off: (b * nblk + n, off + q))
        hspec = pl.BlockSpec((1, width), lambda b, q, n: (0, q))
        sspec = pl.BlockSpec((1, n_pairs, LANES, HEAD_DIM), lambda b, q, n: (b, q, 0, 0))
        extra_specs, extra_args = [], []
        sem = ("arbitrary", "arbitrary", "arbitrary")
        scratch = [pltpu.VMEM((n_pairs, LANES, LANES), F32)]
    in_specs = [blk(0), blk(pair_blocks), blk(2 * pair_blocks), blk(0), blk(0), blk(0)] + [hspec] * 5 + extra_specs
    args = [rkv, rkv, rkv, lw, a, g] + head + extra_args
    return pl.pallas_call(
        functools.partial(_wkv_kernel, sample=sample, chunk=chunk, n_pairs=n_pairs, n_sub=t_blk // WKV_SUB),
        grid=grid, in_specs=in_specs, out_specs=[blk(0), sspec],
        out_shape=[jax.ShapeDtypeStruct((rows, gdim), BF16),
                   jax.ShapeDtypeStruct((nseq, npair, LANES, HEAD_DIM), F32)],
        scratch_shapes=scratch, compiler_params=_params(*sem), name="wkv")(*args)


def _outproj_kernel(orw_ref, ocv_ref, w_ref, x_ref, o_ref):
    lhs = jnp.concatenate([orw_ref[...], ocv_ref[...]], axis=1)
    o_ref[...] = x_ref[...] + _dot(lhs, w_ref[...])


def _outproj_call(orw, ocv, w_out, x2d, *, tm, tn):
    rows, d = x2d.shape
    g, gc = orw.shape[1], ocv.shape[1]
    tile = pl.BlockSpec((tm, tn), lambda i, j: (i, j))
    return pl.pallas_call(
        _outproj_kernel, grid=(rows // tm, d // tn),
        in_specs=[pl.BlockSpec((tm, g), lambda i, j: (i, 0)), pl.BlockSpec((tm, gc), lambda i, j: (i, 0)),
                  pl.BlockSpec((d, tn), lambda i, j: (0, j)), tile],
        out_specs=tile, out_shape=jax.ShapeDtypeStruct((rows, d), F32),
        compiler_params=_params("arbitrary", "arbitrary"), name="outproj")(orw, ocv, w_out, x2d)


def _norm_kernel(x_ref, g_ref, o_ref):
    x = x_ref[...]
    y = x * lax.rsqrt(jnp.mean(x * x, axis=-1, keepdims=True) + RMS_EPS) * g_ref[...]
    o_ref[...] = y.astype(o_ref.dtype)


def _norm_call(x2d, g, dtype, *, tm):
    rows, d = x2d.shape
    blk = pl.BlockSpec((tm, d), lambda i: (i, 0))
    return pl.pallas_call(
        _norm_kernel, grid=(rows // tm,), in_specs=[blk, pl.BlockSpec((1, d), lambda i: (0, 0))],
        out_specs=blk, out_shape=jax.ShapeDtypeStruct((rows, d), dtype),
        compiler_params=_params("arbitrary"), name="rmsnorm")(x2d, g)


def _ffnup_kernel(*refs, sample, tiles_per_seq):
    if sample:
        x_ref, w1_ref, w3_ref, cw_ref, cb_ref, st1_ref, st0_ref, h_ref, last_ref = refs
        carry = None
    else:
        x_ref, w1_ref, w3_ref, cw_ref, cb_ref, h_ref, last_ref, carry_ref = refs
        carry = (carry_ref, pl.program_id(1), pl.program_id(0), tiles_per_seq)
        _init_carry(*carry[:3])
    tn = h_ref.shape[1]
    w13 = jnp.concatenate([w1_ref[...].astype(BF16), w3_ref[...].astype(BF16)], axis=1)
    uw = _dot(x_ref[...], w13)
    u = uw[:, :tn]
    tm = u.shape[0]
    states = (st1_ref[...], st0_ref[...]) if sample else None
    prev1, prev2 = _shifted_rows(u, carry, states)
    if sample:
        last_ref[...] = u.reshape(tm // SUBLANES, SUBLANES, tn)[:, SUBLANES - 2:, :]
    else:
        last_ref[...] = u[tm - SUBLANES:, :]
    z = prev2 * cw_ref[0:1, :] + prev1 * cw_ref[1:2, :] + u * cw_ref[2:3, :] + cb_ref[...]
    hz = 0.5 * z
    h_ref[...] = ((hz + hz * jnp.tanh(hz)) * uw[:, tn:]).astype(BF16)


def _ffnup_call(hn, w1, w3, conv_w, conv_b, st1, st0, *, sample, seq_len, tm, tn):
    rows, d = hn.shape
    dff = conv_w.shape[1]
    n_i, n_j = rows // tm, dff // tn
    tile = pl.BlockSpec((tm, tn), lambda i, j: (i, j))
    wspec = pl.BlockSpec((d, tn), lambda i, j: (0, j))
    in_specs = [pl.BlockSpec((tm, d), lambda i, j: (i, 0)), wspec, wspec,
                pl.BlockSpec((3, tn), lambda i, j: (0, j)), pl.BlockSpec((1, tn), lambda i, j: (0, j))]
    args = [hn, w1, w3, conv_w, conv_b]
    if sample:
        sspec = pl.BlockSpec((tm // seq_len, tn), lambda i, j: (i, j))
        in_specs += [sspec, sspec]
        args += [st1, st0]
    if sample:
        last_spec = pl.BlockSpec((tm // seq_len, 2, tn), lambda i, j: (i, 0, j))
        last_shape = jax.ShapeDtypeStruct((rows // seq_len, 2, dff), F32)
    else:
        last_spec = pl.BlockSpec((SUBLANES, tn), lambda i, j: (i, j))
        last_shape = jax.ShapeDtypeStruct((n_i * SUBLANES, dff), F32)
    return pl.pallas_call(
        functools.partial(_ffnup_kernel, sample=sample, tiles_per_seq=max(seq_len // tm, 1)),
        grid=(n_i, n_j), in_specs=in_specs,
        out_specs=[tile, last_spec], out_shape=[jax.ShapeDtypeStruct((rows, dff), BF16), last_shape],
        scratch_shapes=[] if sample else [pltpu.VMEM((n_j, SUBLANES, tn), F32)],
        compiler_params=_params("arbitrary", "arbitrary"), name="ffnup")(*args)


def _ffndown_kernel(h_ref, w_ref, x_ref, o_ref):
    o_ref[...] = x_ref[...] + _dot(h_ref[...], w_ref[...])


def _ffndown_call(h, w2, x1, *, tm, tn):
    rows, dff = h.shape
    d = w2.shape[1]
    tile = pl.BlockSpec((tm, tn), lambda i, j: (i, j))
    return pl.pallas_call(
        _ffndown_kernel, grid=(rows // tm, d // tn),
        in_specs=[pl.BlockSpec((tm, dff), lambda i, j: (i, 0)), pl.BlockSpec((dff, tn), lambda i, j: (0, j)), tile],
        out_specs=tile, out_shape=jax.ShapeDtypeStruct((rows, d), F32),
        compiler_params=_params("arbitrary", "arbitrary"), name="ffndown")(h, w2, x1)


def _tile(n, want):
    t = min(n, want)
    while n % t or (t % SUBLANES and t != n):
        t -= 1
    return t


def _col_tile(n, want):
    t = min(n, want)
    while n % t or t % LANES:
        t -= LANES
    return t


def _layer(x, states, p, *, sample):
    nseq, seq_len, d = x.shape
    rows = nseq * seq_len
    g = p["dw2"].shape[1]
    gc = d - g
    dff = p["ffn_conv_w"].shape[1]
    npair = g // LANES
    x2d = x.reshape(rows, d)
    big = dict(sample=sample, seq_len=seq_len)
    tm_big = rows if sample else _tile(seq_len, 1024)

    if sample:
        assert seq_len == SUBLANES, "the sample path shifts rows inside 8-row groups"
        shift, wkv, conv, ffn = states
        ext_x = shift
        ext_p = shift.astype(BF16)
        ce1, ce2 = conv[:, 1], conv[:, 0]
        fe1, fe2 = ffn[:, 1], ffn[:, 0]
        s_bd = wkv.reshape(nseq, npair, LANES, HEAD_DIM)
    else:
        ext_x = ext_p = ce1 = ce2 = fe1 = fe2 = s_bd = None

    tm_mix = _tile(rows if sample else seq_len, 128)
    xnb, lw, a, gate, xlast = _mix_call(x2d, ext_x, p, sample=sample, seq_len=seq_len, tm=tm_mix)
    rkv = _rkv_call(xnb, p["w_in"], p["mu_rkv"], ext_p, tm=tm_big, tn=_col_tile(3 * g, 512), **big)
    tm_conv = _tile(rows if sample else seq_len, 512)
    ycv, cxlast = _convbr_call(xnb, p["w_conv"], p["conv_w"], ce1, ce2, col0=0, tm=tm_conv,
                               tn=_col_tile(gc, 512), **big)
    if sample:
        t_blk, chunk = WKV_SUB, seq_len
    else:
        chunk = WKV_SUB
        t_blk = _tile(seq_len, 4 * WKV_SUB)
    orw, s_new = _wkv_call(rkv, lw, a, gate, p, s_bd, t_blk=t_blk, chunk=chunk,
                           n_pairs=math.gcd(npair, 4), **big)
    tm_e = _tile(rows, 512)
    ocv = _norm_call(ycv, p["conv_norm_g"], BF16, tm=_tile(rows, 512))
    x1 = _outproj_call(orw, ocv, p["w_out"], x2d, tm=_tile(rows, 1024), tn=_col_tile(d, 512))
    hn = _norm_call(x1, p["norm2_g"], BF16, tm=_tile(rows, 512))
    h, ulast = _ffnup_call(hn, p["ffn_w1"], p["ffn_w3"], p["ffn_conv_w"], p["ffn_conv_b"], fe1, fe2,
                           tm=tm_big, tn=_col_tile(dff, 256), **big)
    x2 = _ffndown_call(h, p["ffn_w2"], x1, tm=tm_e, tn=_col_tile(d, 512))

    def last_rows(arr, tile_rows, k):
        if sample:
            return arr.reshape(nseq, seq_len, -1)[:, seq_len - k:]
        per_seq = seq_len // tile_rows
        return arr.reshape(nseq, per_seq, SUBLANES, -1)[:, -1, SUBLANES - k:]
    new_shift = last_rows(xlast, tm_mix, 1)[:, 0]
    new_conv = last_rows(cxlast, tm_conv, 2)
    new_ffn = ulast if sample else last_rows(ulast, tm_big, 2)
    new_wkv = s_new.reshape(nseq, 2 * npair, HEAD_DIM, HEAD_DIM)
    return x2.reshape(nseq, seq_len, d), new_shift, new_wkv, new_conv, new_ffn


def _pad_to(a, axis, mult):
    pad = (-a.shape[axis]) % mult
    if not pad:
        return a
    widths = [(0, 0)] * a.ndim
    widths[axis] = (0, pad)
    return jnp.pad(a, widths)


def kernel(x_prompt, x_sample, state_shift, state_wkv, state_conv, state_ffn, norm1_g, w_in, mu_rkv, mu_lora, decay_w0, decay_w1, decay_w2, aaa_a0, aaa_a1, aaa_a2, gate_g1, gate_g2, k_k, k_a, r_k, ln_x_w, ln_x_b, conv_w, conv_norm_g, w_out, norm2_g, ffn_w1, ffn_conv_w, ffn_conv_b, ffn_w3, ffn_w2, final_norm_g):
    depth = w_in.shape[0]
    d = x_prompt.shape[-1]
    row = lambda v: v.reshape(1, -1).astype(F32)
    yp, ys = x_prompt, x_sample
    outs_p, outs_s = [], []
    for l in range(depth):
        p = dict(
            norm1_g=row(norm1_g[l]), w_in=w_in[l],
            w_conv=_cast_cols_call(w_in[l], 3 * (d // 2), w_in.shape[2] - 3 * (d // 2), _tile(d, 512)),
            mu_rkv=row(mu_rkv[l]), mu_lora=mu_lora[l],
            decay_w0=row(decay_w0[l]), aaa_a0=row(aaa_a0[l]),
            dw1=decay_w1[l].astype(BF16), dw2=decay_w2[l].astype(BF16),
            aw1=aaa_a1[l].astype(BF16), aw2=aaa_a2[l].astype(BF16),
            gw1=_pad_to(gate_g1[l], 1, LANES).astype(BF16), gw2=_pad_to(gate_g2[l], 0, LANES).astype(BF16),
            k_k=row(k_k[l]), k_a=row(k_a[l]), r_k=row(r_k[l]), ln_x_w=row(ln_x_w[l]), ln_x_b=row(ln_x_b[l]),
            conv_w=conv_w[l], conv_norm_g=row(conv_norm_g[l]), w_out=w_out[l].astype(BF16),
            norm2_g=row(norm2_g[l]), ffn_w1=ffn_w1[l], ffn_w3=ffn_w3[l],
            ffn_conv_w=ffn_conv_w[l], ffn_conv_b=row(ffn_conv_b[l]), ffn_w2=ffn_w2[l].astype(BF16))
        yp, *st_p = _layer(yp, None, p, sample=False)
        ys, *st_s = _layer(ys, (state_shift[l], state_wkv[l], state_conv[l], state_ffn[l]), p, sample=True)
        outs_p.append(st_p)
        outs_s.append(st_s)
    fin = row(final_norm_g)
    y_prompt = _norm_call(yp.reshape(-1, d), fin, F32, tm=_tile(yp.shape[0] * yp.shape[1], 512)).reshape(yp.shape)
    y_sample = _norm_call(ys.reshape(-1, d), fin, F32, tm=_tile(ys.shape[0] * ys.shape[1], 512)).reshape(ys.shape)
    stack = lambda outs, k: jnp.stack([o[k] for o in outs])
    return (y_prompt, y_sample,
            stack(outs_p, 0), stack(outs_p, 1), stack(outs_p, 2), stack(outs_p, 3),
            stack(outs_s, 0), stack(outs_s, 1), stack(outs_s, 2), stack(outs_s, 3))
```

```python
import functools
import math

import jax
import jax.numpy as jnp
from jax import lax
from jax.experimental import pallas as pl
from jax.experimental.pallas import tpu as pltpu

F32 = jnp.float32
BF16 = jnp.bfloat16

HEAD_DIM = 64
LANES = 128
SUBLANES = 8
WKV_SUB = 64
RMS_EPS = 1e-6
GN_EPS = 64e-5
VMEM_LIMIT_BYTES = 60 * 1024 * 1024


def _params(*sem, flags=None):
    return pltpu.CompilerParams(dimension_semantics=sem, vmem_limit_bytes=VMEM_LIMIT_BYTES, flags=flags)


def _dot(a, b):
    return jnp.dot(a, b, preferred_element_type=F32)


def _dot_nt(a, b):
    return lax.dot_general(a, b, (((1,), (1,)), ((), ())), preferred_element_type=F32)


def _dot_tn(a, b):
    return lax.dot_general(a, b, (((0,), (0,)), ((), ())), preferred_element_type=F32)


def _sigmoid(z):
    return 1.0 / (1.0 + jnp.exp(-z))


def _split_dot(x, b_exact, terms, dot=_dot):
    acc = None
    rem = x
    for _ in range(terms):
        hi = rem.astype(BF16)
        part = dot(hi, b_exact)
        acc = part if acc is None else acc + part
        rem = rem - hi.astype(F32)
    return acc


def _row_iota(shape):
    return lax.broadcasted_iota(jnp.int32, shape, 0)


def _init_carry(carry_ref, j, i):
    @pl.when(i == 0)
    def _():
        carry_ref[j] = jnp.zeros(carry_ref.shape[1:], F32)


def _shifted_rows(u, carry, states):
    tm = u.shape[0]
    row = _row_iota(u.shape)
    r1 = pltpu.roll(u, 1, 0)
    r2 = pltpu.roll(u, 2, 0)
    if carry is not None:
        carry_ref, j, i, tiles_per_seq = carry

        c = carry_ref[j]
        keep = (i % tiles_per_seq) != 0
        first1 = jnp.where(keep, c[SUBLANES - 1:SUBLANES, :], 0.0)
        first2 = jnp.where(keep, c[SUBLANES - 2:SUBLANES - 1, :], 0.0)
        carry_ref[j] = u[tm - SUBLANES:, :]
        return jnp.where(row == 0, first1, r1), jnp.where(row == 0, first2, jnp.where(row == 1, first1, r2))
    first1, first2 = _expand_state_rows(states[0], states[1], tm)
    t = row % SUBLANES
    return jnp.where(t == 0, first1, r1), (None if first2 is None else jnp.where(t < 2, first2, r2))


def _expand_state_rows(st1, st0, tm):
    ns, n = st1.shape
    spread = lambda x: jnp.broadcast_to(x[:, None, :], (ns, SUBLANES, n)).reshape(tm, n)
    first1 = spread(st1)
    if st0 is None:
        return first1, None
    t = _row_iota((tm, n)) % SUBLANES
    return first1, jnp.where(t == 0, spread(st0), first1)


def _mix_kernel(*refs, sample, tiles_per_seq):
    if sample:
        (x_ref, ext_ref, g1_ref, mu_ref, w0_ref, a0_ref, dw1_ref, aw1_ref, gw1_ref, dw2_ref,
         aw2_ref, gw2_ref, xnb_ref, lw_ref, a_ref, g_ref, last_ref) = refs
    else:
        (x_ref, g1_ref, mu_ref, w0_ref, a0_ref, dw1_ref, aw1_ref, gw1_ref, dw2_ref,
         aw2_ref, gw2_ref, xnb_ref, lw_ref, a_ref, g_ref, last_ref, carry_ref) = refs
    if not sample:
        _init_carry(carry_ref, 0, pl.program_id(0))
    x = x_ref[...]
    xn = x * lax.rsqrt(jnp.mean(x * x, axis=-1, keepdims=True) + RMS_EPS) * g1_ref[...]
    tm = xn.shape[0]
    if sample:
        prev, _unused = _shifted_rows(xn, None, (ext_ref[...], None))
        last_ref[...] = xn
    else:
        prev, _unused = _shifted_rows(xn, (carry_ref, 0, pl.program_id(0), tiles_per_seq), None)
        last_ref[...] = xn[tm - SUBLANES:, :]
    dx = prev - xn
    xnb_ref[...] = xn.astype(BF16)
    xw = (xn + dx * mu_ref[0:1, :]).astype(BF16)
    hw = jnp.tanh(_dot(xw, dw1_ref[...]))
    wl = w0_ref[...] + _dot(hw.astype(BF16), dw2_ref[...])
    lw_ref[...] = -_sigmoid(wl) * math.exp(-0.5)
    xa = (xn + dx * mu_ref[1:2, :]).astype(BF16)
    ha = _dot(xa, aw1_ref[...])
    a_ref[...] = _sigmoid(a0_ref[...] + _dot(ha.astype(BF16), aw2_ref[...]))
    xg = (xn + dx * mu_ref[2:3, :]).astype(BF16)
    hg = _sigmoid(_dot(xg, gw1_ref[...]))
    g_ref[...] = _dot(hg.astype(BF16), gw2_ref[...])


def _mix_call(x2d, ext, p, *, sample, seq_len, tm):
    rows, d = x2d.shape
    g = p["dw2"].shape[1]
    n_i = rows // tm
    row_blk = lambda w: pl.BlockSpec((tm, w), lambda i: (i, 0))
    full = lambda a: pl.BlockSpec(a.shape, lambda i: (0, 0))
    weights = [p["norm1_g"], p["mu_lora"], p["decay_w0"], p["aaa_a0"], p["dw1"], p["aw1"], p["gw1"],
               p["dw2"], p["aw2"], p["gw2"]]
    state_blk = pl.BlockSpec((tm // seq_len, d), lambda i: (i, 0))
    in_specs = [row_blk(d)] + ([state_blk] if sample else []) + [full(w) for w in weights]
    args = [x2d] + ([ext] if sample else []) + weights
    last_rows = tm if sample else SUBLANES
    out_shape = [jax.ShapeDtypeStruct((rows, d), BF16)] + [jax.ShapeDtypeStruct((rows, g), F32)] * 3 + [
        jax.ShapeDtypeStruct((n_i * last_rows, d), F32)]
    out_specs = [row_blk(d), row_blk(g), row_blk(g), row_blk(g), pl.BlockSpec((last_rows, d), lambda i: (i, 0))]
    return pl.pallas_call(
        functools.partial(_mix_kernel, sample=sample, tiles_per_seq=max(seq_len // tm, 1)),
        grid=(n_i,), in_specs=in_specs, out_specs=out_specs, out_shape=out_shape,
        scratch_shapes=[] if sample else [pltpu.VMEM((1, SUBLANES, d), F32)],
        compiler_params=_params("arbitrary"), name="mix")(*args)


def _rkv_kernel(*refs, sample, tiles_per_seq):
    if sample:
        x_ref, w_ref, mu_ref, shift_ref, o_ref = refs
    else:
        x_ref, w_ref, mu_ref, o_ref, carry_ref = refs
    if not sample:
        _init_carry(carry_ref, pl.program_id(1), pl.program_id(0))
    w = w_ref[...].astype(BF16)
    p = _dot(x_ref[...], w)
    if sample:
        prev, _unused = _shifted_rows(p, None, (_dot(shift_ref[...], w), None))
    else:
        prev, _unused = _shifted_rows(p, (carry_ref, pl.program_id(1), pl.program_id(0), tiles_per_seq), None)
    o_ref[...] = p + mu_ref[...] * (prev - p)


def _rkv_call(xnb, w_in, mu, ext, *, sample, seq_len, tm, tn):
    rows, d = xnb.shape
    n = mu.shape[1]
    n_i, n_j = rows // tm, n // tn
    in_specs = [pl.BlockSpec((tm, d), lambda i, j: (i, 0)), pl.BlockSpec((d, tn), lambda i, j: (0, j)),
                pl.BlockSpec((1, tn), lambda i, j: (0, j))]
    args = [xnb, w_in, mu]
    if sample:
        in_specs.append(pl.BlockSpec((tm // seq_len, d), lambda i, j: (i, 0)))
        args.append(ext)
    return pl.pallas_call(
        functools.partial(_rkv_kernel, sample=sample, tiles_per_seq=max(seq_len // tm, 1)),
        grid=(n_i, n_j), in_specs=in_specs, out_specs=pl.BlockSpec((tm, tn), lambda i, j: (i, j)),
        out_shape=jax.ShapeDtypeStruct((rows, n), F32),
        scratch_shapes=[] if sample else [pltpu.VMEM((n_j, SUBLANES, tn), F32)],
        compiler_params=_params("arbitrary", "arbitrary"), name="rkv")(*args)


def _cast_kernel(w_ref, o_ref):
    o_ref[...] = w_ref[...].astype(BF16)


def _cast_cols_call(w, col0, n, tr):
    d = w.shape[0]
    assert col0 % n == 0
    return pl.pallas_call(
        _cast_kernel, grid=(d // tr,),
        in_specs=[pl.BlockSpec((tr, n), lambda i: (i, col0 // n))],
        out_specs=pl.BlockSpec((tr, n), lambda i: (i, 0)),
        out_shape=jax.ShapeDtypeStruct((d, n), BF16),
        compiler_params=_params("arbitrary"), name="cast_cols")(w)


def _convbr_kernel(*refs, sample, tiles_per_seq):
    if sample:
        x_ref, wb_ref, wc_ref, wx_ref, cw_ref, st1_ref, st0_ref, y_ref, last_ref = refs
    else:
        x_ref, wb_ref, wc_ref, wx_ref, cw_ref, y_ref, last_ref, carry_ref = refs
    if not sample:
        _init_carry(carry_ref, pl.program_id(1), pl.program_id(0))
    x = x_ref[...]
    cx = _dot(x, wc_ref[...]) * _dot(x, wx_ref[...])
    tm = cx.shape[0]
    if sample:
        prev1, prev2 = _shifted_rows(cx, None, (st1_ref[...], st0_ref[...]))
        last_ref[...] = cx
    else:
        prev1, prev2 = _shifted_rows(cx, (carry_ref, pl.program_id(1), pl.program_id(0), tiles_per_seq), None)
        last_ref[...] = cx[tm - SUBLANES:, :]
    hconv = prev2 * cw_ref[0:1, :] + prev1 * cw_ref[1:2, :] + cx * cw_ref[2:3, :]
    y_ref[...] = _dot(x, wb_ref[...]) * hconv


def _convbr_call(xnb, w_in, conv_w, e1, e2, *, col0, sample, seq_len, tm, tn):
    rows, d = xnb.shape
    gc = conv_w.shape[1]
    n_i, n_j = rows // tm, gc // tn
    off = col0 // tn
    nb = gc // tn
    wspec = lambda k: pl.BlockSpec((d, tn), lambda i, j: (0, off + k * nb + j))
    tile = pl.BlockSpec((tm, tn), lambda i, j: (i, j))
    in_specs = [pl.BlockSpec((tm, d), lambda i, j: (i, 0)), wspec(0), wspec(1), wspec(2),
                pl.BlockSpec((3, tn), lambda i, j: (0, j))]
    args = [xnb, w_in, w_in, w_in, conv_w]
    if sample:
        sspec = pl.BlockSpec((tm // seq_len, tn), lambda i, j: (i, j))
        in_specs += [sspec, sspec]
        args += [e1, e2]
    last_rows = tm if sample else SUBLANES
    return pl.pallas_call(
        functools.partial(_convbr_kernel, sample=sample, tiles_per_seq=max(seq_len // tm, 1)),
        grid=(n_i, n_j), in_specs=in_specs,
        out_specs=[tile, pl.BlockSpec((last_rows, tn), lambda i, j: (i, j))],
        out_shape=[jax.ShapeDtypeStruct((rows, gc), F32), jax.ShapeDtypeStruct((n_i * last_rows, gc), F32)],
        scratch_shapes=[] if sample else [pltpu.VMEM((n_j, SUBLANES, tn), F32)],
        compiler_params=_params("arbitrary", "arbitrary"), name="convbr")(*args)


def _wkv_masks(c):
    m = 2 * WKV_SUB
    r2 = jnp.bitwise_and(_row_iota((m, m)), WKV_SUB - 1)
    c2 = jnp.bitwise_and(lax.broadcasted_iota(jnp.int32, (m, m), 1), WKV_SUB - 1)
    shift = int(math.log2(c))
    same = jnp.right_shift(r2, shift) == jnp.right_shift(c2, shift)
    eye = (_row_iota((m, m)) == lax.broadcasted_iota(jnp.int32, (m, m), 1)).astype(F32)
    lane_lo = lax.broadcasted_iota(jnp.int32, (WKV_SUB, LANES), 1) < HEAD_DIM
    return lane_lo, same & (c2 < r2), same & (c2 <= r2), eye


def _stack_heads(x, lane_lo):
    z = jnp.zeros_like(x)
    return jnp.concatenate([jnp.where(lane_lo, x, z), jnp.where(lane_lo, z, x)], axis=0)


def _fold_heads(x):
    return x[:WKV_SUB] + x[WKV_SUB:]


def _wkv_phase1(units, masks, c):
    lane_lo, strict, incl, eye = masks
    m = 2 * WKV_SUB
    phs = []
    for r, kt, v, kp, bt, lg, lw in units:
        e1 = jnp.exp(lg)
        e0 = jnp.exp(lg - lw)
        ei = jnp.exp(-lg)
        phs.append(dict(
            e1=e1, rt_st=_stack_heads(r * e1, lane_lo), kp_b=_stack_heads(kp * e0, lane_lo).astype(BF16),
            kh_st=_stack_heads(kt * ei, lane_lo), bh_st=_stack_heads(bt * ei, lane_lo),
            v_st=_stack_heads(v, lane_lo)))
    for ph in phs:
        ph["kh_b"], ph["bh_b"], ph["v_b"] = (ph[n].astype(BF16) for n in ("kh_st", "bh_st", "v_st"))
    gs = [_dot_nt(jnp.concatenate([ph["kp_b"], ph["rt_st"].astype(BF16)], axis=0),
                  jnp.concatenate([ph["bh_b"], ph["kh_b"]], axis=0)) for ph in phs]
    ps = [-jnp.where(strict, g[:m, :m], 0.0) for g in gs]
    ts = [eye + p for p in ps]
    n = 1
    while 2 * n < c:
        pbs = [p.astype(BF16) for p in ps]
        ps = [_dot(pb, pb) for pb in pbs]
        ts = [t + _dot(t.astype(BF16), p.astype(BF16)) for t, p in zip(ts, ps)]
        n *= 2
    abvs = [_dot(jnp.concatenate([jnp.where(strict, g[:m, m:], 0.0), jnp.where(incl, g[m:, m:], 0.0)],
                                 axis=0).astype(BF16), ph["v_b"]) for g, ph in zip(gs, phs)]
    tts = [_dot(t.astype(BF16), jnp.concatenate([ph["kp_b"], abv[:m].astype(BF16)], axis=1))
           for t, ph, abv in zip(ts, phs, abvs)]
    bbtts = [_dot(jnp.where(incl, g[m:, :m], 0.0).astype(BF16), tt.astype(BF16)) for g, tt in zip(gs, tts)]
    for ph, abv, tt, bbtt in zip(phs, abvs, tts, bbtts):
        ph["rq"] = _fold_heads(ph["rt_st"] - bbtt[:, :LANES])
        ph["ov"] = _fold_heads(abv[m:] - bbtt[:, LANES:])
        ph["tk_st"], ph["tav_st"] = tt[:, :LANES], tt[:, LANES:]
    return phs


def _wkv_transitions(phs):
    gams = [ph["e1"][WKV_SUB - 1:WKV_SUB, :] for ph in phs]
    kgs = [(_dot_tn(ph["tk_st"].astype(BF16), ph["bh_b"]) * gam).astype(BF16) for ph, gam in zip(phs, gams)]
    bcgs = [_dot_tn(jnp.concatenate([ph["v_b"], (-ph["tav_st"]).astype(BF16)], axis=0),
                    jnp.concatenate([ph["kh_b"], ph["bh_b"]], axis=0)) * gam for ph, gam in zip(phs, gams)]
    return gams, kgs, bcgs


def _wkv_units_small(states, units, c, lane_lo_c):
    rows = lambda u: slice(u * c, (u + 1) * c)
    rows_hi = lambda u: slice(WKV_SUB + u * c, WKV_SUB + (u + 1) * c)
    ous = [_dot_nt(jnp.concatenate([ph["rq"][rows(u)], ph["tkm"][rows(u)]], axis=0).astype(BF16), s.astype(BF16))
           for s, (ph, u) in zip(states, units)]
    outs, new_states = [], []
    for s, (ph, u), ou in zip(states, units, ous):
        pick = lambda x: jnp.concatenate([x[rows(u)], x[rows_hi(u)]], axis=0)
        uu = ou[c:] + ph["tav"][rows(u)]
        z = jnp.zeros_like(uu)
        u_st = jnp.concatenate([jnp.where(lane_lo_c, uu, z), jnp.where(lane_lo_c, z, uu)], axis=0)
        gam = ph["e1"][(u + 1) * c - 1:(u + 1) * c, :]
        lhs = jnp.concatenate([pick(ph["v_st"]), -u_st], axis=0).astype(BF16)
        rhs = (jnp.concatenate([pick(ph["kh_st"]), pick(ph["bh_st"])], axis=0) * gam).astype(BF16)
        outs.append(ou[:c] + ph["ov"][rows(u)])
        new_states.append(s * gam + _dot_tn(lhs, rhs))
    return outs, new_states


def _wkv_kernel(*refs, sample, chunk, n_pairs, n_sub):
    if sample:
        (r_ref, k_ref, v_ref, lw_ref, a_ref, g_ref, kk_ref, ka_ref, rk_ref, lnw_ref, lnb_ref, sin_ref,
         o_ref, sout_ref) = refs
    else:
        (r_ref, k_ref, v_ref, lw_ref, a_ref, g_ref, kk_ref, ka_ref, rk_ref, lnw_ref, lnb_ref,
         o_ref, sout_ref, s_s) = refs
    c = chunk
    lane128 = lax.broadcasted_iota(jnp.int32, (LANES, LANES), 1)
    row128 = _row_iota((LANES, LANES))
    blockdiag = (row128 < HEAD_DIM) == (lane128 < HEAD_DIM)
    block_ones = blockdiag.astype(BF16)
    seg_sum = lambda z: _split_dot(z, block_ones, 2)
    to_blockdiag = lambda x: jnp.where(blockdiag, jnp.concatenate([x, x], axis=1), 0.0)
    from_blockdiag = lambda s_: s_[:, :HEAD_DIM] + s_[:, HEAD_DIM:]
    masks = _wkv_masks(c)
    r2 = jnp.bitwise_and(_row_iota((WKV_SUB, WKV_SUB)), WKV_SUB - 1)
    c2 = lax.broadcasted_iota(jnp.int32, (WKV_SUB, WKV_SUB), 1)
    shift = int(math.log2(c))
    tril_b = ((jnp.right_shift(r2, shift) == jnp.right_shift(c2, shift)) & (c2 <= r2)).astype(BF16)
    lane_lo_c = lax.broadcasted_iota(jnp.int32, (c, LANES), 1) < HEAD_DIM

    if not sample:
        @pl.when(pl.program_id(2) == 0)
        def _():
            s_s[...] = jnp.zeros_like(s_s)

    lgs = []
    for sb in range(n_sub):
        rows = slice(sb * WKV_SUB, (sb + 1) * WKV_SUB)
        lgs.append(_split_dot(lw_ref[rows, :], tril_b, 3, dot=lambda x, b: _dot(b, x)))

    pairs = range(n_pairs)
    lanes = [slice(q * LANES, (q + 1) * LANES) for q in pairs]
    r = [r_ref[:, l] for l in lanes]
    k = [k_ref[:, l] for l in lanes]
    v = [v_ref[:, l] for l in lanes]
    a = [a_ref[:, l] for l in lanes]
    lw = [lw_ref[:, l] for l in lanes]
    kk = [k[q] * kk_ref[:, lanes[q]] for q in pairs]
    norms = [seg_sum(x * x) for x in kk]
    kk = [x / jnp.maximum(jnp.sqrt(n2), 1e-12) for x, n2 in zip(kk, norms)]
    kmod = [k[q] * (1.0 + (a[q] - 1.0) * ka_ref[:, lanes[q]]) for q in pairs]
    bt = [kk[q] * a[q] for q in pairs]

    subs = [(q, sb) for sb in range(n_sub) for q in pairs]
    units = []
    for q, sb in subs:
        rows = slice(sb * WKV_SUB, (sb + 1) * WKV_SUB)
        units.append((r[q][rows], kmod[q][rows], v[q][rows], kk[q][rows], bt[q][rows],
                      lgs[sb][:, lanes[q]], lw[q][rows]))
    phs = dict(zip(subs, _wkv_phase1(units, masks, c)))

    outs = {q: [] for q in pairs}
    if sample:
        per_sub = WKV_SUB // c
        todo = []
        for (q, sb), ph in phs.items():
            ph["tkm"] = _fold_heads(ph["tk_st"])
            ph["tav"] = _fold_heads(ph["tav_st"])
            todo += [(q, sb * per_sub + u, ph, u) for u in range(per_sub)]
        o_units, new_states = _wkv_units_small([to_blockdiag(sin_ref[seq, q]) for q, seq, _, _ in todo],
                                               [(ph, u) for _, _, ph, u in todo], c, lane_lo_c)
        for (q, seq, _, _), o, s_new in zip(todo, o_units, new_states):
            sout_ref[seq, q] = from_blockdiag(s_new)
            outs[q].append(o)
    else:
        gams, kgs, bcgs = _wkv_transitions([phs[key] for key in subs])
        trans = dict(zip(subs, zip(gams, kgs, bcgs)))
        s = [s_s[q] for q in pairs]
        for sb in range(n_sub):
            sb16 = [x.astype(BF16) for x in s]
            for q in pairs:
                outs[q].append(_dot_nt(phs[q, sb]["rq"].astype(BF16), sb16[q]) + phs[q, sb]["ov"])
            s = [s[q] * trans[q, sb][0] - _dot(sb16[q], trans[q, sb][1]) + trans[q, sb][2] for q in pairs]
        for q in pairs:
            s_s[q] = s[q]

        @pl.when(pl.program_id(2) == pl.num_programs(2) - 1)
        def _():
            for q in pairs:
                sout_ref[0, q] = from_blockdiag(s[q])

    o = [jnp.concatenate(outs[q], axis=0) for q in pairs]
    inv_n = 1.0 / HEAD_DIM
    mu = [_split_dot(x, block_ones, 1) * inv_n for x in o]
    dev = [x - m_ for x, m_ in zip(o, mu)]
    var = [_split_dot(x * x, block_ones, 1) * inv_n for x in dev]
    bonus = [seg_sum(r[q] * kmod[q] * rk_ref[:, lanes[q]]) * v[q] for q in pairs]
    for q in pairs:
        on = dev[q] * lax.rsqrt(var[q] + GN_EPS) * lnw_ref[:, lanes[q]] + lnb_ref[:, lanes[q]]
        o_ref[:, lanes[q]] = ((on + bonus[q]) * g_ref[:, lanes[q]]).astype(BF16)


def _wkv_call(rkv, lw, a, g, p, s_bd, *, sample, seq_len, t_blk, chunk, n_pairs):
    rows, gdim = lw.shape
    npair = gdim // LANES
    nseq = rows // seq_len
    width = n_pairs * LANES
    pair_blocks = npair // n_pairs
    head = [p["k_k"], p["k_a"], p["r_k"], p["ln_x_w"], p["ln_x_b"]]
    if sample:
        seq_blk = t_blk // seq_len
        grid = (nseq // seq_blk, pair_blocks)
        blk = lambda off: pl.BlockSpec((t_blk, width), lambda b, q, off=off: (b, off + q))
        hspec = pl.BlockSpec((1, width), lambda b, q: (0, q))
        sspec = pl.BlockSpec((seq_blk, n_pairs, LANES, HEAD_DIM), lambda b, q: (b, q, 0, 0))
        extra_specs, extra_args = [sspec], [s_bd]
        sem = ("arbitrary", "arbitrary")
        scratch = []
    else:
        nblk = seq_len // t_blk
        grid = (nseq, pair_blocks, nblk)
        blk = lambda off: pl.BlockSpec((t_blk, width), lambda b, q, n, off=off: (b * nblk + n, off + q))
        hspec = pl.BlockSpec((1, width), lambda b, q, n: (0, q))
        sspec = pl.BlockSpec((1, n_pairs, LANES, HEAD_DIM), lambda b, q, n: (b, q, 0, 0))
        extra_specs, extra_args = [], []
        sem = ("arbitrary", "arbitrary", "arbitrary")
        scratch = [pltpu.VMEM((n_pairs, LANES, LANES), F32)]
    in_specs = [blk(0), blk(pair_blocks), blk(2 * pair_blocks), blk(0), blk(0), blk(0)] + [hspec] * 5 + extra_specs
    args = [rkv, rkv, rkv, lw, a, g] + head + extra_args
    return pl.pallas_call(
        functools.partial(_wkv_kernel, sample=sample, chunk=chunk, n_pairs=n_pairs, n_sub=t_blk // WKV_SUB),
        grid=grid, in_specs=in_specs, out_specs=[blk(0), sspec],
        out_shape=[jax.ShapeDtypeStruct((rows, gdim), BF16),
                   jax.ShapeDtypeStruct((nseq, npair, LANES, HEAD_DIM), F32)],
        scratch_shapes=scratch, compiler_params=_params(*sem), name="wkv")(*args)


def _outproj_kernel(orw_ref, ocv_ref, w_ref, x_ref, o_ref):
    lhs = jnp.concatenate([orw_ref[...], ocv_ref[...]], axis=1)
    o_ref[...] = x_ref[...] + _dot(lhs, w_ref[...])


def _outproj_call(orw, ocv, w_out, x2d, *, tm, tn):
    rows, d = x2d.shape
    g, gc = orw.shape[1], ocv.shape[1]
    tile = pl.BlockSpec((tm, tn), lambda i, j: (i, j))
    return pl.pallas_call(
        _outproj_kernel, grid=(rows // tm, d // tn),
        in_specs=[pl.BlockSpec((tm, g), lambda i, j: (i, 0)), pl.BlockSpec((tm, gc), lambda i, j: (i, 0)),
                  pl.BlockSpec((d, tn), lambda i, j: (0, j)), tile],
        out_specs=tile, out_shape=jax.ShapeDtypeStruct((rows, d), F32),
        compiler_params=_params("arbitrary", "arbitrary"), name="outproj")(orw, ocv, w_out, x2d)


def _norm_kernel(x_ref, g_ref, o_ref):
    x = x_ref[...]
    y = x * lax.rsqrt(jnp.mean(x * x, axis=-1, keepdims=True) + RMS_EPS) * g_ref[...]
    o_ref[...] = y.astype(o_ref.dtype)


def _norm_call(x2d, g, dtype, *, tm):
    rows, d = x2d.shape
    blk = pl.BlockSpec((tm, d), lambda i: (i, 0))
    return pl.pallas_call(
        _norm_kernel, grid=(rows // tm,), in_specs=[blk, pl.BlockSpec((1, d), lambda i: (0, 0))],
        out_specs=blk, out_shape=jax.ShapeDtypeStruct((rows, d), dtype),
        compiler_params=_params("arbitrary"), name="rmsnorm")(x2d, g)


def _ffnup_kernel(*refs, sample, tiles_per_seq):
    if sample:
        x_ref, w1_ref, w3_ref, cw_ref, cb_ref, st1_ref, st0_ref, h_ref, last_ref = refs
        carry = None
    else:
        x_ref, w1_ref, w3_ref, cw_ref, cb_ref, h_ref, last_ref, carry_ref = refs
        carry = (carry_ref, pl.program_id(1), pl.program_id(0), tiles_per_seq)
        _init_carry(*carry[:3])
    tn = h_ref.shape[1]
    w13 = jnp.concatenate([w1_ref[...].astype(BF16), w3_ref[...].astype(BF16)], axis=1)
    uw = _dot(x_ref[...], w13)
    u = uw[:, :tn]
    tm = u.shape[0]
    states = (st1_ref[...], st0_ref[...]) if sample else None
    prev1, prev2 = _shifted_rows(u, carry, states)
    if sample:
        last_ref[...] = u.reshape(tm // SUBLANES, SUBLANES, tn)[:, SUBLANES - 2:, :]
    else:
        last_ref[...] = u[tm - SUBLANES:, :]
    z = prev2 * cw_ref[0:1, :] + prev1 * cw_ref[1:2, :] + u * cw_ref[2:3, :] + cb_ref[...]
    hz = 0.5 * z
    h_ref[...] = ((hz + hz * jnp.tanh(hz)) * uw[:, tn:]).astype(BF16)


def _ffnup_call(hn, w1, w3, conv_w, conv_b, st1, st0, *, sample, seq_len, tm, tn):
    rows, d = hn.shape
    dff = conv_w.shape[1]
    n_i, n_j = rows // tm, dff // tn
    tile = pl.BlockSpec((tm, tn), lambda i, j: (i, j))
    wspec = pl.BlockSpec((d, tn), lambda i, j: (0, j))
    in_specs = [pl.BlockSpec((tm, d), lambda i, j: (i, 0)), wspec, wspec,
                pl.BlockSpec((3, tn), lambda i, j: (0, j)), pl.BlockSpec((1, tn), lambda i, j: (0, j))]
    args = [hn, w1, w3, conv_w, conv_b]
    if sample:
        sspec = pl.BlockSpec((tm // seq_len, tn), lambda i, j: (i, j))
        in_specs += [sspec, sspec]
        args += [st1, st0]
    if sample:
        last_spec = pl.BlockSpec((tm // seq_len, 2, tn), lambda i, j: (i, 0, j))
        last_shape = jax.ShapeDtypeStruct((rows // seq_len, 2, dff), F32)
    else:
        last_spec = pl.BlockSpec((SUBLANES, tn), lambda i, j: (i, j))
        last_shape = jax.ShapeDtypeStruct((n_i * SUBLANES, dff), F32)
    return pl.pallas_call(
        functools.partial(_ffnup_kernel, sample=sample, tiles_per_seq=max(seq_len // tm, 1)),
        grid=(n_i, n_j), in_specs=in_specs,
        out_specs=[tile, last_spec], out_shape=[jax.ShapeDtypeStruct((rows, dff), BF16), last_shape],
        scratch_shapes=[] if sample else [pltpu.VMEM((n_j, SUBLANES, tn), F32)],
        compiler_params=_params("arbitrary", "arbitrary"), name="ffnup")(*args)


def _ffndown_kernel(h_ref, w_ref, x_ref, o_ref):
    o_ref[...] = x_ref[...] + _dot(h_ref[...], w_ref[...])


def _ffndown_call(h, w2, x1, *, tm, tn):
    rows, dff = h.shape
    d = w2.shape[1]
    tile = pl.BlockSpec((tm, tn), lambda i, j: (i, j))
    return pl.pallas_call(
        _ffndown_kernel, grid=(rows // tm, d // tn),
        in_specs=[pl.BlockSpec((tm, dff), lambda i, j: (i, 0)), pl.BlockSpec((dff, tn), lambda i, j: (0, j)), tile],
        out_specs=tile, out_shape=jax.ShapeDtypeStruct((rows, d), F32),
        compiler_params=_params("arbitrary", "arbitrary"), name="ffndown")(h, w2, x1)


def _tile(n, want):
    t = min(n, want)
    while n % t or (t % SUBLANES and t != n):
        t -= 1
    return t


def _col_tile(n, want):
    t = min(n, want)
    while n % t or t % LANES:
        t -= LANES
    return t


def _layer(x, states, p, *, sample):
    nseq, seq_len, d = x.shape
    rows = nseq * seq_len
    g = p["dw2"].shape[1]
    gc = d - g
    dff = p["ffn_conv_w"].shape[1]
    npair = g // LANES
    x2d = x.reshape(rows, d)
    big = dict(sample=sample, seq_len=seq_len)
    tm_big = rows if sample else _tile(seq_len, 1024)

    if sample:
        assert seq_len == SUBLANES, "the sample path shifts rows inside 8-row groups"
        shift, wkv, conv, ffn = states
        ext_x = shift
        ext_p = shift.astype(BF16)
        ce1, ce2 = conv[:, 1], conv[:, 0]
        fe1, fe2 = ffn[:, 1], ffn[:, 0]
        s_bd = wkv.reshape(nseq, npair, LANES, HEAD_DIM)
    else:
        ext_x = ext_p = ce1 = ce2 = fe1 = fe2 = s_bd = None

    tm_mix = _tile(rows if sample else seq_len, 256)
    xnb, lw, a, gate, xlast = _mix_call(x2d, ext_x, p, sample=sample, seq_len=seq_len, tm=tm_mix)
    rkv = _rkv_call(xnb, p["w_in"], p["mu_rkv"], ext_p, tm=tm_big, tn=_col_tile(3 * g, 512), **big)
    tm_conv = _tile(rows if sample else seq_len, 1024)
    ycv, cxlast = _convbr_call(xnb, p["w_conv"], p["conv_w"], ce1, ce2, col0=0, tm=tm_conv,
                               tn=_col_tile(gc, 512), **big)
    if sample:
        t_blk, chunk = WKV_SUB, seq_len
    else:
        chunk = WKV_SUB
        t_blk = _tile(seq_len, 4 * WKV_SUB)
    orw, s_new = _wkv_call(rkv, lw, a, gate, p, s_bd, t_blk=t_blk, chunk=chunk,
                           n_pairs=math.gcd(npair, 4), **big)
    tm_e = _tile(rows, 512)
    ocv = _norm_call(ycv, p["conv_norm_g"], BF16, tm=_tile(rows, 512))
    x1 = _outproj_call(orw, ocv, p["w_out"], x2d, tm=_tile(rows, 1024), tn=_col_tile(d, 512))
    hn = _norm_call(x1, p["norm2_g"], BF16, tm=_tile(rows, 512))
    tm_ffn = rows if sample else _tile(seq_len, 2048)
    h, ulast = _ffnup_call(hn, p["ffn_w1"], p["ffn_w3"], p["ffn_conv_w"], p["ffn_conv_b"], fe1, fe2,
                           tm=tm_ffn, tn=_col_tile(dff, 256), **big)
    x2 = _ffndown_call(h, p["ffn_w2"], x1, tm=tm_e, tn=_col_tile(d, 512))

    def last_rows(arr, tile_rows, k):
        if sample:
            return arr.reshape(nseq, seq_len, -1)[:, seq_len - k:]
        per_seq = seq_len // tile_rows
        return arr.reshape(nseq, per_seq, SUBLANES, -1)[:, -1, SUBLANES - k:]
    new_shift = last_rows(xlast, tm_mix, 1)[:, 0]
    new_conv = last_rows(cxlast, tm_conv, 2)
    new_ffn = ulast if sample else last_rows(ulast, tm_ffn, 2)
    new_wkv = s_new.reshape(nseq, 2 * npair, HEAD_DIM, HEAD_DIM)
    return x2.reshape(nseq, seq_len, d), new_shift, new_wkv, new_conv, new_ffn


def _pad_to(a, axis, mult):
    pad = (-a.shape[axis]) % mult
    if not pad:
        return a
    widths = [(0, 0)] * a.ndim
    widths[axis] = (0, pad)
    return jnp.pad(a, widths)


def kernel(x_prompt, x_sample, state_shift, state_wkv, state_conv, state_ffn, norm1_g, w_in, mu_rkv, mu_lora, decay_w0, decay_w1, decay_w2, aaa_a0, aaa_a1, aaa_a2, gate_g1, gate_g2, k_k, k_a, r_k, ln_x_w, ln_x_b, conv_w, conv_norm_g, w_out, norm2_g, ffn_w1, ffn_conv_w, ffn_conv_b, ffn_w3, ffn_w2, final_norm_g):
    depth = w_in.shape[0]
    d = x_prompt.shape[-1]
    row = lambda v: v.reshape(1, -1).astype(F32)
    yp, ys = x_prompt, x_sample
    outs_p, outs_s = [], []
    for l in range(depth):
        p = dict(
            norm1_g=row(norm1_g[l]), w_in=w_in[l],
            w_conv=_cast_cols_call(w_in[l], 3 * (d // 2), w_in.shape[2] - 3 * (d // 2), _tile(d, 512)),
            mu_rkv=row(mu_rkv[l]), mu_lora=mu_lora[l],
            decay_w0=row(decay_w0[l]), aaa_a0=row(aaa_a0[l]),
            dw1=decay_w1[l].astype(BF16), dw2=decay_w2[l].astype(BF16),
            aw1=aaa_a1[l].astype(BF16), aw2=aaa_a2[l].astype(BF16),
            gw1=_pad_to(gate_g1[l], 1, LANES).astype(BF16), gw2=_pad_to(gate_g2[l], 0, LANES).astype(BF16),
            k_k=row(k_k[l]), k_a=row(k_a[l]), r_k=row(r_k[l]), ln_x_w=row(ln_x_w[l]), ln_x_b=row(ln_x_b[l]),
            conv_w=conv_w[l], conv_norm_g=row(conv_norm_g[l]), w_out=w_out[l].astype(BF16),
            norm2_g=row(norm2_g[l]), ffn_w1=ffn_w1[l], ffn_w3=ffn_w3[l],
            ffn_conv_w=ffn_conv_w[l], ffn_conv_b=row(ffn_conv_b[l]), ffn_w2=ffn_w2[l].astype(BF16))
        yp, *st_p = _layer(yp, None, p, sample=False)
        ys, *st_s = _layer(ys, (state_shift[l], state_wkv[l], state_conv[l], state_ffn[l]), p, sample=True)
        outs_p.append(st_p)
        outs_s.append(st_s)
    fin = row(final_norm_g)
    y_prompt = _norm_call(yp.reshape(-1, d), fin, F32, tm=_tile(yp.shape[0] * yp.shape[1], 512)).reshape(yp.shape)
    y_sample = _norm_call(ys.reshape(-1, d), fin, F32, tm=_tile(ys.shape[0] * ys.shape[1], 512)).reshape(ys.shape)
    stack = lambda outs, k: jnp.stack([o[k] for o in outs])
    return (y_prompt, y_sample,
            stack(outs_p, 0), stack(outs_p, 1), stack(outs_p, 2), stack(outs_p, 3),
            stack(outs_s, 0), stack(outs_s, 1), stack(outs_s, 2), stack(outs_s, 3))
```

```python
import functools
import math

import jax
import jax.numpy as jnp
from jax import lax
from jax.experimental import pallas as pl
from jax.experimental.pallas import tpu as pltpu

F32 = jnp.float32
BF16 = jnp.bfloat16

HEAD_DIM = 64
LANES = 128
SUBLANES = 8
WKV_SUB = 64
RMS_EPS = 1e-6
GN_EPS = 64e-5
VMEM_LIMIT_BYTES = 60 * 1024 * 1024


def _params(*sem, flags=None):
    return pltpu.CompilerParams(dimension_semantics=sem, vmem_limit_bytes=VMEM_LIMIT_BYTES, flags=flags)


def _dot(a, b):
    return jnp.dot(a, b, preferred_element_type=F32)


def _dot_nt(a, b):
    return lax.dot_general(a, b, (((1,), (1,)), ((), ())), preferred_element_type=F32)


def _dot_tn(a, b):
    return lax.dot_general(a, b, (((0,), (0,)), ((), ())), preferred_element_type=F32)


def _sigmoid(z):
    return 1.0 / (1.0 + jnp.exp(-z))


def _split_dot(x, b_exact, terms, dot=_dot):
    acc = None
    rem = x
    for _ in range(terms):
        hi = rem.astype(BF16)
        part = dot(hi, b_exact)
        acc = part if acc is None else acc + part
        rem = rem - hi.astype(F32)
    return acc


def _row_iota(shape):
    return lax.broadcasted_iota(jnp.int32, shape, 0)


def _init_carry(carry_ref, j, i):
    @pl.when(i == 0)
    def _():
        carry_ref[j] = jnp.zeros(carry_ref.shape[1:], F32)


def _shifted_rows(u, carry, states):
    tm = u.shape[0]
    row = _row_iota(u.shape)
    r1 = pltpu.roll(u, 1, 0)
    r2 = pltpu.roll(u, 2, 0)
    if carry is not None:
        carry_ref, j, i, tiles_per_seq = carry

        c = carry_ref[j]
        keep = (i % tiles_per_seq) != 0
        first1 = jnp.where(keep, c[SUBLANES - 1:SUBLANES, :], 0.0)
        first2 = jnp.where(keep, c[SUBLANES - 2:SUBLANES - 1, :], 0.0)
        carry_ref[j] = u[tm - SUBLANES:, :]
        return jnp.where(row == 0, first1, r1), jnp.where(row == 0, first2, jnp.where(row == 1, first1, r2))
    first1, first2 = _expand_state_rows(states[0], states[1], tm)
    t = row % SUBLANES
    return jnp.where(t == 0, first1, r1), (None if first2 is None else jnp.where(t < 2, first2, r2))


def _expand_state_rows(st1, st0, tm):
    ns, n = st1.shape
    spread = lambda x: jnp.broadcast_to(x[:, None, :], (ns, SUBLANES, n)).reshape(tm, n)
    first1 = spread(st1)
    if st0 is None:
        return first1, None
    t = _row_iota((tm, n)) % SUBLANES
    return first1, jnp.where(t == 0, spread(st0), first1)


def _mix_kernel(*refs, sample, tiles_per_seq):
    if sample:
        (x_ref, ext_ref, g1_ref, mu_ref, w0_ref, a0_ref, dw1_ref, aw1_ref, gw1_ref, dw2_ref,
         aw2_ref, gw2_ref, xnb_ref, lw_ref, a_ref, g_ref, last_ref) = refs
    else:
        (x_ref, g1_ref, mu_ref, w0_ref, a0_ref, dw1_ref, aw1_ref, gw1_ref, dw2_ref,
         aw2_ref, gw2_ref, xnb_ref, lw_ref, a_ref, g_ref, last_ref, carry_ref) = refs
    if not sample:
        _init_carry(carry_ref, 0, pl.program_id(0))
    x = x_ref[...]
    xn = x * lax.rsqrt(jnp.mean(x * x, axis=-1, keepdims=True) + RMS_EPS) * g1_ref[...]
    tm = xn.shape[0]
    if sample:
        prev, _unused = _shifted_rows(xn, None, (ext_ref[...], None))
        last_ref[...] = xn
    else:
        prev, _unused = _shifted_rows(xn, (carry_ref, 0, pl.program_id(0), tiles_per_seq), None)
        last_ref[...] = xn[tm - SUBLANES:, :]
    dx = prev - xn
    xnb_ref[...] = xn.astype(BF16)
    xw = (xn + dx * mu_ref[0:1, :]).astype(BF16)
    hw = jnp.tanh(_dot(xw, dw1_ref[...]))
    wl = w0_ref[...] + _dot(hw.astype(BF16), dw2_ref[...])
    lw_ref[...] = -_sigmoid(wl) * math.exp(-0.5)
    xa = (xn + dx * mu_ref[1:2, :]).astype(BF16)
    ha = _dot(xa, aw1_ref[...])
    a_ref[...] = _sigmoid(a0_ref[...] + _dot(ha.astype(BF16), aw2_ref[...]))
    xg = (xn + dx * mu_ref[2:3, :]).astype(BF16)
    hg = _sigmoid(_dot(xg, gw1_ref[...]))
    g_ref[...] = _dot(hg.astype(BF16), gw2_ref[...])


def _mix_call(x2d, ext, p, *, sample, seq_len, tm):
    rows, d = x2d.shape
    g = p["dw2"].shape[1]
    n_i = rows // tm
    row_blk = lambda w: pl.BlockSpec((tm, w), lambda i: (i, 0))
    full = lambda a: pl.BlockSpec(a.shape, lambda i: (0, 0))
    weights = [p["norm1_g"], p["mu_lora"], p["decay_w0"], p["aaa_a0"], p["dw1"], p["aw1"], p["gw1"],
               p["dw2"], p["aw2"], p["gw2"]]
    state_blk = pl.BlockSpec((tm // seq_len, d), lambda i: (i, 0))
    in_specs = [row_blk(d)] + ([state_blk] if sample else []) + [full(w) for w in weights]
    args = [x2d] + ([ext] if sample else []) + weights
    last_rows = tm if sample else SUBLANES
    out_shape = [jax.ShapeDtypeStruct((rows, d), BF16)] + [jax.ShapeDtypeStruct((rows, g), F32)] * 3 + [
        jax.ShapeDtypeStruct((n_i * last_rows, d), F32)]
    out_specs = [row_blk(d), row_blk(g), row_blk(g), row_blk(g), pl.BlockSpec((last_rows, d), lambda i: (i, 0))]
    return pl.pallas_call(
        functools.partial(_mix_kernel, sample=sample, tiles_per_seq=max(seq_len // tm, 1)),
        grid=(n_i,), in_specs=in_specs, out_specs=out_specs, out_shape=out_shape,
        scratch_shapes=[] if sample else [pltpu.VMEM((1, SUBLANES, d), F32)],
        compiler_params=_params("arbitrary"), name="mix")(*args)


def _rkv_kernel(*refs, sample, tiles_per_seq):
    if sample:
        x_ref, w_ref, mu_ref, shift_ref, o_ref = refs
    else:
        x_ref, w_ref, mu_ref, o_ref, carry_ref = refs
    if not sample:
        _init_carry(carry_ref, pl.program_id(1), pl.program_id(0))
    w = w_ref[...].astype(BF16)
    p = _dot(x_ref[...], w)
    if sample:
        prev, _unused = _shifted_rows(p, None, (_dot(shift_ref[...], w), None))
    else:
        prev, _unused = _shifted_rows(p, (carry_ref, pl.program_id(1), pl.program_id(0), tiles_per_seq), None)
    o_ref[...] = p + mu_ref[...] * (prev - p)


def _rkv_call(xnb, w_in, mu, ext, *, sample, seq_len, tm, tn):
    rows, d = xnb.shape
    n = mu.shape[1]
    n_i, n_j = rows // tm, n // tn
    in_specs = [pl.BlockSpec((tm, d), lambda i, j: (i, 0)), pl.BlockSpec((d, tn), lambda i, j: (0, j)),
                pl.BlockSpec((1, tn), lambda i, j: (0, j))]
    args = [xnb, w_in, mu]
    if sample:
        in_specs.append(pl.BlockSpec((tm // seq_len, d), lambda i, j: (i, 0)))
        args.append(ext)
    return pl.pallas_call(
        functools.partial(_rkv_kernel, sample=sample, tiles_per_seq=max(seq_len // tm, 1)),
        grid=(n_i, n_j), in_specs=in_specs, out_specs=pl.BlockSpec((tm, tn), lambda i, j: (i, j)),
        out_shape=jax.ShapeDtypeStruct((rows, n), F32),
        scratch_shapes=[] if sample else [pltpu.VMEM((n_j, SUBLANES, tn), F32)],
        compiler_params=_params("arbitrary", "arbitrary"), name="rkv")(*args)


def _cast_kernel(w_ref, o_ref):
    o_ref[...] = w_ref[...].astype(BF16)


def _cast_cols_call(w, col0, n, tr):
    d = w.shape[0]
    assert col0 % n == 0
    return pl.pallas_call(
        _cast_kernel, grid=(d // tr,),
        in_specs=[pl.BlockSpec((tr, n), lambda i: (i, col0 // n))],
        out_specs=pl.BlockSpec((tr, n), lambda i: (i, 0)),
        out_shape=jax.ShapeDtypeStruct((d, n), BF16),
        compiler_params=_params("arbitrary"), name="cast_cols")(w)


def _convbr_kernel(*refs, sample, tiles_per_seq):
    if sample:
        x_ref, wb_ref, wc_ref, wx_ref, cw_ref, st1_ref, st0_ref, y_ref, last_ref = refs
    else:
        x_ref, wb_ref, wc_ref, wx_ref, cw_ref, y_ref, last_ref, carry_ref = refs
    if not sample:
        _init_carry(carry_ref, pl.program_id(1), pl.program_id(0))
    x = x_ref[...]
    cx = _dot(x, wc_ref[...]) * _dot(x, wx_ref[...])
    tm = cx.shape[0]
    if sample:
        prev1, prev2 = _shifted_rows(cx, None, (st1_ref[...], st0_ref[...]))
        last_ref[...] = cx
    else:
        prev1, prev2 = _shifted_rows(cx, (carry_ref, pl.program_id(1), pl.program_id(0), tiles_per_seq), None)
        last_ref[...] = cx[tm - SUBLANES:, :]
    hconv = prev2 * cw_ref[0:1, :] + prev1 * cw_ref[1:2, :] + cx * cw_ref[2:3, :]
    y_ref[...] = _dot(x, wb_ref[...]) * hconv


def _convbr_call(xnb, w_in, conv_w, e1, e2, *, col0, sample, seq_len, tm, tn):
    rows, d = xnb.shape
    gc = conv_w.shape[1]
    n_i, n_j = rows // tm, gc // tn
    off = col0 // tn
    nb = gc // tn
    wspec = lambda k: pl.BlockSpec((d, tn), lambda i, j: (0, off + k * nb + j))
    tile = pl.BlockSpec((tm, tn), lambda i, j: (i, j))
    in_specs = [pl.BlockSpec((tm, d), lambda i, j: (i, 0)), wspec(0), wspec(1), wspec(2),
                pl.BlockSpec((3, tn), lambda i, j: (0, j))]
    args = [xnb, w_in, w_in, w_in, conv_w]
    if sample:
        sspec = pl.BlockSpec((tm // seq_len, tn), lambda i, j: (i, j))
        in_specs += [sspec, sspec]
        args += [e1, e2]
    last_rows = tm if sample else SUBLANES
    return pl.pallas_call(
        functools.partial(_convbr_kernel, sample=sample, tiles_per_seq=max(seq_len // tm, 1)),
        grid=(n_i, n_j), in_specs=in_specs,
        out_specs=[tile, pl.BlockSpec((last_rows, tn), lambda i, j: (i, j))],
        out_shape=[jax.ShapeDtypeStruct((rows, gc), F32), jax.ShapeDtypeStruct((n_i * last_rows, gc), F32)],
        scratch_shapes=[] if sample else [pltpu.VMEM((n_j, SUBLANES, tn), F32)],
        compiler_params=_params("arbitrary", "arbitrary"), name="convbr")(*args)


def _wkv_masks(c):
    m = 2 * WKV_SUB
    r2 = jnp.bitwise_and(_row_iota((m, m)), WKV_SUB - 1)
    c2 = jnp.bitwise_and(lax.broadcasted_iota(jnp.int32, (m, m), 1), WKV_SUB - 1)
    shift = int(math.log2(c))
    same = jnp.right_shift(r2, shift) == jnp.right_shift(c2, shift)
    eye = (_row_iota((m, m)) == lax.broadcasted_iota(jnp.int32, (m, m), 1)).astype(F32)
    lane_lo = lax.broadcasted_iota(jnp.int32, (WKV_SUB, LANES), 1) < HEAD_DIM
    return lane_lo, same & (c2 < r2), same & (c2 <= r2), eye


def _stack_heads(x, lane_lo):
    z = jnp.zeros_like(x)
    return jnp.concatenate([jnp.where(lane_lo, x, z), jnp.where(lane_lo, z, x)], axis=0)


def _fold_heads(x):
    return x[:WKV_SUB] + x[WKV_SUB:]


def _wkv_phase1(units, masks, c):
    lane_lo, strict, incl, eye = masks
    m = 2 * WKV_SUB
    phs = []
    for r, kt, v, kp, bt, lg, lw in units:
        e1 = jnp.exp(lg)
        e0 = jnp.exp(lg - lw)
        ei = jnp.exp(-lg)
        phs.append(dict(
            e1=e1, rt_st=_stack_heads(r * e1, lane_lo), kp_b=_stack_heads(kp * e0, lane_lo).astype(BF16),
            kh_st=_stack_heads(kt * ei, lane_lo), bh_st=_stack_heads(bt * ei, lane_lo),
            v_st=_stack_heads(v, lane_lo)))
    for ph in phs:
        ph["kh_b"], ph["bh_b"], ph["v_b"] = (ph[n].astype(BF16) for n in ("kh_st", "bh_st", "v_st"))
    gs = [_dot_nt(jnp.concatenate([ph["kp_b"], ph["rt_st"].astype(BF16)], axis=0),
                  jnp.concatenate([ph["bh_b"], ph["kh_b"]], axis=0)) for ph in phs]
    ps = [-jnp.where(strict, g[:m, :m], 0.0) for g in gs]
    ts = [eye + p for p in ps]
    n = 1
    while 2 * n < c:
        pbs = [p.astype(BF16) for p in ps]
        ps = [_dot(pb, pb) for pb in pbs]
        ts = [t + _dot(t.astype(BF16), p.astype(BF16)) for t, p in zip(ts, ps)]
        n *= 2
    abvs = [_dot(jnp.concatenate([jnp.where(strict, g[:m, m:], 0.0), jnp.where(incl, g[m:, m:], 0.0)],
                                 axis=0).astype(BF16), ph["v_b"]) for g, ph in zip(gs, phs)]
    tts = [_dot(t.astype(BF16), jnp.concatenate([ph["kp_b"], abv[:m].astype(BF16)], axis=1))
           for t, ph, abv in zip(ts, phs, abvs)]
    bbtts = [_dot(jnp.where(incl, g[m:, :m], 0.0).astype(BF16), tt.astype(BF16)) for g, tt in zip(gs, tts)]
    for ph, abv, tt, bbtt in zip(phs, abvs, tts, bbtts):
        ph["rq"] = _fold_heads(ph["rt_st"] - bbtt[:, :LANES])
        ph["ov"] = _fold_heads(abv[m:] - bbtt[:, LANES:])
        ph["tk_st"], ph["tav_st"] = tt[:, :LANES], tt[:, LANES:]
    return phs


def _wkv_transitions(phs):
    gams = [ph["e1"][WKV_SUB - 1:WKV_SUB, :] for ph in phs]
    kgs = [(_dot_tn(ph["tk_st"].astype(BF16), ph["bh_b"]) * gam).astype(BF16) for ph, gam in zip(phs, gams)]
    bcgs = [_dot_tn(jnp.concatenate([ph["v_b"], (-ph["tav_st"]).astype(BF16)], axis=0),
                    jnp.concatenate([ph["kh_b"], ph["bh_b"]], axis=0)) * gam for ph, gam in zip(phs, gams)]
    return gams, kgs, bcgs


def _wkv_units_small(states, units, c, lane_lo_c):
    rows = lambda u: slice(u * c, (u + 1) * c)
    rows_hi = lambda u: slice(WKV_SUB + u * c, WKV_SUB + (u + 1) * c)
    ous = [_dot_nt(jnp.concatenate([ph["rq"][rows(u)], ph["tkm"][rows(u)]], axis=0).astype(BF16), s.astype(BF16))
           for s, (ph, u) in zip(states, units)]
    outs, new_states = [], []
    for s, (ph, u), ou in zip(states, units, ous):
        pick = lambda x: jnp.concatenate([x[rows(u)], x[rows_hi(u)]], axis=0)
        uu = ou[c:] + ph["tav"][rows(u)]
        z = jnp.zeros_like(uu)
        u_st = jnp.concatenate([jnp.where(lane_lo_c, uu, z), jnp.where(lane_lo_c, z, uu)], axis=0)
        gam = ph["e1"][(u + 1) * c - 1:(u + 1) * c, :]
        lhs = jnp.concatenate([pick(ph["v_st"]), -u_st], axis=0).astype(BF16)
        rhs = (jnp.concatenate([pick(ph["kh_st"]), pick(ph["bh_st"])], axis=0) * gam).astype(BF16)
        outs.append(ou[:c] + ph["ov"][rows(u)])
        new_states.append(s * gam + _dot_tn(lhs, rhs))
    return outs, new_states


def _wkv_kernel(*refs, sample, chunk, n_pairs, n_sub):
    if sample:
        (r_ref, k_ref, v_ref, lw_ref, a_ref, g_ref, kk_ref, ka_ref, rk_ref, lnw_ref, lnb_ref, sin_ref,
         o_ref, sout_ref) = refs
    else:
        (r_ref, k_ref, v_ref, lw_ref, a_ref, g_ref, kk_ref, ka_ref, rk_ref, lnw_ref, lnb_ref,
         o_ref, sout_ref, s_s) = refs
    c = chunk
    lane128 = lax.broadcasted_iota(jnp.int32, (LANES, LANES), 1)
    row128 = _row_iota((LANES, LANES))
    blockdiag = (row128 < HEAD_DIM) == (lane128 < HEAD_DIM)
    block_ones = blockdiag.astype(BF16)
    seg_sum = lambda z: _split_dot(z, block_ones, 2)
    to_blockdiag = lambda x: jnp.where(blockdiag, jnp.concatenate([x, x], axis=1), 0.0)
    from_blockdiag = lambda s_: s_[:, :HEAD_DIM] + s_[:, HEAD_DIM:]
    masks = _wkv_masks(c)
    r2 = jnp.bitwise_and(_row_iota((WKV_SUB, WKV_SUB)), WKV_SUB - 1)
    c2 = lax.broadcasted_iota(jnp.int32, (WKV_SUB, WKV_SUB), 1)
    shift = int(math.log2(c))
    tril_b = ((jnp.right_shift(r2, shift) == jnp.right_shift(c2, shift)) & (c2 <= r2)).astype(BF16)
    lane_lo_c = lax.broadcasted_iota(jnp.int32, (c, LANES), 1) < HEAD_DIM

    if not sample:
        @pl.when(pl.program_id(2) == 0)
        def _():
            s_s[...] = jnp.zeros_like(s_s)

    lgs = []
    for sb in range(n_sub):
        rows = slice(sb * WKV_SUB, (sb + 1) * WKV_SUB)
        lgs.append(_split_dot(lw_ref[rows, :], tril_b, 3, dot=lambda x, b: _dot(b, x)))

    pairs = range(n_pairs)
    lanes = [slice(q * LANES, (q + 1) * LANES) for q in pairs]
    r = [r_ref[:, l] for l in lanes]
    k = [k_ref[:, l] for l in lanes]
    v = [v_ref[:, l] for l in lanes]
    a = [a_ref[:, l] for l in lanes]
    lw = [lw_ref[:, l] for l in lanes]
    kk = [k[q] * kk_ref[:, lanes[q]] for q in pairs]
    norms = [seg_sum(x * x) for x in kk]
    kk = [x / jnp.maximum(jnp.sqrt(n2), 1e-12) for x, n2 in zip(kk, norms)]
    kmod = [k[q] * (1.0 + (a[q] - 1.0) * ka_ref[:, lanes[q]]) for q in pairs]
    bt = [kk[q] * a[q] for q in pairs]

    subs = [(q, sb) for sb in range(n_sub) for q in pairs]
    units = []
    for q, sb in subs:
        rows = slice(sb * WKV_SUB, (sb + 1) * WKV_SUB)
        units.append((r[q][rows], kmod[q][rows], v[q][rows], kk[q][rows], bt[q][rows],
                      lgs[sb][:, lanes[q]], lw[q][rows]))
    phs = dict(zip(subs, _wkv_phase1(units, masks, c)))

    outs = {q: [] for q in pairs}
    if sample:
        per_sub = WKV_SUB // c
        todo = []
        for (q, sb), ph in phs.items():
            ph["tkm"] = _fold_heads(ph["tk_st"])
            ph["tav"] = _fold_heads(ph["tav_st"])
            todo += [(q, sb * per_sub + u, ph, u) for u in range(per_sub)]
        o_units, new_states = _wkv_units_small([to_blockdiag(sin_ref[seq, q]) for q, seq, _, _ in todo],
                                               [(ph, u) for _, _, ph, u in todo], c, lane_lo_c)
        for (q, seq, _, _), o, s_new in zip(todo, o_units, new_states):
            sout_ref[seq, q] = from_blockdiag(s_new)
            outs[q].append(o)
    else:
        gams, kgs, bcgs = _wkv_transitions([phs[key] for key in subs])
        trans = dict(zip(subs, zip(gams, kgs, bcgs)))
        s = [s_s[q] for q in pairs]
        for sb in range(n_sub):
            sb16 = [x.astype(BF16) for x in s]
            for q in pairs:
                outs[q].append(_dot_nt(phs[q, sb]["rq"].astype(BF16), sb16[q]) + phs[q, sb]["ov"])
            s = [s[q] * trans[q, sb][0] - _dot(sb16[q], trans[q, sb][1]) + trans[q, sb][2] for q in pairs]
        for q in pairs:
            s_s[q] = s[q]

        @pl.when(pl.program_id(2) == pl.num_programs(2) - 1)
        def _():
            for q in pairs:
                sout_ref[0, q] = from_blockdiag(s[q])

    o = [jnp.concatenate(outs[q], axis=0) for q in pairs]
    inv_n = 1.0 / HEAD_DIM
    mu = [_split_dot(x, block_ones, 1) * inv_n for x in o]
    dev = [x - m_ for x, m_ in zip(o, mu)]
    var = [_split_dot(x * x, block_ones, 1) * inv_n for x in dev]
    bonus = [seg_sum(r[q] * kmod[q] * rk_ref[:, lanes[q]]) * v[q] for q in pairs]
    for q in pairs:
        on = dev[q] * lax.rsqrt(var[q] + GN_EPS) * lnw_ref[:, lanes[q]] + lnb_ref[:, lanes[q]]
        o_ref[:, lanes[q]] = ((on + bonus[q]) * g_ref[:, lanes[q]]).astype(BF16)


def _wkv_call(rkv, lw, a, g, p, s_bd, *, sample, seq_len, t_blk, chunk, n_pairs):
    rows, gdim = lw.shape
    npair = gdim // LANES
    nseq = rows // seq_len
    width = n_pairs * LANES
    pair_blocks = npair // n_pairs
    head = [p["k_k"], p["k_a"], p["r_k"], p["ln_x_w"], p["ln_x_b"]]
    if sample:
        seq_blk = t_blk // seq_len
        grid = (nseq // seq_blk, pair_blocks)
        blk = lambda off: pl.BlockSpec((t_blk, width), lambda b, q, off=off: (b, off + q))
        hspec = pl.BlockSpec((1, width), lambda b, q: (0, q))
        sspec = pl.BlockSpec((seq_blk, n_pairs, LANES, HEAD_DIM), lambda b, q: (b, q, 0, 0))
        extra_specs, extra_args = [sspec], [s_bd]
        sem = ("arbitrary", "arbitrary")
        scratch = []
    else:
        nblk = seq_len // t_blk
        grid = (nseq, pair_blocks, nblk)
        blk = lambda off: pl.BlockSpec((t_blk, width), lambda b, q, n, off=off: (b * nblk + n, off + q))
        hspec = pl.BlockSpec((1, width), lambda b, q, n: (0, q))
        sspec = pl.BlockSpec((1, n_pairs, LANES, HEAD_DIM), lambda b, q, n: (b, q, 0, 0))
        extra_specs, extra_args = [], []
        sem = ("arbitrary", "arbitrary", "arbitrary")
        scratch = [pltpu.VMEM((n_pairs, LANES, LANES), F32)]
    in_specs = [blk(0), blk(pair_blocks), blk(2 * pair_blocks), blk(0), blk(0), blk(0)] + [hspec] * 5 + extra_specs
    args = [rkv, rkv, rkv, lw, a, g] + head + extra_args
    return pl.pallas_call(
        functools.partial(_wkv_kernel, sample=sample, chunk=chunk, n_pairs=n_pairs, n_sub=t_blk // WKV_SUB),
        grid=grid, in_specs=in_specs, out_specs=[blk(0), sspec],
        out_shape=[jax.ShapeDtypeStruct((rows, gdim), BF16),
                   jax.ShapeDtypeStruct((nseq, npair, LANES, HEAD_DIM), F32)],
        scratch_shapes=scratch, compiler_params=_params(*sem), name="wkv")(*args)


def _outproj_kernel(orw_ref, ocv_ref, w_ref, x_ref, o_ref):
    lhs = jnp.concatenate([orw_ref[...], ocv_ref[...]], axis=1)
    o_ref[...] = x_ref[...] + _dot(lhs, w_ref[...])


def _outproj_call(orw, ocv, w_out, x2d, *, tm, tn):
    rows, d = x2d.shape
    g, gc = orw.shape[1], ocv.shape[1]
    tile = pl.BlockSpec((tm, tn), lambda i, j: (i, j))
    return pl.pallas_call(
        _outproj_kernel, grid=(rows // tm, d // tn),
        in_specs=[pl.BlockSpec((tm, g), lambda i, j: (i, 0)), pl.BlockSpec((tm, gc), lambda i, j: (i, 0)),
                  pl.BlockSpec((d, tn), lambda i, j: (0, j)), tile],
        out_specs=tile, out_shape=jax.ShapeDtypeStruct((rows, d), F32),
        compiler_params=_params("arbitrary", "arbitrary"), name="outproj")(orw, ocv, w_out, x2d)


def _norm_kernel(x_ref, g_ref, o_ref):
    x = x_ref[...]
    y = x * lax.rsqrt(jnp.mean(x * x, axis=-1, keepdims=True) + RMS_EPS) * g_ref[...]
    o_ref[...] = y.astype(o_ref.dtype)


def _norm_call(x2d, g, dtype, *, tm):
    rows, d = x2d.shape
    blk = pl.BlockSpec((tm, d), lambda i: (i, 0))
    return pl.pallas_call(
        _norm_kernel, grid=(rows // tm,), in_specs=[blk, pl.BlockSpec((1, d), lambda i: (0, 0))],
        out_specs=blk, out_shape=jax.ShapeDtypeStruct((rows, d), dtype),
        compiler_params=_params("arbitrary"), name="rmsnorm")(x2d, g)


def _ffnup_kernel(*refs, sample, tiles_per_seq):
    if sample:
        x_ref, w1_ref, w3_ref, cw_ref, cb_ref, st1_ref, st0_ref, h_ref, last_ref = refs
        carry = None
    else:
        x_ref, w1_ref, w3_ref, cw_ref, cb_ref, h_ref, last_ref, carry_ref = refs
        carry = (carry_ref, pl.program_id(1), pl.program_id(0), tiles_per_seq)
        _init_carry(*carry[:3])
    tn = h_ref.shape[1]
    w13 = jnp.concatenate([w1_ref[...].astype(BF16), w3_ref[...].astype(BF16)], axis=1)
    uw = _dot(x_ref[...], w13)
    u = uw[:, :tn]
    tm = u.shape[0]
    states = (st1_ref[...], st0_ref[...]) if sample else None
    prev1, prev2 = _shifted_rows(u, carry, states)
    if sample:
        last_ref[...] = u.reshape(tm // SUBLANES, SUBLANES, tn)[:, SUBLANES - 2:, :]
    else:
        last_ref[...] = u[tm - SUBLANES:, :]
    z = prev2 * cw_ref[0:1, :] + prev1 * cw_ref[1:2, :] + u * cw_ref[2:3, :] + cb_ref[...]
    hz = 0.5 * z
    h_ref[...] = ((hz + hz * jnp.tanh(hz)) * uw[:, tn:]).astype(BF16)


def _ffnup_call(hn, w1, w3, conv_w, conv_b, st1, st0, *, sample, seq_len, tm, tn):
    rows, d = hn.shape
    dff = conv_w.shape[1]
    n_i, n_j = rows // tm, dff // tn
    tile = pl.BlockSpec((tm, tn), lambda i, j: (i, j))
    wspec = pl.BlockSpec((d, tn), lambda i, j: (0, j))
    in_specs = [pl.BlockSpec((tm, d), lambda i, j: (i, 0)), wspec, wspec,
                pl.BlockSpec((3, tn), lambda i, j: (0, j)), pl.BlockSpec((1, tn), lambda i, j: (0, j))]
    args = [hn, w1, w3, conv_w, conv_b]
    if sample:
        sspec = pl.BlockSpec((tm // seq_len, tn), lambda i, j: (i, j))
        in_specs += [sspec, sspec]
        args += [st1, st0]
    if sample:
        last_spec = pl.BlockSpec((tm // seq_len, 2, tn), lambda i, j: (i, 0, j))
        last_shape = jax.ShapeDtypeStruct((rows // seq_len, 2, dff), F32)
    else:
        last_spec = pl.BlockSpec((SUBLANES, tn), lambda i, j: (i, j))
        last_shape = jax.ShapeDtypeStruct((n_i * SUBLANES, dff), F32)
    return pl.pallas_call(
        functools.partial(_ffnup_kernel, sample=sample, tiles_per_seq=max(seq_len // tm, 1)),
        grid=(n_i, n_j), in_specs=in_specs,
        out_specs=[tile, last_spec], out_shape=[jax.ShapeDtypeStruct((rows, dff), BF16), last_shape],
        scratch_shapes=[] if sample else [pltpu.VMEM((n_j, SUBLANES, tn), F32)],
        compiler_params=_params("arbitrary", "arbitrary"), name="ffnup")(*args)


def _ffndown_kernel(h_ref, w_ref, x_ref, o_ref):
    o_ref[...] = x_ref[...] + _dot(h_ref[...], w_ref[...])


def _ffndown_call(h, w2, x1, *, tm, tn):
    rows, dff = h.shape
    d = w2.shape[1]
    tile = pl.BlockSpec((tm, tn), lambda i, j: (i, j))
    return pl.pallas_call(
        _ffndown_kernel, grid=(rows // tm, d // tn),
        in_specs=[pl.BlockSpec((tm, dff), lambda i, j: (i, 0)), pl.BlockSpec((dff, tn), lambda i, j: (0, j)), tile],
        out_specs=tile, out_shape=jax.ShapeDtypeStruct((rows, d), F32),
        compiler_params=_params("arbitrary", "arbitrary"), name="ffndown")(h, w2, x1)


def _tile(n, want):
    t = min(n, want)
    while n % t or (t % SUBLANES and t != n):
        t -= 1
    return t


def _col_tile(n, want):
    t = min(n, want)
    while n % t or t % LANES:
        t -= LANES
    return t


def _layer(x, states, p, *, sample):
    nseq, seq_len, d = x.shape
    rows = nseq * seq_len
    g = p["dw2"].shape[1]
    gc = d - g
    dff = p["ffn_conv_w"].shape[1]
    npair = g // LANES
    x2d = x.reshape(rows, d)
    big = dict(sample=sample, seq_len=seq_len)
    tm_big = rows if sample else _tile(seq_len, 1024)

    if sample:
        assert seq_len == SUBLANES, "the sample path shifts rows inside 8-row groups"
        shift, wkv, conv, ffn = states
        ext_x = shift
        ext_p = shift.astype(BF16)
        ce1, ce2 = conv[:, 1], conv[:, 0]
        fe1, fe2 = ffn[:, 1], ffn[:, 0]
        s_bd = wkv.reshape(nseq, npair, LANES, HEAD_DIM)
    else:
        ext_x = ext_p = ce1 = ce2 = fe1 = fe2 = s_bd = None

    tm_mix = _tile(rows if sample else seq_len, 256)
    xnb, lw, a, gate, xlast = _mix_call(x2d, ext_x, p, sample=sample, seq_len=seq_len, tm=tm_mix)
    rkv = _rkv_call(xnb, p["w_in"], p["mu_rkv"], ext_p, tm=tm_big, tn=_col_tile(3 * g, 512), **big)
    tm_conv = _tile(rows if sample else seq_len, 1024)
    ycv, cxlast = _convbr_call(xnb, p["w_conv"], p["conv_w"], ce1, ce2, col0=0, tm=tm_conv,
                               tn=_col_tile(gc, 512), **big)
    if sample:
        t_blk, chunk = WKV_SUB, seq_len
    else:
        chunk = WKV_SUB
        t_blk = _tile(seq_len, 4 * WKV_SUB)
    orw, s_new = _wkv_call(rkv, lw, a, gate, p, s_bd, t_blk=t_blk, chunk=chunk,
                           n_pairs=math.gcd(npair, 8), **big)
    tm_e = _tile(rows, 512)
    ocv = _norm_call(ycv, p["conv_norm_g"], BF16, tm=_tile(rows, 512))
    x1 = _outproj_call(orw, ocv, p["w_out"], x2d, tm=_tile(rows, 1024), tn=_col_tile(d, 512))
    hn = _norm_call(x1, p["norm2_g"], BF16, tm=_tile(rows, 512))
    tm_ffn = rows if sample else _tile(seq_len, 2048)
    h, ulast = _ffnup_call(hn, p["ffn_w1"], p["ffn_w3"], p["ffn_conv_w"], p["ffn_conv_b"], fe1, fe2,
                           tm=tm_ffn, tn=_col_tile(dff, 256), **big)
    x2 = _ffndown_call(h, p["ffn_w2"], x1, tm=tm_e, tn=_col_tile(d, 512))

    def last_rows(arr, tile_rows, k):
        if sample:
            return arr.reshape(nseq, seq_len, -1)[:, seq_len - k:]
        per_seq = seq_len // tile_rows
        return arr.reshape(nseq, per_seq, SUBLANES, -1)[:, -1, SUBLANES - k:]
    new_shift = last_rows(xlast, tm_mix, 1)[:, 0]
    new_conv = last_rows(cxlast, tm_conv, 2)
    new_ffn = ulast if sample else last_rows(ulast, tm_ffn, 2)
    new_wkv = s_new.reshape(nseq, 2 * npair, HEAD_DIM, HEAD_DIM)
    return x2.reshape(nseq, seq_len, d), new_shift, new_wkv, new_conv, new_ffn


def _pad_to(a, axis, mult):
    pad = (-a.shape[axis]) % mult
    if not pad:
        return a
    widths = [(0, 0)] * a.ndim
    widths[axis] = (0, pad)
    return jnp.pad(a, widths)


def kernel(x_prompt, x_sample, state_shift, state_wkv, state_conv, state_ffn, norm1_g, w_in, mu_rkv, mu_lora, decay_w0, decay_w1, decay_w2, aaa_a0, aaa_a1, aaa_a2, gate_g1, gate_g2, k_k, k_a, r_k, ln_x_w, ln_x_b, conv_w, conv_norm_g, w_out, norm2_g, ffn_w1, ffn_conv_w, ffn_conv_b, ffn_w3, ffn_w2, final_norm_g):
    depth = w_in.shape[0]
    d = x_prompt.shape[-1]
    row = lambda v: v.reshape(1, -1).astype(F32)
    yp, ys = x_prompt, x_sample
    outs_p, outs_s = [], []
    for l in range(depth):
        p = dict(
            norm1_g=row(norm1_g[l]), w_in=w_in[l],
            w_conv=_cast_cols_call(w_in[l], 3 * (d // 2), w_in.shape[2] - 3 * (d // 2), _tile(d, 512)),
            mu_rkv=row(mu_rkv[l]), mu_lora=mu_lora[l],
            decay_w0=row(decay_w0[l]), aaa_a0=row(aaa_a0[l]),
            dw1=decay_w1[l].astype(BF16), dw2=decay_w2[l].astype(BF16),
            aw1=aaa_a1[l].astype(BF16), aw2=aaa_a2[l].astype(BF16),
            gw1=_pad_to(gate_g1[l], 1, LANES).astype(BF16), gw2=_pad_to(gate_g2[l], 0, LANES).astype(BF16),
            k_k=row(k_k[l]), k_a=row(k_a[l]), r_k=row(r_k[l]), ln_x_w=row(ln_x_w[l]), ln_x_b=row(ln_x_b[l]),
            conv_w=conv_w[l], conv_norm_g=row(conv_norm_g[l]), w_out=w_out[l].astype(BF16),
            norm2_g=row(norm2_g[l]), ffn_w1=ffn_w1[l], ffn_w3=ffn_w3[l],
            ffn_conv_w=ffn_conv_w[l], ffn_conv_b=row(ffn_conv_b[l]), ffn_w2=ffn_w2[l].astype(BF16))
        yp, *st_p = _layer(yp, None, p, sample=False)
        ys, *st_s = _layer(ys, (state_shift[l], state_wkv[l], state_conv[l], state_ffn[l]), p, sample=True)
        outs_p.append(st_p)
        outs_s.append(st_s)
    fin = row(final_norm_g)
    y_prompt = _norm_call(yp.reshape(-1, d), fin, F32, tm=_tile(yp.shape[0] * yp.shape[1], 512)).reshape(yp.shape)
    y_sample = _norm_call(ys.reshape(-1, d), fin, F32, tm=_tile(ys.shape[0] * ys.shape[1], 512)).reshape(ys.shape)
    stack = lambda outs, k: jnp.stack([o[k] for o in outs])
    return (y_prompt, y_sample,
            stack(outs_p, 0), stack(outs_p, 1), stack(outs_p, 2), stack(outs_p, 3),
            stack(outs_s, 0), stack(outs_s, 1), stack(outs_s, 2), stack(outs_s, 3))
```

```python
import functools
import math

import jax
import jax.numpy as jnp
from jax import lax
from jax.experimental import pallas as pl
from jax.experimental.pallas import tpu as pltpu

F32 = jnp.float32
BF16 = jnp.bfloat16

HEAD_DIM = 64
LANES = 128
SUBLANES = 8
WKV_SUB = 64
RMS_EPS = 1e-6
GN_EPS = 64e-5
VMEM_LIMIT_BYTES = 60 * 1024 * 1024


def _params(*sem):
    return pltpu.CompilerParams(dimension_semantics=sem, vmem_limit_bytes=VMEM_LIMIT_BYTES)


def _dot(a, b):
    return jnp.dot(a, b, preferred_element_type=F32)


def _dot_nt(a, b):
    return lax.dot_general(a, b, (((1,), (1,)), ((), ())), preferred_element_type=F32)


def _dot_tn(a, b):
    return lax.dot_general(a, b, (((0,), (0,)), ((), ())), preferred_element_type=F32)


def _sigmoid(z):
    return 0.5 + 0.5 * jnp.tanh(0.5 * z)


def _split_dot(x, b_exact, terms, dot=_dot):
    acc = None
    rem = x
    for _ in range(terms):
        hi = rem.astype(BF16)
        part = dot(hi, b_exact)
        acc = part if acc is None else acc + part
        rem = rem - hi.astype(F32)
    return acc


def _row_iota(shape):
    return lax.broadcasted_iota(jnp.int32, shape, 0)


def _init_carry(carry_ref, j, i):
    @pl.when(i == 0)
    def _():
        carry_ref[j] = jnp.zeros(carry_ref.shape[1:], F32)


def _shifted_rows(u, carry, states):
    tm = u.shape[0]
    row = _row_iota(u.shape)
    r1 = pltpu.roll(u, 1, 0)
    r2 = pltpu.roll(u, 2, 0)
    if carry is not None:
        carry_ref, j, i, tiles_per_seq = carry
        c = carry_ref[j]
        keep = (i % tiles_per_seq) != 0
        first1 = jnp.where(keep, c[SUBLANES - 1:SUBLANES, :], 0.0)
        first2 = jnp.where(keep, c[SUBLANES - 2:SUBLANES - 1, :], 0.0)
        carry_ref[j] = u[tm - SUBLANES:, :]
        return jnp.where(row == 0, first1, r1), jnp.where(row == 0, first2, jnp.where(row == 1, first1, r2))
    first1, first2 = _expand_state_rows(states[0], states[1], tm)
    t = row % SUBLANES
    return jnp.where(t == 0, first1, r1), (None if first2 is None else jnp.where(t < 2, first2, r2))


def _expand_state_rows(st1, st0, tm):
    ns, n = st1.shape
    spread = lambda x: jnp.broadcast_to(x[:, None, :], (ns, SUBLANES, n)).reshape(tm, n)
    first1 = spread(st1)
    if st0 is None:
        return first1, None
    t = _row_iota((tm, n)) % SUBLANES
    return first1, jnp.where(t == 0, spread(st0), first1)


def _mix_kernel(*refs, sample, tiles_per_seq):
    if sample:
        (x_ref, ext_ref, g1_ref, mu_ref, w0_ref, a0_ref, dw1_ref, aw1_ref, gw1_ref, dw2_ref,
         aw2_ref, gw2_ref, xnb_ref, lw_ref, a_ref, g_ref, last_ref) = refs
    else:
        (x_ref, g1_ref, mu_ref, w0_ref, a0_ref, dw1_ref, aw1_ref, gw1_ref, dw2_ref,
         aw2_ref, gw2_ref, xnb_ref, lw_ref, a_ref, g_ref, last_ref, carry_ref) = refs
    if not sample:
        _init_carry(carry_ref, 0, pl.program_id(0))
    x = x_ref[...]
    xn = x * lax.rsqrt(jnp.mean(x * x, axis=-1, keepdims=True) + RMS_EPS) * g1_ref[...]
    tm = xn.shape[0]
    if sample:
        prev, _unused = _shifted_rows(xn, None, (ext_ref[...], None))
        last_ref[...] = xn
    else:
        prev, _unused = _shifted_rows(xn, (carry_ref, 0, pl.program_id(0), tiles_per_seq), None)
        last_ref[...] = xn[tm - SUBLANES:, :]
    dx = prev - xn
    xnb_ref[...] = xn.astype(BF16)
    xw = (xn + dx * mu_ref[0:1, :]).astype(BF16)
    hw = jnp.tanh(_dot(xw, dw1_ref[...]))
    wl = w0_ref[...] + _dot(hw.astype(BF16), dw2_ref[...])
    lw_ref[...] = -_sigmoid(wl) * math.exp(-0.5)
    xa = (xn + dx * mu_ref[1:2, :]).astype(BF16)
    ha = _dot(xa, aw1_ref[...])
    a_ref[...] = _sigmoid(a0_ref[...] + _dot(ha.astype(BF16), aw2_ref[...]))
    xg = (xn + dx * mu_ref[2:3, :]).astype(BF16)
    hg = _sigmoid(_dot(xg, gw1_ref[...]))
    g_ref[...] = _dot(hg.astype(BF16), gw2_ref[...])


def _mix_call(x2d, ext, p, *, sample, seq_len, tm):
    rows, d = x2d.shape
    g = p["dw2"].shape[1]
    n_i = rows // tm
    row_blk = lambda w: pl.BlockSpec((tm, w), lambda i: (i, 0))
    full = lambda a: pl.BlockSpec(a.shape, lambda i: (0, 0))
    weights = [p["norm1_g"], p["mu_lora"], p["decay_w0"], p["aaa_a0"], p["dw1"], p["aw1"], p["gw1"],
               p["dw2"], p["aw2"], p["gw2"]]
    state_blk = pl.BlockSpec((tm // seq_len, d), lambda i: (i, 0))
    in_specs = [row_blk(d)] + ([state_blk] if sample else []) + [full(w) for w in weights]
    args = [x2d] + ([ext] if sample else []) + weights
    last_rows = tm if sample else SUBLANES
    out_shape = [jax.ShapeDtypeStruct((rows, d), BF16)] + [jax.ShapeDtypeStruct((rows, g), F32)] * 3 + [
        jax.ShapeDtypeStruct((n_i * last_rows, d), F32)]
    out_specs = [row_blk(d), row_blk(g), row_blk(g), row_blk(g), pl.BlockSpec((last_rows, d), lambda i: (i, 0))]
    return pl.pallas_call(
        functools.partial(_mix_kernel, sample=sample, tiles_per_seq=max(seq_len // tm, 1)),
        grid=(n_i,), in_specs=in_specs, out_specs=out_specs, out_shape=out_shape,
        scratch_shapes=[] if sample else [pltpu.VMEM((1, SUBLANES, d), F32)],
        compiler_params=_params("arbitrary"), name="mix")(*args)


def _rkv_kernel(*refs, sample, tiles_per_seq):
    if sample:
        x_ref, w_ref, mu_ref, shift_ref, o_ref = refs
    else:
        x_ref, w_ref, mu_ref, o_ref, carry_ref = refs
    if not sample:
        _init_carry(carry_ref, pl.program_id(1), pl.program_id(0))
    w = w_ref[...]
    p = _dot(x_ref[...], w)
    if sample:
        prev, _unused = _shifted_rows(p, None, (_dot(shift_ref[...], w), None))
    else:
        prev, _unused = _shifted_rows(p, (carry_ref, pl.program_id(1), pl.program_id(0), tiles_per_seq), None)
    o_ref[...] = p + mu_ref[...] * (prev - p)


def _rkv_call(xnb, w_in, mu, ext, *, sample, seq_len, tm, tn):
    rows, d = xnb.shape
    n = mu.shape[1]
    n_i, n_j = rows // tm, n // tn
    in_specs = [pl.BlockSpec((tm, d), lambda i, j: (i, 0)), pl.BlockSpec((d, tn), lambda i, j: (0, j)),
                pl.BlockSpec((1, tn), lambda i, j: (0, j))]
    args = [xnb, w_in, mu]
    if sample:
        in_specs.append(pl.BlockSpec((tm // seq_len, d), lambda i, j: (i, 0)))
        args.append(ext)
    return pl.pallas_call(
        functools.partial(_rkv_kernel, sample=sample, tiles_per_seq=max(seq_len // tm, 1)),
        grid=(n_i, n_j), in_specs=in_specs, out_specs=pl.BlockSpec((tm, tn), lambda i, j: (i, j)),
        out_shape=jax.ShapeDtypeStruct((rows, n), F32),
        scratch_shapes=[] if sample else [pltpu.VMEM((n_j, SUBLANES, tn), F32)],
        compiler_params=_params("arbitrary", "arbitrary"), name="rkv")(*args)


def _convbr_kernel(*refs, sample, tiles_per_seq):
    if sample:
        x_ref, wb_ref, wc_ref, wx_ref, cw_ref, st1_ref, st0_ref, y_ref, last_ref = refs
    else:
        x_ref, wb_ref, wc_ref, wx_ref, cw_ref, y_ref, last_ref, carry_ref = refs
    if not sample:
        _init_carry(carry_ref, pl.program_id(1), pl.program_id(0))
    x = x_ref[...]
    cx = _dot(x, wc_ref[...]) * _dot(x, wx_ref[...])
    tm = cx.shape[0]
    if sample:
        prev1, prev2 = _shifted_rows(cx, None, (st1_ref[...], st0_ref[...]))
        last_ref[...] = cx
    else:
        prev1, prev2 = _shifted_rows(cx, (carry_ref, pl.program_id(1), pl.program_id(0), tiles_per_seq), None)
        last_ref[...] = cx[tm - SUBLANES:, :]
    hconv = prev2 * cw_ref[0:1, :] + prev1 * cw_ref[1:2, :] + cx * cw_ref[2:3, :]
    y_ref[...] = _dot(x, wb_ref[...]) * hconv


def _convbr_call(xnb, w_in, conv_w, e1, e2, *, col0, sample, seq_len, tm, tn):
    rows, d = xnb.shape
    gc = conv_w.shape[1]
    n_i, n_j = rows // tm, gc // tn
    off = col0 // tn
    nb = gc // tn
    wspec = lambda k: pl.BlockSpec((d, tn), lambda i, j: (0, off + k * nb + j))
    tile = pl.BlockSpec((tm, tn), lambda i, j: (i, j))
    in_specs = [pl.BlockSpec((tm, d), lambda i, j: (i, 0)), wspec(0), wspec(1), wspec(2),
                pl.BlockSpec((3, tn), lambda i, j: (0, j))]
    args = [xnb, w_in, w_in, w_in, conv_w]
    if sample:
        sspec = pl.BlockSpec((tm // seq_len, tn), lambda i, j: (i, j))
        in_specs += [sspec, sspec]
        args += [e1, e2]
    last_rows = tm if sample else SUBLANES
    return pl.pallas_call(
        functools.partial(_convbr_kernel, sample=sample, tiles_per_seq=max(seq_len // tm, 1)),
        grid=(n_i, n_j), in_specs=in_specs,
        out_specs=[tile, pl.BlockSpec((last_rows, tn), lambda i, j: (i, j))],
        out_shape=[jax.ShapeDtypeStruct((rows, gc), F32), jax.ShapeDtypeStruct((n_i * last_rows, gc), F32)],
        scratch_shapes=[] if sample else [pltpu.VMEM((n_j, SUBLANES, tn), F32)],
        compiler_params=_params("arbitrary", "arbitrary"), name="convbr")(*args)


def _wkv_masks(c):
    m = 2 * WKV_SUB
    r2 = jnp.bitwise_and(_row_iota((m, m)), WKV_SUB - 1)
    c2 = jnp.bitwise_and(lax.broadcasted_iota(jnp.int32, (m, m), 1), WKV_SUB - 1)
    shift = int(math.log2(c))
    same = jnp.right_shift(r2, shift) == jnp.right_shift(c2, shift)
    eye = (_row_iota((m, m)) == lax.broadcasted_iota(jnp.int32, (m, m), 1)).astype(F32)
    lane_lo = lax.broadcasted_iota(jnp.int32, (WKV_SUB, LANES), 1) < HEAD_DIM
    return lane_lo, same & (c2 < r2), same & (c2 <= r2), eye


def _stack_heads(x, lane_lo):
    z = jnp.zeros_like(x)
    return jnp.concatenate([jnp.where(lane_lo, x, z), jnp.where(lane_lo, z, x)], axis=0)


def _fold_heads(x):
    return x[:WKV_SUB] + x[WKV_SUB:]


def _wkv_phase1(units, masks, c):
    lane_lo, strict, incl, eye = masks
    m = 2 * WKV_SUB
    phs = []
    for r, kt, v, kp, bt, lg, lw in units:
        e1 = jnp.exp(lg)
        e0 = jnp.exp(lg - lw)
        ei = jnp.exp(-lg)
        phs.append(dict(
            e1=e1, rt_st=_stack_heads(r * e1, lane_lo), kp_b=_stack_heads(kp * e0, lane_lo).astype(BF16),
            kh_st=_stack_heads(kt * ei, lane_lo), bh_st=_stack_heads(bt * ei, lane_lo),
            v_st=_stack_heads(v, lane_lo)))
    for ph in phs:
        ph["kh_b"], ph["bh_b"], ph["v_b"] = (ph[n].astype(BF16) for n in ("kh_st", "bh_st", "v_st"))
    gs = [_dot_nt(jnp.concatenate([ph["kp_b"], ph["rt_st"].astype(BF16)], axis=0),
                  jnp.concatenate([ph["bh_b"], ph["kh_b"]], axis=0)) for ph in phs]
    ps = [-jnp.where(strict, g[:m, :m], 0.0) for g in gs]
    ts = [eye + p for p in ps]
    n = 1
    while 2 * n < c:
        pbs = [p.astype(BF16) for p in ps]
        ps = [_dot(pb, pb) for pb in pbs]
        ts = [t + _dot(t.astype(BF16), p.astype(BF16)) for t, p in zip(ts, ps)]
        n *= 2
    abvs = [_dot(jnp.concatenate([jnp.where(strict, g[:m, m:], 0.0), jnp.where(incl, g[m:, m:], 0.0)],
                                 axis=0).astype(BF16), ph["v_b"]) for g, ph in zip(gs, phs)]
    tts = [_dot(t.astype(BF16), jnp.concatenate([ph["kp_b"], abv[:m].astype(BF16)], axis=1))
           for t, ph, abv in zip(ts, phs, abvs)]
    bbtts = [_dot(jnp.where(incl, g[m:, :m], 0.0).astype(BF16), tt.astype(BF16)) for g, tt in zip(gs, tts)]
    for ph, abv, tt, bbtt in zip(phs, abvs, tts, bbtts):
        ph["rq"] = _fold_heads(ph["rt_st"] - bbtt[:, :LANES])
        ph["ov"] = _fold_heads(abv[m:] - bbtt[:, LANES:])
        ph["tk_st"], ph["tav_st"] = tt[:, :LANES], tt[:, LANES:]
    return phs


def _wkv_transitions(phs):
    gams = [ph["e1"][WKV_SUB - 1:WKV_SUB, :] for ph in phs]
    kgs = [(_dot_tn(ph["tk_st"].astype(BF16), ph["bh_b"]) * gam).astype(BF16) for ph, gam in zip(phs, gams)]
    bcgs = [_dot_tn(jnp.concatenate([ph["v_b"], (-ph["tav_st"]).astype(BF16)], axis=0),
                    jnp.concatenate([ph["kh_b"], ph["bh_b"]], axis=0)) * gam for ph, gam in zip(phs, gams)]
    return gams, kgs, bcgs


def _wkv_units_small(states, units, c, lane_lo_c):
    rows = lambda u: slice(u * c, (u + 1) * c)
    rows_hi = lambda u: slice(WKV_SUB + u * c, WKV_SUB + (u + 1) * c)
    ous = [_dot_nt(jnp.concatenate([ph["rq"][rows(u)], ph["tkm"][rows(u)]], axis=0).astype(BF16), s.astype(BF16))
           for s, (ph, u) in zip(states, units)]
    outs, new_states = [], []
    for s, (ph, u), ou in zip(states, units, ous):
        pick = lambda x: jnp.concatenate([x[rows(u)], x[rows_hi(u)]], axis=0)
        uu = ou[c:] + ph["tav"][rows(u)]
        z = jnp.zeros_like(uu)
        u_st = jnp.concatenate([jnp.where(lane_lo_c, uu, z), jnp.where(lane_lo_c, z, uu)], axis=0)
        gam = ph["e1"][(u + 1) * c - 1:(u + 1) * c, :]
        lhs = jnp.concatenate([pick(ph["v_st"]), -u_st], axis=0).astype(BF16)
        rhs = (jnp.concatenate([pick(ph["kh_st"]), pick(ph["bh_st"])], axis=0) * gam).astype(BF16)
        outs.append(ou[:c] + ph["ov"][rows(u)])
        new_states.append(s * gam + _dot_tn(lhs, rhs))
    return outs, new_states


def _wkv_kernel(*refs, sample, chunk, n_pairs, n_sub):
    if sample:
        (r_ref, k_ref, v_ref, lw_ref, a_ref, g_ref, kk_ref, ka_ref, rk_ref, lnw_ref, lnb_ref, sin_ref,
         o_ref, sout_ref) = refs
    else:
        (r_ref, k_ref, v_ref, lw_ref, a_ref, g_ref, kk_ref, ka_ref, rk_ref, lnw_ref, lnb_ref,
         o_ref, sout_ref, s_s) = refs
    c = chunk
    lane128 = lax.broadcasted_iota(jnp.int32, (LANES, LANES), 1)
    row128 = _row_iota((LANES, LANES))
    blockdiag = (row128 < HEAD_DIM) == (lane128 < HEAD_DIM)
    block_ones = blockdiag.astype(BF16)
    seg_sum = lambda z: _split_dot(z, block_ones, 2)
    to_blockdiag = lambda x: jnp.where(blockdiag, jnp.concatenate([x, x], axis=1), 0.0)
    from_blockdiag = lambda s_: s_[:, :HEAD_DIM] + s_[:, HEAD_DIM:]
    masks = _wkv_masks(c)
    r2 = jnp.bitwise_and(_row_iota((WKV_SUB, WKV_SUB)), WKV_SUB - 1)
    c2 = lax.broadcasted_iota(jnp.int32, (WKV_SUB, WKV_SUB), 1)
    shift = int(math.log2(c))
    tril_b = ((jnp.right_shift(r2, shift) == jnp.right_shift(c2, shift)) & (c2 <= r2)).astype(BF16)
    lane_lo_c = lax.broadcasted_iota(jnp.int32, (c, LANES), 1) < HEAD_DIM

    if not sample:
        @pl.when(pl.program_id(2) == 0)
        def _():
            s_s[...] = jnp.zeros_like(s_s)

    lgs = []
    for sb in range(n_sub):
        rows = slice(sb * WKV_SUB, (sb + 1) * WKV_SUB)
        lgs.append(_split_dot(lw_ref[rows, :], tril_b, 3, dot=lambda x, b: _dot(b, x)))

    pairs = range(n_pairs)
    lanes = [slice(q * LANES, (q + 1) * LANES) for q in pairs]
    r = [r_ref[:, l] for l in lanes]
    k = [k_ref[:, l] for l in lanes]
    v = [v_ref[:, l] for l in lanes]
    a = [a_ref[:, l] for l in lanes]
    lw = [lw_ref[:, l] for l in lanes]
    kk = [k[q] * kk_ref[:, lanes[q]] for q in pairs]
    norms = [seg_sum(x * x) for x in kk]
    kk = [x / jnp.maximum(jnp.sqrt(n2), 1e-12) for x, n2 in zip(kk, norms)]
    kmod = [k[q] * (1.0 + (a[q] - 1.0) * ka_ref[:, lanes[q]]) for q in pairs]
    bt = [kk[q] * a[q] for q in pairs]

    subs = [(q, sb) for sb in range(n_sub) for q in pairs]
    units = []
    for q, sb in subs:
        rows = slice(sb * WKV_SUB, (sb + 1) * WKV_SUB)
        units.append((r[q][rows], kmod[q][rows], v[q][rows], kk[q][rows], bt[q][rows],
                      lgs[sb][:, lanes[q]], lw[q][rows]))
    phs = dict(zip(subs, _wkv_phase1(units, masks, c)))

    outs = {q: [] for q in pairs}
    if sample:
        per_sub = WKV_SUB // c
        todo = []
        for (q, sb), ph in phs.items():
            ph["tkm"] = _fold_heads(ph["tk_st"])
            ph["tav"] = _fold_heads(ph["tav_st"])
            todo += [(q, sb * per_sub + u, ph, u) for u in range(per_sub)]
        o_units, new_states = _wkv_units_small([to_blockdiag(sin_ref[seq, q]) for q, seq, _, _ in todo],
                                               [(ph, u) for _, _, ph, u in todo], c, lane_lo_c)
        for (q, seq, _, _), o, s_new in zip(todo, o_units, new_states):
            sout_ref[seq, q] = from_blockdiag(s_new)
            outs[q].append(o)
    else:
        gams, kgs, bcgs = _wkv_transitions([phs[key] for key in subs])
        trans = dict(zip(subs, zip(gams, kgs, bcgs)))
        s = [s_s[q] for q in pairs]
        for sb in range(n_sub):
            sb16 = [x.astype(BF16) for x in s]
            for q in pairs:
                outs[q].append(_dot_nt(phs[q, sb]["rq"].astype(BF16), sb16[q]) + phs[q, sb]["ov"])
            s = [s[q] * trans[q, sb][0] - _dot(sb16[q], trans[q, sb][1]) + trans[q, sb][2] for q in pairs]
        for q in pairs:
            s_s[q] = s[q]

        @pl.when(pl.program_id(2) == pl.num_programs(2) - 1)
        def _():
            for q in pairs:
                sout_ref[0, q] = from_blockdiag(s[q])

    o = [jnp.concatenate(outs[q], axis=0) for q in pairs]
    inv_n = 1.0 / HEAD_DIM
    mu = [_split_dot(x, block_ones, 1) * inv_n for x in o]
    dev = [x - m_ for x, m_ in zip(o, mu)]
    var = [_split_dot(x * x, block_ones, 1) * inv_n for x in dev]
    bonus = [seg_sum(r[q] * kmod[q] * rk_ref[:, lanes[q]]) * v[q] for q in pairs]
    for q in pairs:
        on = dev[q] * lax.rsqrt(var[q] + GN_EPS) * lnw_ref[:, lanes[q]] + lnb_ref[:, lanes[q]]
        o_ref[:, lanes[q]] = ((on + bonus[q]) * g_ref[:, lanes[q]]).astype(BF16)


def _wkv_call(rkv, lw, a, g, p, s_bd, *, sample, seq_len, t_blk, chunk, n_pairs):
    rows, gdim = lw.shape
    npair = gdim // LANES
    nseq = rows // seq_len
    width = n_pairs * LANES
    pair_blocks = npair // n_pairs
    head = [p["k_k"], p["k_a"], p["r_k"], p["ln_x_w"], p["ln_x_b"]]
    if sample:
        seq_blk = t_blk // seq_len
        grid = (nseq // seq_blk, pair_blocks)
        blk = lambda off: pl.BlockSpec((t_blk, width), lambda b, q, off=off: (b, off + q))
        hspec = pl.BlockSpec((1, width), lambda b, q: (0, q))
        sspec = pl.BlockSpec((seq_blk, n_pairs, LANES, HEAD_DIM), lambda b, q: (b, q, 0, 0))
        extra_specs, extra_args = [sspec], [s_bd]
        sem = ("arbitrary", "arbitrary")
        scratch = []
    else:
        nblk = seq_len // t_blk
        grid = (nseq, pair_blocks, nblk)
        blk = lambda off: pl.BlockSpec((t_blk, width), lambda b, q, n, off=off: (b * nblk + n, off + q))
        hspec = pl.BlockSpec((1, width), lambda b, q, n: (0, q))
        sspec = pl.BlockSpec((1, n_pairs, LANES, HEAD_DIM), lambda b, q, n: (b, q, 0, 0))
        extra_specs, extra_args = [], []
        sem = ("arbitrary", "arbitrary", "arbitrary")
        scratch = [pltpu.VMEM((n_pairs, LANES, LANES), F32)]
    in_specs = [blk(0), blk(pair_blocks), blk(2 * pair_blocks), blk(0), blk(0), blk(0)] + [hspec] * 5 + extra_specs
    args = [rkv, rkv, rkv, lw, a, g] + head + extra_args
    return pl.pallas_call(
        functools.partial(_wkv_kernel, sample=sample, chunk=chunk, n_pairs=n_pairs, n_sub=t_blk // WKV_SUB),
        grid=grid, in_specs=in_specs, out_specs=[blk(0), sspec],
        out_shape=[jax.ShapeDtypeStruct((rows, gdim), BF16),
                   jax.ShapeDtypeStruct((nseq, npair, LANES, HEAD_DIM), F32)],
        scratch_shapes=scratch, compiler_params=_params(*sem), name="wkv")(*args)


def _outproj_kernel(orw_ref, ocv_ref, w_ref, x_ref, o_ref):
    lhs = jnp.concatenate([orw_ref[...], ocv_ref[...]], axis=1)
    o_ref[...] = x_ref[...] + _dot(lhs, w_ref[...])


def _outproj_call(orw, ocv, w_out, x2d, *, tm, tn):
    rows, d = x2d.shape
    g, gc = orw.shape[1], ocv.shape[1]
    tile = pl.BlockSpec((tm, tn), lambda i, j: (i, j))
    return pl.pallas_call(
        _outproj_kernel, grid=(rows // tm, d // tn),
        in_specs=[pl.BlockSpec((tm, g), lambda i, j: (i, 0)), pl.BlockSpec((tm, gc), lambda i, j: (i, 0)),
                  pl.BlockSpec((d, tn), lambda i, j: (0, j)), tile],
        out_specs=tile, out_shape=jax.ShapeDtypeStruct((rows, d), F32),
        compiler_params=_params("arbitrary", "arbitrary"), name="outproj")(orw, ocv, w_out, x2d)


def _norm_kernel(x_ref, g_ref, o_ref):
    x = x_ref[...]
    y = x * lax.rsqrt(jnp.mean(x * x, axis=-1, keepdims=True) + RMS_EPS) * g_ref[...]
    o_ref[...] = y.astype(o_ref.dtype)


def _norm_call(x2d, g, dtype, *, tm):
    rows, d = x2d.shape
    blk = pl.BlockSpec((tm, d), lambda i: (i, 0))
    return pl.pallas_call(
        _norm_kernel, grid=(rows // tm,), in_specs=[blk, pl.BlockSpec((1, d), lambda i: (0, 0))],
        out_specs=blk, out_shape=jax.ShapeDtypeStruct((rows, d), dtype),
        compiler_params=_params("arbitrary"), name="rmsnorm")(x2d, g)


def _ffnup_kernel(*refs, sample, tiles_per_seq):
    if sample:
        x_ref, w1_ref, w3_ref, cw_ref, cb_ref, st1_ref, st0_ref, h_ref, last_ref = refs
        carry = None
    else:
        x_ref, w1_ref, w3_ref, cw_ref, cb_ref, h_ref, last_ref, carry_ref = refs
        carry = (carry_ref, pl.program_id(1), pl.program_id(0), tiles_per_seq)
        _init_carry(*carry[:3])
    tn = h_ref.shape[1]
    w13 = jnp.concatenate([w1_ref[...].astype(BF16), w3_ref[...].astype(BF16)], axis=1)
    uw = _dot(x_ref[...], w13)
    u = uw[:, :tn]
    tm = u.shape[0]
    states = (st1_ref[...], st0_ref[...]) if sample else None
    prev1, prev2 = _shifted_rows(u, carry, states)
    if sample:
        last_ref[...] = u.reshape(tm // SUBLANES, SUBLANES, tn)[:, SUBLANES - 2:, :]
    else:
        last_ref[...] = u[tm - SUBLANES:, :]
    z = prev2 * cw_ref[0:1, :] + prev1 * cw_ref[1:2, :] + u * cw_ref[2:3, :] + cb_ref[...]
    hz = 0.5 * z
    h_ref[...] = ((hz + hz * jnp.tanh(hz)) * uw[:, tn:]).astype(BF16)


def _ffnup_call(hn, w1, w3, conv_w, conv_b, st1, st0, *, sample, seq_len, tm, tn):
    rows, d = hn.shape
    dff = conv_w.shape[1]
    n_i, n_j = rows // tm, dff // tn
    tile = pl.BlockSpec((tm, tn), lambda i, j: (i, j))
    wspec = pl.BlockSpec((d, tn), lambda i, j: (0, j))
    in_specs = [pl.BlockSpec((tm, d), lambda i, j: (i, 0)), wspec, wspec,
                pl.BlockSpec((3, tn), lambda i, j: (0, j)), pl.BlockSpec((1, tn), lambda i, j: (0, j))]
    args = [hn, w1, w3, conv_w, conv_b]
    if sample:
        sspec = pl.BlockSpec((tm // seq_len, tn), lambda i, j: (i, j))
        in_specs += [sspec, sspec]
        args += [st1, st0]
    if sample:
        last_spec = pl.BlockSpec((tm // seq_len, 2, tn), lambda i, j: (i, 0, j))
        last_shape = jax.ShapeDtypeStruct((rows // seq_len, 2, dff), F32)
    else:
        last_spec = pl.BlockSpec((SUBLANES, tn), lambda i, j: (i, j))
        last_shape = jax.ShapeDtypeStruct((n_i * SUBLANES, dff), F32)
    return pl.pallas_call(
        functools.partial(_ffnup_kernel, sample=sample, tiles_per_seq=max(seq_len // tm, 1)),
        grid=(n_i, n_j), in_specs=in_specs,
        out_specs=[tile, last_spec], out_shape=[jax.ShapeDtypeStruct((rows, dff), BF16), last_shape],
        scratch_shapes=[] if sample else [pltpu.VMEM((n_j, SUBLANES, tn), F32)],
        compiler_params=_params("arbitrary", "arbitrary"), name="ffnup")(*args)


def _ffndown_kernel(h_ref, w_ref, x_ref, o_ref):
    o_ref[...] = x_ref[...] + _dot(h_ref[...], w_ref[...])


def _ffndown_call(h, w2, x1, *, tm, tn):
    rows, dff = h.shape
    d = w2.shape[1]
    tile = pl.BlockSpec((tm, tn), lambda i, j: (i, j))
    return pl.pallas_call(
        _ffndown_kernel, grid=(rows // tm, d // tn),
        in_specs=[pl.BlockSpec((tm, dff), lambda i, j: (i, 0)), pl.BlockSpec((dff, tn), lambda i, j: (0, j)), tile],
        out_specs=tile, out_shape=jax.ShapeDtypeStruct((rows, d), F32),
        compiler_params=_params("arbitrary", "arbitrary"), name="ffndown")(h, w2, x1)


def _tile(n, want):
    t = min(n, want)
    while n % t or (t % SUBLANES and t != n):
        t -= 1
    return t


def _col_tile(n, want):
    t = min(n, want)
    while n % t or t % LANES:
        t -= LANES
    return t


def _layer(x, states, p, *, sample):
    nseq, seq_len, d = x.shape
    rows = nseq * seq_len
    g = p["dw2"].shape[1]
    gc = d - g
    dff = p["ffn_conv_w"].shape[1]
    npair = g // LANES
    x2d = x.reshape(rows, d)
    big = dict(sample=sample, seq_len=seq_len)
    tm_big = rows if sample else _tile(seq_len, 1024)

    if sample:
        assert seq_len == SUBLANES, "the sample path shifts rows inside 8-row groups"
        shift, wkv, conv, ffn = states
        ext_x = shift
        ext_p = shift.astype(BF16)
        ce1, ce2 = conv[:, 1], conv[:, 0]
        fe1, fe2 = ffn[:, 1], ffn[:, 0]
        s_bd = wkv.reshape(nseq, npair, LANES, HEAD_DIM)
    else:
        ext_x = ext_p = ce1 = ce2 = fe1 = fe2 = s_bd = None

    tm_mix = _tile(rows if sample else seq_len, 256)
    xnb, lw, a, gate, xlast = _mix_call(x2d, ext_x, p, sample=sample, seq_len=seq_len, tm=tm_mix)
    rkv = _rkv_call(xnb, p["w_in"], p["mu_rkv"], ext_p, tm=tm_big, tn=_col_tile(3 * g, 1024), **big)
    tm_conv = _tile(rows if sample else seq_len, 1024)
    ycv, cxlast = _convbr_call(xnb, p["w_in"], p["conv_w"], ce1, ce2, col0=3 * g, tm=tm_conv,
                               tn=_col_tile(gc, 512), **big)
    if sample:
        t_blk, chunk = WKV_SUB, seq_len
    else:
        chunk = WKV_SUB
        t_blk = _tile(seq_len, 4 * WKV_SUB)
    orw, s_new = _wkv_call(rkv, lw, a, gate, p, s_bd, t_blk=t_blk, chunk=chunk,
                           n_pairs=math.gcd(npair, 8), **big)
    tm_e = _tile(rows, 512)
    ocv = _norm_call(ycv, p["conv_norm_g"], BF16, tm=_tile(rows, 512))
    x1 = _outproj_call(orw, ocv, p["w_out"], x2d, tm=_tile(rows, 1024), tn=_col_tile(d, 1024))
    hn = _norm_call(x1, p["norm2_g"], BF16, tm=_tile(rows, 512))
    tm_ffn = rows if sample else _tile(seq_len, 2048)
    h, ulast = _ffnup_call(hn, p["ffn_w1"], p["ffn_w3"], p["ffn_conv_w"], p["ffn_conv_b"], fe1, fe2,
                           tm=tm_ffn, tn=_col_tile(dff, 256), **big)
    x2 = _ffndown_call(h, p["ffn_w2"], x1, tm=tm_e, tn=_col_tile(d, 512))

    def last_rows(arr, tile_rows, k):
        if sample:
            return arr.reshape(nseq, seq_len, -1)[:, seq_len - k:]
        per_seq = seq_len // tile_rows
        return arr.reshape(nseq, per_seq, SUBLANES, -1)[:, -1, SUBLANES - k:]
    new_shift = last_rows(xlast, tm_mix, 1)[:, 0]
    new_conv = last_rows(cxlast, tm_conv, 2)
    new_ffn = ulast if sample else last_rows(ulast, tm_ffn, 2)
    new_wkv = s_new.reshape(nseq, 2 * npair, HEAD_DIM, HEAD_DIM)
    return x2.reshape(nseq, seq_len, d), new_shift, new_wkv, new_conv, new_ffn


def _pad_to(a, axis, mult):
    pad = (-a.shape[axis]) % mult
    if not pad:
        return a
    widths = [(0, 0)] * a.ndim
    widths[axis] = (0, pad)
    return jnp.pad(a, widths)


def kernel(x_prompt, x_sample, state_shift, state_wkv, state_conv, state_ffn, norm1_g, w_in, mu_rkv, mu_lora, decay_w0, decay_w1, decay_w2, aaa_a0, aaa_a1, aaa_a2, gate_g1, gate_g2, k_k, k_a, r_k, ln_x_w, ln_x_b, conv_w, conv_norm_g, w_out, norm2_g, ffn_w1, ffn_conv_w, ffn_conv_b, ffn_w3, ffn_w2, final_norm_g):
    depth = w_in.shape[0]
    d = x_prompt.shape[-1]
    row = lambda v: v.reshape(1, -1).astype(F32)
    yp, ys = x_prompt, x_sample
    outs_p, outs_s = [], []
    for l in range(depth):
        p = dict(
            norm1_g=row(norm1_g[l]), w_in=w_in[l].astype(BF16), mu_rkv=row(mu_rkv[l]), mu_lora=mu_lora[l],
            decay_w0=row(decay_w0[l]), aaa_a0=row(aaa_a0[l]),
            dw1=decay_w1[l].astype(BF16), dw2=decay_w2[l].astype(BF16),
            aw1=aaa_a1[l].astype(BF16), aw2=aaa_a2[l].astype(BF16),
            gw1=_pad_to(gate_g1[l], 1, LANES).astype(BF16), gw2=_pad_to(gate_g2[l], 0, LANES).astype(BF16),
            k_k=row(k_k[l]), k_a=row(k_a[l]), r_k=row(r_k[l]), ln_x_w=row(ln_x_w[l]), ln_x_b=row(ln_x_b[l]),
            conv_w=conv_w[l], conv_norm_g=row(conv_norm_g[l]), w_out=w_out[l].astype(BF16),
            norm2_g=row(norm2_g[l]), ffn_w1=ffn_w1[l], ffn_w3=ffn_w3[l],
            ffn_conv_w=ffn_conv_w[l], ffn_conv_b=row(ffn_conv_b[l]), ffn_w2=ffn_w2[l].astype(BF16))
        yp, *st_p = _layer(yp, None, p, sample=False)
        ys, *st_s = _layer(ys, (state_shift[l], state_wkv[l], state_conv[l], state_ffn[l]), p, sample=True)
        outs_p.append(st_p)
        outs_s.append(st_s)
    fin = row(final_norm_g)
    y_prompt = _norm_call(yp.reshape(-1, d), fin, F32, tm=_tile(yp.shape[0] * yp.shape[1], 512)).reshape(yp.shape)
    y_sample = _norm_call(ys.reshape(-1, d), fin, F32, tm=_tile(ys.shape[0] * ys.shape[1], 512)).reshape(ys.shape)
    stack = lambda outs, k: jnp.stack([o[k] for o in outs])
    return (y_prompt, y_sample,
            stack(outs_p, 0), stack(outs_p, 1), stack(outs_p, 2), stack(outs_p, 3),
            stack(outs_s, 0), stack(outs_s, 1), stack(outs_s, 2), stack(outs_s, 3))
```

```python
import functools
import math

import jax
import jax.numpy as jnp
from jax import lax
from jax.experimental import pallas as pl
from jax.experimental.pallas import tpu as pltpu

F32 = jnp.float32
BF16 = jnp.bfloat16

HEAD_DIM = 64
LANES = 128
SUBLANES = 8
WKV_SUB = 64
RMS_EPS = 1e-6
GN_EPS = 64e-5
VMEM_LIMIT_BYTES = 60 * 1024 * 1024


def _params(*sem):
    return pltpu.CompilerParams(dimension_semantics=sem, vmem_limit_bytes=VMEM_LIMIT_BYTES)


def _dot(a, b):
    return jnp.dot(a, b, preferred_element_type=F32)


def _dot_nt(a, b):
    return lax.dot_general(a, b, (((1,), (1,)), ((), ())), preferred_element_type=F32)


def _dot_tn(a, b):
    return lax.dot_general(a, b, (((0,), (0,)), ((), ())), preferred_element_type=F32)


def _sigmoid(z):
    return 1.0 / (1.0 + jnp.exp(-z))


def _split_dot(x, b_exact, terms, dot=_dot):
    acc = None
    rem = x
    for _ in range(terms):
        hi = rem.astype(BF16)
        part = dot(hi, b_exact)
        acc = part if acc is None else acc + part
        rem = rem - hi.astype(F32)
    return acc


def _row_iota(shape):
    return lax.broadcasted_iota(jnp.int32, shape, 0)


def _init_carry(carry_ref, j, i):
    @pl.when(i == 0)
    def _():
        carry_ref[j] = jnp.zeros(carry_ref.shape[1:], F32)


def _shifted_rows(u, carry, states):
    tm = u.shape[0]
    row = _row_iota(u.shape)
    r1 = pltpu.roll(u, 1, 0)
    r2 = pltpu.roll(u, 2, 0)
    if carry is not None:
        carry_ref, j, i, tiles_per_seq = carry
        c = carry_ref[j]
        keep = (i % tiles_per_seq) != 0
        first1 = jnp.where(keep, c[SUBLANES - 1:SUBLANES, :], 0.0)
        first2 = jnp.where(keep, c[SUBLANES - 2:SUBLANES - 1, :], 0.0)
        carry_ref[j] = u[tm - SUBLANES:, :]
        return jnp.where(row == 0, first1, r1), jnp.where(row == 0, first2, jnp.where(row == 1, first1, r2))
    first1, first2 = _expand_state_rows(states[0], states[1], tm)
    t = row % SUBLANES
    return jnp.where(t == 0, first1, r1), (None if first2 is None else jnp.where(t < 2, first2, r2))


def _expand_state_rows(st1, st0, tm):
    ns, n = st1.shape
    spread = lambda x: jnp.broadcast_to(x[:, None, :], (ns, SUBLANES, n)).reshape(tm, n)
    first1 = spread(st1)
    if st0 is None:
        return first1, None
    t = _row_iota((tm, n)) % SUBLANES
    return first1, jnp.where(t == 0, spread(st0), first1)


def _mix_kernel(*refs, sample, tiles_per_seq):
    if sample:
        (x_ref, ext_ref, g1_ref, mu_ref, w0_ref, a0_ref, dw1_ref, aw1_ref, gw1_ref, dw2_ref,
         aw2_ref, gw2_ref, xnb_ref, lw_ref, a_ref, g_ref, last_ref) = refs
    else:
        (x_ref, g1_ref, mu_ref, w0_ref, a0_ref, dw1_ref, aw1_ref, gw1_ref, dw2_ref,
         aw2_ref, gw2_ref, xnb_ref, lw_ref, a_ref, g_ref, last_ref, carry_ref) = refs
    if not sample:
        _init_carry(carry_ref, 0, pl.program_id(0))
    x = x_ref[...]
    xn = x * lax.rsqrt(jnp.mean(x * x, axis=-1, keepdims=True) + RMS_EPS) * g1_ref[...]
    tm = xn.shape[0]
    if sample:
        prev, _unused = _shifted_rows(xn, None, (ext_ref[...], None))
        last_ref[...] = xn
    else:
        prev, _unused = _shifted_rows(xn, (carry_ref, 0, pl.program_id(0), tiles_per_seq), None)
        last_ref[...] = xn[tm - SUBLANES:, :]
    dx = prev - xn
    xnb_ref[...] = xn.astype(BF16)
    xw = (xn + dx * mu_ref[0:1, :]).astype(BF16)
    hw = jnp.tanh(_dot(xw, dw1_ref[...]))
    wl = w0_ref[...] + _dot(hw.astype(BF16), dw2_ref[...])
    lw_ref[...] = -_sigmoid(wl) * math.exp(-0.5)
    xa = (xn + dx * mu_ref[1:2, :]).astype(BF16)
    ha = _dot(xa, aw1_ref[...])
    a_ref[...] = _sigmoid(a0_ref[...] + _dot(ha.astype(BF16), aw2_ref[...]))
    xg = (xn + dx * mu_ref[2:3, :]).astype(BF16)
    hg = _sigmoid(_dot(xg, gw1_ref[...]))
    g_ref[...] = _dot(hg.astype(BF16), gw2_ref[...])


def _mix_call(x2d, ext, p, *, sample, seq_len, tm):
    rows, d = x2d.shape
    g = p["dw2"].shape[1]
    n_i = rows // tm
    row_blk = lambda w: pl.BlockSpec((tm, w), lambda i: (i, 0))
    full = lambda a: pl.BlockSpec(a.shape, lambda i: (0, 0))
    weights = [p["norm1_g"], p["mu_lora"], p["decay_w0"], p["aaa_a0"], p["dw1"], p["aw1"], p["gw1"],
               p["dw2"], p["aw2"], p["gw2"]]
    state_blk = pl.BlockSpec((tm // seq_len, d), lambda i: (i, 0))
    in_specs = [row_blk(d)] + ([state_blk] if sample else []) + [full(w) for w in weights]
    args = [x2d] + ([ext] if sample else []) + weights
    last_rows = tm if sample else SUBLANES
    out_shape = [jax.ShapeDtypeStruct((rows, d), BF16)] + [jax.ShapeDtypeStruct((rows, g), F32)] * 3 + [
        jax.ShapeDtypeStruct((n_i * last_rows, d), F32)]
    out_specs = [row_blk(d), row_blk(g), row_blk(g), row_blk(g), pl.BlockSpec((last_rows, d), lambda i: (i, 0))]
    return pl.pallas_call(
        functools.partial(_mix_kernel, sample=sample, tiles_per_seq=max(seq_len // tm, 1)),
        grid=(n_i,), in_specs=in_specs, out_specs=out_specs, out_shape=out_shape,
        scratch_shapes=[] if sample else [pltpu.VMEM((1, SUBLANES, d), F32)],
        compiler_params=_params("arbitrary"), name="mix")(*args)


def _rkv_kernel(*refs, sample, tiles_per_seq):
    if sample:
        x_ref, w_ref, mu_ref, shift_ref, o_ref = refs
    else:
        x_ref, w_ref, mu_ref, o_ref, carry_ref = refs
    if not sample:
        _init_carry(carry_ref, pl.program_id(1), pl.program_id(0))
    w = w_ref[...]
    p = _dot(x_ref[...], w)
    if sample:
        prev, _unused = _shifted_rows(p, None, (_dot(shift_ref[...], w), None))
    else:
        prev, _unused = _shifted_rows(p, (carry_ref, pl.program_id(1), pl.program_id(0), tiles_per_seq), None)
    o_ref[...] = p + mu_ref[...] * (prev - p)


def _rkv_call(xnb, w_in, mu, ext, *, sample, seq_len, tm, tn):
    rows, d = xnb.shape
    n = mu.shape[1]
    n_i, n_j = rows // tm, n // tn
    in_specs = [pl.BlockSpec((tm, d), lambda i, j: (i, 0)), pl.BlockSpec((d, tn), lambda i, j: (0, j)),
                pl.BlockSpec((1, tn), lambda i, j: (0, j))]
    args = [xnb, w_in, mu]
    if sample:
        in_specs.append(pl.BlockSpec((tm // seq_len, d), lambda i, j: (i, 0)))
        args.append(ext)
    return pl.pallas_call(
        functools.partial(_rkv_kernel, sample=sample, tiles_per_seq=max(seq_len // tm, 1)),
        grid=(n_i, n_j), in_specs=in_specs, out_specs=pl.BlockSpec((tm, tn), lambda i, j: (i, j)),
        out_shape=jax.ShapeDtypeStruct((rows, n), F32),
        scratch_shapes=[] if sample else [pltpu.VMEM((n_j, SUBLANES, tn), F32)],
        compiler_params=_params("arbitrary", "arbitrary"), name="rkv")(*args)


def _convbr_kernel(*refs, sample, tiles_per_seq):
    if sample:
        x_ref, wb_ref, wc_ref, wx_ref, cw_ref, st1_ref, st0_ref, y_ref, last_ref = refs
    else:
        x_ref, wb_ref, wc_ref, wx_ref, cw_ref, y_ref, last_ref, carry_ref = refs
    if not sample:
        _init_carry(carry_ref, pl.program_id(1), pl.program_id(0))
    x = x_ref[...]
    cx = _dot(x, wc_ref[...]) * _dot(x, wx_ref[...])
    tm = cx.shape[0]
    if sample:
        prev1, prev2 = _shifted_rows(cx, None, (st1_ref[...], st0_ref[...]))
        last_ref[...] = cx
    else:
        prev1, prev2 = _shifted_rows(cx, (carry_ref, pl.program_id(1), pl.program_id(0), tiles_per_seq), None)
        last_ref[...] = cx[tm - SUBLANES:, :]
    hconv = prev2 * cw_ref[0:1, :] + prev1 * cw_ref[1:2, :] + cx * cw_ref[2:3, :]
    y_ref[...] = _dot(x, wb_ref[...]) * hconv


def _convbr_call(xnb, w_in, conv_w, e1, e2, *, col0, sample, seq_len, tm, tn):
    rows, d = xnb.shape
    gc = conv_w.shape[1]
    n_i, n_j = rows // tm, gc // tn
    off = col0 // tn
    nb = gc // tn
    wspec = lambda k: pl.BlockSpec((d, tn), lambda i, j: (0, off + k * nb + j))
    tile = pl.BlockSpec((tm, tn), lambda i, j: (i, j))
    in_specs = [pl.BlockSpec((tm, d), lambda i, j: (i, 0)), wspec(0), wspec(1), wspec(2),
                pl.BlockSpec((3, tn), lambda i, j: (0, j))]
    args = [xnb, w_in, w_in, w_in, conv_w]
    if sample:
        sspec = pl.BlockSpec((tm // seq_len, tn), lambda i, j: (i, j))
        in_specs += [sspec, sspec]
        args += [e1, e2]
    last_rows = tm if sample else SUBLANES
    return pl.pallas_call(
        functools.partial(_convbr_kernel, sample=sample, tiles_per_seq=max(seq_len // tm, 1)),
        grid=(n_i, n_j), in_specs=in_specs,
        out_specs=[tile, pl.BlockSpec((last_rows, tn), lambda i, j: (i, j))],
        out_shape=[jax.ShapeDtypeStruct((rows, gc), F32), jax.ShapeDtypeStruct((n_i * last_rows, gc), F32)],
        scratch_shapes=[] if sample else [pltpu.VMEM((n_j, SUBLANES, tn), F32)],
        compiler_params=_params("arbitrary", "arbitrary"), name="convbr")(*args)


def _wkv_masks(c):
    m = 2 * WKV_SUB
    r2 = jnp.bitwise_and(_row_iota((m, m)), WKV_SUB - 1)
    c2 = jnp.bitwise_and(lax.broadcasted_iota(jnp.int32, (m, m), 1), WKV_SUB - 1)
    shift = int(math.log2(c))
    same = jnp.right_shift(r2, shift) == jnp.right_shift(c2, shift)
    eye = (_row_iota((m, m)) == lax.broadcasted_iota(jnp.int32, (m, m), 1)).astype(F32)
    lane_lo = lax.broadcasted_iota(jnp.int32, (WKV_SUB, LANES), 1) < HEAD_DIM
    return lane_lo, same & (c2 < r2), same & (c2 <= r2), eye


def _stack_heads(x, lane_lo):
    z = jnp.zeros_like(x)
    return jnp.concatenate([jnp.where(lane_lo, x, z), jnp.where(lane_lo, z, x)], axis=0)


def _fold_heads(x):
    return x[:WKV_SUB] + x[WKV_SUB:]


def _wkv_phase1(units, masks, c):
    lane_lo, strict, incl, eye = masks
    m = 2 * WKV_SUB
    phs = []
    for r, kt, v, kp, bt, lg, lw in units:
        e1 = jnp.exp(lg)
        e0 = jnp.exp(lg - lw)
        ei = jnp.exp(-lg)
        phs.append(dict(
            e1=e1, rt_st=_stack_heads(r * e1, lane_lo), kp_b=_stack_heads(kp * e0, lane_lo).astype(BF16),
            kh_st=_stack_heads(kt * ei, lane_lo), bh_st=_stack_heads(bt * ei, lane_lo),
            v_st=_stack_heads(v, lane_lo)))
    for ph in phs:
        ph["kh_b"], ph["bh_b"], ph["v_b"] = (ph[n].astype(BF16) for n in ("kh_st", "bh_st", "v_st"))
    gs = [_dot_nt(jnp.concatenate([ph["kp_b"], ph["rt_st"].astype(BF16)], axis=0),
                  jnp.concatenate([ph["bh_b"], ph["kh_b"]], axis=0)) for ph in phs]
    ps = [-jnp.where(strict, g[:m, :m], 0.0) for g in gs]
    ts = [eye + p for p in ps]
    n = 1
    while 2 * n < c:
        pbs = [p.astype(BF16) for p in ps]
        ps = [_dot(pb, pb) for pb in pbs]
        ts = [t + _dot(t.astype(BF16), p.astype(BF16)) for t, p in zip(ts, ps)]
        n *= 2
    abvs = [_dot(jnp.concatenate([jnp.where(strict, g[:m, m:], 0.0), jnp.where(incl, g[m:, m:], 0.0)],
                                 axis=0).astype(BF16), ph["v_b"]) for g, ph in zip(gs, phs)]
    tts = [_dot(t.astype(BF16), jnp.concatenate([ph["kp_b"], abv[:m].astype(BF16)], axis=1))
           for t, ph, abv in zip(ts, phs, abvs)]
    bbtts = [_dot(jnp.where(incl, g[m:, :m], 0.0).astype(BF16), tt.astype(BF16)) for g, tt in zip(gs, tts)]
    for ph, abv, tt, bbtt in zip(phs, abvs, tts, bbtts):
        ph["rq"] = _fold_heads(ph["rt_st"] - bbtt[:, :LANES])
        ph["ov"] = _fold_heads(abv[m:] - bbtt[:, LANES:])
        ph["tk_st"], ph["tav_st"] = tt[:, :LANES], tt[:, LANES:]
    return phs


def _wkv_transitions(phs):
    gams = [ph["e1"][WKV_SUB - 1:WKV_SUB, :] for ph in phs]
    kgs = [(_dot_tn(ph["tk_st"].astype(BF16), ph["bh_b"]) * gam).astype(BF16) for ph, gam in zip(phs, gams)]
    bcgs = [_dot_tn(jnp.concatenate([ph["v_b"], (-ph["tav_st"]).astype(BF16)], axis=0),
                    jnp.concatenate([ph["kh_b"], ph["bh_b"]], axis=0)) * gam for ph, gam in zip(phs, gams)]
    return gams, kgs, bcgs


def _wkv_units_small(states, units, c, lane_lo_c):
    rows = lambda u: slice(u * c, (u + 1) * c)
    rows_hi = lambda u: slice(WKV_SUB + u * c, WKV_SUB + (u + 1) * c)
    ous = [_dot_nt(jnp.concatenate([ph["rq"][rows(u)], ph["tkm"][rows(u)]], axis=0).astype(BF16), s.astype(BF16))
           for s, (ph, u) in zip(states, units)]
    outs, new_states = [], []
    for s, (ph, u), ou in zip(states, units, ous):
        pick = lambda x: jnp.concatenate([x[rows(u)], x[rows_hi(u)]], axis=0)
        uu = ou[c:] + ph["tav"][rows(u)]
        z = jnp.zeros_like(uu)
        u_st = jnp.concatenate([jnp.where(lane_lo_c, uu, z), jnp.where(lane_lo_c, z, uu)], axis=0)
        gam = ph["e1"][(u + 1) * c - 1:(u + 1) * c, :]
        lhs = jnp.concatenate([pick(ph["v_st"]), -u_st], axis=0).astype(BF16)
        rhs = (jnp.concatenate([pick(ph["kh_st"]), pick(ph["bh_st"])], axis=0) * gam).astype(BF16)
        outs.append(ou[:c] + ph["ov"][rows(u)])
        new_states.append(s * gam + _dot_tn(lhs, rhs))
    return outs, new_states


def _wkv_kernel(*refs, sample, chunk, n_pairs, n_sub):
    if sample:
        (r_ref, k_ref, v_ref, lw_ref, a_ref, g_ref, kk_ref, ka_ref, rk_ref, lnw_ref, lnb_ref, sin_ref,
         o_ref, sout_ref) = refs
    else:
        (r_ref, k_ref, v_ref, lw_ref, a_ref, g_ref, kk_ref, ka_ref, rk_ref, lnw_ref, lnb_ref,
         o_ref, sout_ref, s_s) = refs
    c = chunk
    lane128 = lax.broadcasted_iota(jnp.int32, (LANES, LANES), 1)
    row128 = _row_iota((LANES, LANES))
    blockdiag = (row128 < HEAD_DIM) == (lane128 < HEAD_DIM)
    block_ones = blockdiag.astype(BF16)
    seg_sum = lambda z: _split_dot(z, block_ones, 2)
    to_blockdiag = lambda x: jnp.where(blockdiag, jnp.concatenate([x, x], axis=1), 0.0)
    from_blockdiag = lambda s_: s_[:, :HEAD_DIM] + s_[:, HEAD_DIM:]
    masks = _wkv_masks(c)
    r2 = jnp.bitwise_and(_row_iota((WKV_SUB, WKV_SUB)), WKV_SUB - 1)
    c2 = lax.broadcasted_iota(jnp.int32, (WKV_SUB, WKV_SUB), 1)
    shift = int(math.log2(c))
    tril_b = ((jnp.right_shift(r2, shift) == jnp.right_shift(c2, shift)) & (c2 <= r2)).astype(BF16)
    lane_lo_c = lax.broadcasted_iota(jnp.int32, (c, LANES), 1) < HEAD_DIM

    if not sample:
        @pl.when(pl.program_id(2) == 0)
        def _():
            s_s[...] = jnp.zeros_like(s_s)

    lgs = []
    for sb in range(n_sub):
        rows = slice(sb * WKV_SUB, (sb + 1) * WKV_SUB)
        lgs.append(_split_dot(lw_ref[rows, :], tril_b, 3, dot=lambda x, b: _dot(b, x)))

    pairs = range(n_pairs)
    lanes = [slice(q * LANES, (q + 1) * LANES) for q in pairs]
    r = [r_ref[:, l] for l in lanes]
    k = [k_ref[:, l] for l in lanes]
    v = [v_ref[:, l] for l in lanes]
    a = [a_ref[:, l] for l in lanes]
    lw = [lw_ref[:, l] for l in lanes]
    kk = [k[q] * kk_ref[:, lanes[q]] for q in pairs]
    norms = [seg_sum(x * x) for x in kk]
    kk = [x / jnp.maximum(jnp.sqrt(n2), 1e-12) for x, n2 in zip(kk, norms)]
    kmod = [k[q] * (1.0 + (a[q] - 1.0) * ka_ref[:, lanes[q]]) for q in pairs]
    bt = [kk[q] * a[q] for q in pairs]

    subs = [(q, sb) for sb in range(n_sub) for q in pairs]
    units = []
    for q, sb in subs:
        rows = slice(sb * WKV_SUB, (sb + 1) * WKV_SUB)
        units.append((r[q][rows], kmod[q][rows], v[q][rows], kk[q][rows], bt[q][rows],
                      lgs[sb][:, lanes[q]], lw[q][rows]))
    phs = dict(zip(subs, _wkv_phase1(units, masks, c)))

    outs = {q: [] for q in pairs}
    if sample:
        per_sub = WKV_SUB // c
        todo = []
        for (q, sb), ph in phs.items():
            ph["tkm"] = _fold_heads(ph["tk_st"])
            ph["tav"] = _fold_heads(ph["tav_st"])
            todo += [(q, sb * per_sub + u, ph, u) for u in range(per_sub)]
        o_units, new_states = _wkv_units_small([to_blockdiag(sin_ref[seq, q]) for q, seq, _, _ in todo],
                                               [(ph, u) for _, _, ph, u in todo], c, lane_lo_c)
        for (q, seq, _, _), o, s_new in zip(todo, o_units, new_states):
            sout_ref[seq, q] = from_blockdiag(s_new)
            outs[q].append(o)
    else:
        gams, kgs, bcgs = _wkv_transitions([phs[key] for key in subs])
        trans = dict(zip(subs, zip(gams, kgs, bcgs)))
        s = [s_s[q] for q in pairs]
        for sb in range(n_sub):
            sb16 = [x.astype(BF16) for x in s]
            for q in pairs:
                outs[q].append(_dot_nt(phs[q, sb]["rq"].astype(BF16), sb16[q]) + phs[q, sb]["ov"])
            s = [s[q] * trans[q, sb][0] - _dot(sb16[q], trans[q, sb][1]) + trans[q, sb][2] for q in pairs]
        for q in pairs:
            s_s[q] = s[q]

        @pl.when(pl.program_id(2) == pl.num_programs(2) - 1)
        def _():
            for q in pairs:
                sout_ref[0, q] = from_blockdiag(s[q])

    o = [jnp.concatenate(outs[q], axis=0) for q in pairs]
    inv_n = 1.0 / HEAD_DIM
    mu = [_split_dot(x, block_ones, 1) * inv_n for x in o]
    dev = [x - m_ for x, m_ in zip(o, mu)]
    var = [_split_dot(x * x, block_ones, 1) * inv_n for x in dev]
    bonus = [seg_sum(r[q] * kmod[q] * rk_ref[:, lanes[q]]) * v[q] for q in pairs]
    for q in pairs:
        on = dev[q] * lax.rsqrt(var[q] + GN_EPS) * lnw_ref[:, lanes[q]] + lnb_ref[:, lanes[q]]
        o_ref[:, lanes[q]] = ((on + bonus[q]) * g_ref[:, lanes[q]]).astype(BF16)


def _wkv_call(rkv, lw, a, g, p, s_bd, *, sample, seq_len, t_blk, chunk, n_pairs):
    rows, gdim = lw.shape
    npair = gdim // LANES
    nseq = rows // seq_len
    width = n_pairs * LANES
    pair_blocks = npair // n_pairs
    head = [p["k_k"], p["k_a"], p["r_k"], p["ln_x_w"], p["ln_x_b"]]
    if sample:
        seq_blk = t_blk // seq_len
        grid = (nseq // seq_blk, pair_blocks)
        blk = lambda off: pl.BlockSpec((t_blk, width), lambda b, q, off=off: (b, off + q))
        hspec = pl.BlockSpec((1, width), lambda b, q: (0, q))
        sspec = pl.BlockSpec((seq_blk, n_pairs, LANES, HEAD_DIM), lambda b, q: (b, q, 0, 0))
        extra_specs, extra_args = [sspec], [s_bd]
        sem = ("arbitrary", "arbitrary")
        scratch = []
    else:
        nblk = seq_len // t_blk
        grid = (nseq, pair_blocks, nblk)
        blk = lambda off: pl.BlockSpec((t_blk, width), lambda b, q, n, off=off: (b * nblk + n, off + q))
        hspec = pl.BlockSpec((1, width), lambda b, q, n: (0, q))
        sspec = pl.BlockSpec((1, n_pairs, LANES, HEAD_DIM), lambda b, q, n: (b, q, 0, 0))
        extra_specs, extra_args = [], []
        sem = ("arbitrary", "arbitrary", "arbitrary")
        scratch = [pltpu.VMEM((n_pairs, LANES, LANES), F32)]
    in_specs = [blk(0), blk(pair_blocks), blk(2 * pair_blocks), blk(0), blk(0), blk(0)] + [hspec] * 5 + extra_specs
    args = [rkv, rkv, rkv, lw, a, g] + head + extra_args
    return pl.pallas_call(
        functools.partial(_wkv_kernel, sample=sample, chunk=chunk, n_pairs=n_pairs, n_sub=t_blk // WKV_SUB),
        grid=grid, in_specs=in_specs, out_specs=[blk(0), sspec],
        out_shape=[jax.ShapeDtypeStruct((rows, gdim), BF16),
                   jax.ShapeDtypeStruct((nseq, npair, LANES, HEAD_DIM), F32)],
        scratch_shapes=scratch, compiler_params=_params(*sem), name="wkv")(*args)


def _outproj_kernel(orw_ref, ocv_ref, w_ref, x_ref, o_ref):
    lhs = jnp.concatenate([orw_ref[...], ocv_ref[...]], axis=1)
    o_ref[...] = x_ref[...] + _dot(lhs, w_ref[...])


def _outproj_call(orw, ocv, w_out, x2d, *, tm, tn):
    rows, d = x2d.shape
    g, gc = orw.shape[1], ocv.shape[1]
    tile = pl.BlockSpec((tm, tn), lambda i, j: (i, j))
    return pl.pallas_call(
        _outproj_kernel, grid=(rows // tm, d // tn),
        in_specs=[pl.BlockSpec((tm, g), lambda i, j: (i, 0)), pl.BlockSpec((tm, gc), lambda i, j: (i, 0)),
                  pl.BlockSpec((d, tn), lambda i, j: (0, j)), tile],
        out_specs=tile, out_shape=jax.ShapeDtypeStruct((rows, d), F32),
        compiler_params=_params("arbitrary", "arbitrary"), name="outproj")(orw, ocv, w_out, x2d)


def _norm_kernel(x_ref, g_ref, o_ref):
    x = x_ref[...]
    y = x * lax.rsqrt(jnp.mean(x * x, axis=-1, keepdims=True) + RMS_EPS) * g_ref[...]
    o_ref[...] = y.astype(o_ref.dtype)


def _norm_call(x2d, g, dtype, *, tm):
    rows, d = x2d.shape
    blk = pl.BlockSpec((tm, d), lambda i: (i, 0))
    return pl.pallas_call(
        _norm_kernel, grid=(rows // tm,), in_specs=[blk, pl.BlockSpec((1, d), lambda i: (0, 0))],
        out_specs=blk, out_shape=jax.ShapeDtypeStruct((rows, d), dtype),
        compiler_params=_params("arbitrary"), name="rmsnorm")(x2d, g)


def _ffnup_kernel(*refs, sample, tiles_per_seq):
    if sample:
        x_ref, w1_ref, w3_ref, cw_ref, cb_ref, st1_ref, st0_ref, h_ref, last_ref = refs
        carry = None
    else:
        x_ref, w1_ref, w3_ref, cw_ref, cb_ref, h_ref, last_ref, carry_ref = refs
        carry = (carry_ref, pl.program_id(1), pl.program_id(0), tiles_per_seq)
        _init_carry(*carry[:3])
    tn = h_ref.shape[1]
    w13 = jnp.concatenate([w1_ref[...].astype(BF16), w3_ref[...].astype(BF16)], axis=1)
    uw = _dot(x_ref[...], w13)
    u = uw[:, :tn]
    tm = u.shape[0]
    states = (st1_ref[...], st0_ref[...]) if sample else None
    prev1, prev2 = _shifted_rows(u, carry, states)
    if sample:
        last_ref[...] = u.reshape(tm // SUBLANES, SUBLANES, tn)[:, SUBLANES - 2:, :]
    else:
        last_ref[...] = u[tm - SUBLANES:, :]
    z = prev2 * cw_ref[0:1, :] + prev1 * cw_ref[1:2, :] + u * cw_ref[2:3, :] + cb_ref[...]
    hz = 0.5 * z
    h_ref[...] = ((hz + hz * jnp.tanh(hz)) * uw[:, tn:]).astype(BF16)


def _ffnup_call(hn, w1, w3, conv_w, conv_b, st1, st0, *, sample, seq_len, tm, tn):
    rows, d = hn.shape
    dff = conv_w.shape[1]
    n_i, n_j = rows // tm, dff // tn
    tile = pl.BlockSpec((tm, tn), lambda i, j: (i, j))
    wspec = pl.BlockSpec((d, tn), lambda i, j: (0, j))
    in_specs = [pl.BlockSpec((tm, d), lambda i, j: (i, 0)), wspec, wspec,
                pl.BlockSpec((3, tn), lambda i, j: (0, j)), pl.BlockSpec((1, tn), lambda i, j: (0, j))]
    args = [hn, w1, w3, conv_w, conv_b]
    if sample:
        sspec = pl.BlockSpec((tm // seq_len, tn), lambda i, j: (i, j))
        in_specs += [sspec, sspec]
        args += [st1, st0]
    if sample:
        last_spec = pl.BlockSpec((tm // seq_len, 2, tn), lambda i, j: (i, 0, j))
        last_shape = jax.ShapeDtypeStruct((rows // seq_len, 2, dff), F32)
    else:
        last_spec = pl.BlockSpec((SUBLANES, tn), lambda i, j: (i, j))
        last_shape = jax.ShapeDtypeStruct((n_i * SUBLANES, dff), F32)
    return pl.pallas_call(
        functools.partial(_ffnup_kernel, sample=sample, tiles_per_seq=max(seq_len // tm, 1)),
        grid=(n_i, n_j), in_specs=in_specs,
        out_specs=[tile, last_spec], out_shape=[jax.ShapeDtypeStruct((rows, dff), BF16), last_shape],
        scratch_shapes=[] if sample else [pltpu.VMEM((n_j, SUBLANES, tn), F32)],
        compiler_params=_params("arbitrary", "arbitrary"), name="ffnup")(*args)


def _ffndown_kernel(h_ref, w_ref, x_ref, o_ref):
    o_ref[...] = x_ref[...] + _dot(h_ref[...], w_ref[...])


def _ffndown_call(h, w2, x1, *, tm, tn):
    rows, dff = h.shape
    d = w2.shape[1]
    tile = pl.BlockSpec((tm, tn), lambda i, j: (i, j))
    return pl.pallas_call(
        _ffndown_kernel, grid=(rows // tm, d // tn),
        in_specs=[pl.BlockSpec((tm, dff), lambda i, j: (i, 0)), pl.BlockSpec((dff, tn), lambda i, j: (0, j)), tile],
        out_specs=tile, out_shape=jax.ShapeDtypeStruct((rows, d), F32),
        compiler_params=_params("arbitrary", "arbitrary"), name="ffndown")(h, w2, x1)


def _tile(n, want):
    t = min(n, want)
    while n % t or (t % SUBLANES and t != n):
        t -= 1
    return t


def _col_tile(n, want):
    t = min(n, want)
    while n % t or t % LANES:
        t -= LANES
    return t


def _layer(x, states, p, *, sample):
    nseq, seq_len, d = x.shape
    rows = nseq * seq_len
    g = p["dw2"].shape[1]
    gc = d - g
    dff = p["ffn_conv_w"].shape[1]
    npair = g // LANES
    x2d = x.reshape(rows, d)
    big = dict(sample=sample, seq_len=seq_len)
    tm_big = rows if sample else _tile(seq_len, 1024)

    if sample:
        assert seq_len == SUBLANES, "the sample path shifts rows inside 8-row groups"
        shift, wkv, conv, ffn = states
        ext_x = shift
        ext_p = shift.astype(BF16)
        ce1, ce2 = conv[:, 1], conv[:, 0]
        fe1, fe2 = ffn[:, 1], ffn[:, 0]
        s_bd = wkv.reshape(nseq, npair, LANES, HEAD_DIM)
    else:
        ext_x = ext_p = ce1 = ce2 = fe1 = fe2 = s_bd = None

    tm_mix = _tile(rows if sample else seq_len, 256)
    xnb, lw, a, gate, xlast = _mix_call(x2d, ext_x, p, sample=sample, seq_len=seq_len, tm=tm_mix)
    rkv = _rkv_call(xnb, p["w_in"], p["mu_rkv"], ext_p, tm=tm_big, tn=_col_tile(3 * g, 1024), **big)
    tm_conv = _tile(rows if sample else seq_len, 1024)
    ycv, cxlast = _convbr_call(xnb, p["w_in"], p["conv_w"], ce1, ce2, col0=3 * g, tm=tm_conv,
                               tn=_col_tile(gc, 512), **big)
    if sample:
        t_blk, chunk = WKV_SUB, seq_len
    else:
        chunk = WKV_SUB
        t_blk = _tile(seq_len, 4 * WKV_SUB)
    orw, s_new = _wkv_call(rkv, lw, a, gate, p, s_bd, t_blk=t_blk, chunk=chunk,
                           n_pairs=math.gcd(npair, 8), **big)
    tm_e = _tile(rows, 512)
    ocv = _norm_call(ycv, p["conv_norm_g"], BF16, tm=_tile(rows, 512))
    x1 = _outproj_call(orw, ocv, p["w_out"], x2d, tm=_tile(rows, 1024), tn=_col_tile(d, 1024))
    hn = _norm_call(x1, p["norm2_g"], BF16, tm=_tile(rows, 512))
    tm_ffn = rows if sample else _tile(seq_len, 2048)
    h, ulast = _ffnup_call(hn, p["ffn_w1"], p["ffn_w3"], p["ffn_conv_w"], p["ffn_conv_b"], fe1, fe2,
                           tm=tm_ffn, tn=_col_tile(dff, 256), **big)
    x2 = _ffndown_call(h, p["ffn_w2"], x1, tm=tm_e, tn=_col_tile(d, 512))

    def last_rows(arr, tile_rows, k):
        if sample:
            return arr.reshape(nseq, seq_len, -1)[:, seq_len - k:]
        per_seq = seq_len // tile_rows
        return arr.reshape(nseq, per_seq, SUBLANES, -1)[:, -1, SUBLANES - k:]
    new_shift = last_rows(xlast, tm_mix, 1)[:, 0]
    new_conv = last_rows(cxlast, tm_conv, 2)
    new_ffn = ulast if sample else last_rows(ulast, tm_ffn, 2)
    new_wkv = s_new.reshape(nseq, 2 * npair, HEAD_DIM, HEAD_DIM)
    return x2.reshape(nseq, seq_len, d), new_shift, new_wkv, new_conv, new_ffn


def _pad_to(a, axis, mult):
    pad = (-a.shape[axis]) % mult
    if not pad:
        return a
    widths = [(0, 0)] * a.ndim
    widths[axis] = (0, pad)
    return jnp.pad(a, widths)


def kernel(x_prompt, x_sample, state_shift, state_wkv, state_conv, state_ffn, norm1_g, w_in, mu_rkv, mu_lora, decay_w0, decay_w1, decay_w2, aaa_a0, aaa_a1, aaa_a2, gate_g1, gate_g2, k_k, k_a, r_k, ln_x_w, ln_x_b, conv_w, conv_norm_g, w_out, norm2_g, ffn_w1, ffn_conv_w, ffn_conv_b, ffn_w3, ffn_w2, final_norm_g):
    depth = w_in.shape[0]
    d = x_prompt.shape[-1]
    row = lambda v: v.reshape(1, -1).astype(F32)
    yp, ys = x_prompt, x_sample
    outs_p, outs_s = [], []
    for l in range(depth):
        p = dict(
            norm1_g=row(norm1_g[l]), w_in=w_in[l].astype(BF16), mu_rkv=row(mu_rkv[l]), mu_lora=mu_lora[l],
            decay_w0=row(decay_w0[l]), aaa_a0=row(aaa_a0[l]),
            dw1=decay_w1[l].astype(BF16), dw2=decay_w2[l].astype(BF16),
            aw1=aaa_a1[l].astype(BF16), aw2=aaa_a2[l].astype(BF16),
            gw1=_pad_to(gate_g1[l], 1, LANES).astype(BF16), gw2=_pad_to(gate_g2[l], 0, LANES).astype(BF16),
            k_k=row(k_k[l]), k_a=row(k_a[l]), r_k=row(r_k[l]), ln_x_w=row(ln_x_w[l]), ln_x_b=row(ln_x_b[l]),
            conv_w=conv_w[l], conv_norm_g=row(conv_norm_g[l]), w_out=w_out[l].astype(BF16),
            norm2_g=row(norm2_g[l]), ffn_w1=ffn_w1[l], ffn_w3=ffn_w3[l],
            ffn_conv_w=ffn_conv_w[l], ffn_conv_b=row(ffn_conv_b[l]), ffn_w2=ffn_w2[l].astype(BF16))
        yp, *st_p = _layer(yp, None, p, sample=False)
        ys, *st_s = _layer(ys, (state_shift[l], state_wkv[l], state_conv[l], state_ffn[l]), p, sample=True)
        outs_p.append(st_p)
        outs_s.append(st_s)
    fin = row(final_norm_g)
    y_prompt = _norm_call(yp.reshape(-1, d), fin, F32, tm=_tile(yp.shape[0] * yp.shape[1], 512)).reshape(yp.shape)
    y_sample = _norm_call(ys.reshape(-1, d), fin, F32, tm=_tile(ys.shape[0] * ys.shape[1], 512)).reshape(ys.shape)
    stack = lambda outs, k: jnp.stack([o[k] for o in outs])
    return (y_prompt, y_sample,
            stack(outs_p, 0), stack(outs_p, 1), stack(outs_p, 2), stack(outs_p, 3),
            stack(outs_s, 0), stack(outs_s, 1), stack(outs_s, 2), stack(outs_s, 3))
```

```python
import functools
import math

import jax
import jax.numpy as jnp
from jax import lax
from jax.experimental import pallas as pl
from jax.experimental.pallas import tpu as pltpu

F32 = jnp.float32
BF16 = jnp.bfloat16

HEAD_DIM = 64
LANES = 128
SUBLANES = 8
WKV_SUB = 64
RMS_EPS = 1e-6
GN_EPS = 64e-5
VMEM_LIMIT_BYTES = 60 * 1024 * 1024


def _params(*sem):
    return pltpu.CompilerParams(dimension_semantics=sem, vmem_limit_bytes=VMEM_LIMIT_BYTES)


def _dot(a, b):
    return jnp.dot(a, b, preferred_element_type=F32)


def _dot_nt(a, b):
    return lax.dot_general(a, b, (((1,), (1,)), ((), ())), preferred_element_type=F32)


def _dot_tn(a, b):
    return lax.dot_general(a, b, (((0,), (0,)), ((), ())), preferred_element_type=F32)


def _sigmoid(z):
    return 1.0 / (1.0 + jnp.exp(-z))


def _split_dot(x, b_exact, terms, dot=_dot):
    acc = None
    rem = x
    for _ in range(terms):
        hi = rem.astype(BF16)
        part = dot(hi, b_exact)
        acc = part if acc is None else acc + part
        rem = rem - hi.astype(F32)
    return acc


def _row_iota(shape):
    return lax.broadcasted_iota(jnp.int32, shape, 0)


def _init_carry(carry_ref, j, i):
    @pl.when(i == 0)
    def _():
        carry_ref[j] = jnp.zeros(carry_ref.shape[1:], F32)


def _shifted_rows(u, carry, states):
    tm = u.shape[0]
    row = _row_iota(u.shape)
    r1 = pltpu.roll(u, 1, 0)
    r2 = pltpu.roll(u, 2, 0)
    if carry is not None:
        carry_ref, j, i, tiles_per_seq = carry
        c = carry_ref[j]
        keep = (i % tiles_per_seq) != 0
        first1 = jnp.where(keep, c[SUBLANES - 1:SUBLANES, :], 0.0)
        first2 = jnp.where(keep, c[SUBLANES - 2:SUBLANES - 1, :], 0.0)
        carry_ref[j] = u[tm - SUBLANES:, :]
        return jnp.where(row == 0, first1, r1), jnp.where(row == 0, first2, jnp.where(row == 1, first1, r2))
    first1, first2 = _expand_state_rows(states[0], states[1], tm)
    t = row % SUBLANES
    return jnp.where(t == 0, first1, r1), (None if first2 is None else jnp.where(t < 2, first2, r2))


def _expand_state_rows(st1, st0, tm):
    ns, n = st1.shape
    spread = lambda x: jnp.broadcast_to(x[:, None, :], (ns, SUBLANES, n)).reshape(tm, n)
    first1 = spread(st1)
    if st0 is None:
        return first1, None
    t = _row_iota((tm, n)) % SUBLANES
    return first1, jnp.where(t == 0, spread(st0), first1)


def _mix_kernel(*refs, sample, tiles_per_seq):
    if sample:
        (x_ref, ext_ref, g1_ref, mu_ref, w0_ref, a0_ref, dw1_ref, aw1_ref, gw1_ref, dw2_ref,
         aw2_ref, gw2_ref, xnb_ref, lw_ref, a_ref, g_ref, last_ref) = refs
    else:
        (x_ref, g1_ref, mu_ref, w0_ref, a0_ref, dw1_ref, aw1_ref, gw1_ref, dw2_ref,
         aw2_ref, gw2_ref, xnb_ref, lw_ref, a_ref, g_ref, last_ref, carry_ref) = refs
    if not sample:
        _init_carry(carry_ref, 0, pl.program_id(0))
    x = x_ref[...]
    xn = x * lax.rsqrt(jnp.mean(x * x, axis=-1, keepdims=True) + RMS_EPS) * g1_ref[...]
    tm = xn.shape[0]
    if sample:
        prev, _unused = _shifted_rows(xn, None, (ext_ref[...], None))
        last_ref[...] = xn
    else:
        prev, _unused = _shifted_rows(xn, (carry_ref, 0, pl.program_id(0), tiles_per_seq), None)
        last_ref[...] = xn[tm - SUBLANES:, :]
    dx = prev - xn
    xnb_ref[...] = xn.astype(BF16)
    xw = (xn + dx * mu_ref[0:1, :]).astype(BF16)
    hw = jnp.tanh(_dot(xw, dw1_ref[...]))
    wl = w0_ref[...] + _dot(hw.astype(BF16), dw2_ref[...])
    lw_ref[...] = -_sigmoid(wl) * math.exp(-0.5)
    xa = (xn + dx * mu_ref[1:2, :]).astype(BF16)
    ha = _dot(xa, aw1_ref[...])
    a_ref[...] = _sigmoid(a0_ref[...] + _dot(ha.astype(BF16), aw2_ref[...]))
    xg = (xn + dx * mu_ref[2:3, :]).astype(BF16)
    hg = _sigmoid(_dot(xg, gw1_ref[...]))
    g_ref[...] = _dot(hg.astype(BF16), gw2_ref[...])


def _mix_call(x2d, ext, p, *, sample, seq_len, tm):
    rows, d = x2d.shape
    g = p["dw2"].shape[1]
    n_i = rows // tm
    row_blk = lambda w: pl.BlockSpec((tm, w), lambda i: (i, 0))
    full = lambda a: pl.BlockSpec(a.shape, lambda i: (0, 0))
    weights = [p["norm1_g"], p["mu_lora"], p["decay_w0"], p["aaa_a0"], p["dw1"], p["aw1"], p["gw1"],
               p["dw2"], p["aw2"], p["gw2"]]
    state_blk = pl.BlockSpec((tm // seq_len, d), lambda i: (i, 0))
    in_specs = [row_blk(d)] + ([state_blk] if sample else []) + [full(w) for w in weights]
    args = [x2d] + ([ext] if sample else []) + weights
    last_rows = tm if sample else SUBLANES
    out_shape = [jax.ShapeDtypeStruct((rows, d), BF16)] + [jax.ShapeDtypeStruct((rows, g), F32)] * 3 + [
        jax.ShapeDtypeStruct((n_i * last_rows, d), F32)]
    out_specs = [row_blk(d), row_blk(g), row_blk(g), row_blk(g), pl.BlockSpec((last_rows, d), lambda i: (i, 0))]
    return pl.pallas_call(
        functools.partial(_mix_kernel, sample=sample, tiles_per_seq=max(seq_len // tm, 1)),
        grid=(n_i,), in_specs=in_specs, out_specs=out_specs, out_shape=out_shape,
        scratch_shapes=[] if sample else [pltpu.VMEM((1, SUBLANES, d), F32)],
        compiler_params=_params("arbitrary"), name="mix")(*args)


def _rkv_kernel(*refs, sample, tiles_per_seq):
    if sample:
        x_ref, w_ref, mu_ref, shift_ref, o_ref = refs
    else:
        x_ref, w_ref, mu_ref, o_ref, carry_ref = refs
    if not sample:
        _init_carry(carry_ref, pl.program_id(1), pl.program_id(0))
    w = w_ref[...]
    p = _dot(x_ref[...], w)
    if sample:
        prev, _unused = _shifted_rows(p, None, (_dot(shift_ref[...], w), None))
    else:
        prev, _unused = _shifted_rows(p, (carry_ref, pl.program_id(1), pl.program_id(0), tiles_per_seq), None)
    o_ref[...] = p + mu_ref[...] * (prev - p)


def _rkv_call(xnb, w_in, mu, ext, *, sample, seq_len, tm, tn):
    rows, d = xnb.shape
    n = mu.shape[1]
    n_i, n_j = rows // tm, n // tn
    in_specs = [pl.BlockSpec((tm, d), lambda i, j: (i, 0)), pl.BlockSpec((d, tn), lambda i, j: (0, j)),
                pl.BlockSpec((1, tn), lambda i, j: (0, j))]
    args = [xnb, w_in, mu]
    if sample:
        in_specs.append(pl.BlockSpec((tm // seq_len, d), lambda i, j: (i, 0)))
        args.append(ext)
    return pl.pallas_call(
        functools.partial(_rkv_kernel, sample=sample, tiles_per_seq=max(seq_len // tm, 1)),
        grid=(n_i, n_j), in_specs=in_specs, out_specs=pl.BlockSpec((tm, tn), lambda i, j: (i, j)),
        out_shape=jax.ShapeDtypeStruct((rows, n), F32),
        scratch_shapes=[] if sample else [pltpu.VMEM((n_j, SUBLANES, tn), F32)],
        compiler_params=_params("arbitrary", "arbitrary"), name="rkv")(*args)


def _convbr_kernel(*refs, sample, tiles_per_seq):
    if sample:
        x_ref, wb_ref, wc_ref, wx_ref, cw_ref, st1_ref, st0_ref, y_ref, last_ref = refs
    else:
        x_ref, wb_ref, wc_ref, wx_ref, cw_ref, y_ref, last_ref, carry_ref = refs
    if not sample:
        _init_carry(carry_ref, pl.program_id(1), pl.program_id(0))
    x = x_ref[...]
    cx = _dot(x, wc_ref[...]) * _dot(x, wx_ref[...])
    tm = cx.shape[0]
    if sample:
        prev1, prev2 = _shifted_rows(cx, None, (st1_ref[...], st0_ref[...]))
        last_ref[...] = cx
    else:
        prev1, prev2 = _shifted_rows(cx, (carry_ref, pl.program_id(1), pl.program_id(0), tiles_per_seq), None)
        last_ref[...] = cx[tm - SUBLANES:, :]
    hconv = prev2 * cw_ref[0:1, :] + prev1 * cw_ref[1:2, :] + cx * cw_ref[2:3, :]
    y_ref[...] = _dot(x, wb_ref[...]) * hconv


def _convbr_call(xnb, w_in, conv_w, e1, e2, *, col0, sample, seq_len, tm, tn):
    rows, d = xnb.shape
    gc = conv_w.shape[1]
    n_i, n_j = rows // tm, gc // tn
    off = col0 // tn
    nb = gc // tn
    wspec = lambda k: pl.BlockSpec((d, tn), lambda i, j: (0, off + k * nb + j))
    tile = pl.BlockSpec((tm, tn), lambda i, j: (i, j))
    in_specs = [pl.BlockSpec((tm, d), lambda i, j: (i, 0)), wspec(0), wspec(1), wspec(2),
                pl.BlockSpec((3, tn), lambda i, j: (0, j))]
    args = [xnb, w_in, w_in, w_in, conv_w]
    if sample:
        sspec = pl.BlockSpec((tm // seq_len, tn), lambda i, j: (i, j))
        in_specs += [sspec, sspec]
        args += [e1, e2]
    last_rows = tm if sample else SUBLANES
    return pl.pallas_call(
        functools.partial(_convbr_kernel, sample=sample, tiles_per_seq=max(seq_len // tm, 1)),
        grid=(n_i, n_j), in_specs=in_specs,
        out_specs=[tile, pl.BlockSpec((last_rows, tn), lambda i, j: (i, j))],
        out_shape=[jax.ShapeDtypeStruct((rows, gc), F32), jax.ShapeDtypeStruct((n_i * last_rows, gc), F32)],
        scratch_shapes=[] if sample else [pltpu.VMEM((n_j, SUBLANES, tn), F32)],
        compiler_params=_params("arbitrary", "arbitrary"), name="convbr")(*args)


def _wkv_masks(c):
    m = 2 * WKV_SUB
    r2 = jnp.bitwise_and(_row_iota((m, m)), WKV_SUB - 1)
    c2 = jnp.bitwise_and(lax.broadcasted_iota(jnp.int32, (m, m), 1), WKV_SUB - 1)
    shift = int(math.log2(c))
    same = jnp.right_shift(r2, shift) == jnp.right_shift(c2, shift)
    eye = (_row_iota((m, m)) == lax.broadcasted_iota(jnp.int32, (m, m), 1)).astype(F32)
    lane_lo = lax.broadcasted_iota(jnp.int32, (WKV_SUB, LANES), 1) < HEAD_DIM
    return lane_lo, same & (c2 < r2), same & (c2 <= r2), eye


def _stack_heads(x, lane_lo):
    z = jnp.zeros_like(x)
    return jnp.concatenate([jnp.where(lane_lo, x, z), jnp.where(lane_lo, z, x)], axis=0)


def _fold_heads(x):
    return x[:WKV_SUB] + x[WKV_SUB:]


def _wkv_phase1(units, masks, c):
    lane_lo, strict, incl, eye = masks
    m = 2 * WKV_SUB
    phs = []
    for r, kt, v, kp, bt, lg, lw in units:
        e1 = jnp.exp(lg)
        e0 = jnp.exp(lg - lw)
        ei = jnp.exp(-lg)
        phs.append(dict(
            e1=e1, rt_st=_stack_heads(r * e1, lane_lo), kp_b=_stack_heads(kp * e0, lane_lo).astype(BF16),
            kh_st=_stack_heads(kt * ei, lane_lo), bh_st=_stack_heads(bt * ei, lane_lo),
            v_st=_stack_heads(v, lane_lo)))
    for ph in phs:
        ph["kh_b"], ph["bh_b"], ph["v_b"] = (ph[n].astype(BF16) for n in ("kh_st", "bh_st", "v_st"))
    gs = [_dot_nt(jnp.concatenate([ph["kp_b"], ph["rt_st"].astype(BF16)], axis=0),
                  jnp.concatenate([ph["bh_b"], ph["kh_b"]], axis=0)) for ph in phs]
    ps = [-jnp.where(strict, g[:m, :m], 0.0) for g in gs]
    ts = [eye + p for p in ps]
    n = 1
    while 2 * n < c:
        pbs = [p.astype(BF16) for p in ps]
        ps = [_dot(pb, pb) for pb in pbs]
        ts = [t + _dot(t.astype(BF16), p.astype(BF16)) for t, p in zip(ts, ps)]
        n *= 2
    abvs = [_dot(jnp.concatenate([jnp.where(strict, g[:m, m:], 0.0), jnp.where(incl, g[m:, m:], 0.0)],
                                 axis=0).astype(BF16), ph["v_b"]) for g, ph in zip(gs, phs)]
    tts = [_dot(t.astype(BF16), jnp.concatenate([ph["kp_b"], abv[:m].astype(BF16)], axis=1))
           for t, ph, abv in zip(ts, phs, abvs)]
    bbtts = [_dot(jnp.where(incl, g[m:, :m], 0.0).astype(BF16), tt.astype(BF16)) for g, tt in zip(gs, tts)]
    for ph, abv, tt, bbtt in zip(phs, abvs, tts, bbtts):
        ph["rq"] = _fold_heads(ph["rt_st"] - bbtt[:, :LANES])
        ph["ov"] = _fold_heads(abv[m:] - bbtt[:, LANES:])
        ph["tk_st"], ph["tav_st"] = tt[:, :LANES], tt[:, LANES:]
    return phs


def _wkv_transitions(phs):
    gams = [ph["e1"][WKV_SUB - 1:WKV_SUB, :] for ph in phs]
    kgs = [(_dot_tn(ph["tk_st"].astype(BF16), ph["bh_b"]) * gam).astype(BF16) for ph, gam in zip(phs, gams)]
    bcgs = [_dot_tn(jnp.concatenate([ph["v_b"], (-ph["tav_st"]).astype(BF16)], axis=0),
                    jnp.concatenate([ph["kh_b"], ph["bh_b"]], axis=0)) * gam for ph, gam in zip(phs, gams)]
    return gams, kgs, bcgs


def _wkv_units_small(states, units, c, lane_lo_c):
    rows = lambda u: slice(u * c, (u + 1) * c)
    rows_hi = lambda u: slice(WKV_SUB + u * c, WKV_SUB + (u + 1) * c)
    ous = [_dot_nt(jnp.concatenate([ph["rq"][rows(u)], ph["tkm"][rows(u)]], axis=0).astype(BF16), s.astype(BF16))
           for s, (ph, u) in zip(states, units)]
    outs, new_states = [], []
    for s, (ph, u), ou in zip(states, units, ous):
        pick = lambda x: jnp.concatenate([x[rows(u)], x[rows_hi(u)]], axis=0)
        uu = ou[c:] + ph["tav"][rows(u)]
        z = jnp.zeros_like(uu)
        u_st = jnp.concatenate([jnp.where(lane_lo_c, uu, z), jnp.where(lane_lo_c, z, uu)], axis=0)
        gam = ph["e1"][(u + 1) * c - 1:(u + 1) * c, :]
        lhs = jnp.concatenate([pick(ph["v_st"]), -u_st], axis=0).astype(BF16)
        rhs = (jnp.concatenate([pick(ph["kh_st"]), pick(ph["bh_st"])], axis=0) * gam).astype(BF16)
        outs.append(ou[:c] + ph["ov"][rows(u)])
        new_states.append(s * gam + _dot_tn(lhs, rhs))
    return outs, new_states


def _wkv_kernel(*refs, sample, chunk, n_pairs, n_sub):
    if sample:
        (r_ref, k_ref, v_ref, lw_ref, a_ref, g_ref, kk_ref, ka_ref, rk_ref, lnw_ref, lnb_ref, sin_ref,
         o_ref, sout_ref) = refs
    else:
        (r_ref, k_ref, v_ref, lw_ref, a_ref, g_ref, kk_ref, ka_ref, rk_ref, lnw_ref, lnb_ref,
         o_ref, sout_ref, s_s) = refs
    c = chunk
    lane128 = lax.broadcasted_iota(jnp.int32, (LANES, LANES), 1)
    row128 = _row_iota((LANES, LANES))
    blockdiag = (row128 < HEAD_DIM) == (lane128 < HEAD_DIM)
    block_ones = blockdiag.astype(BF16)
    seg_sum = lambda z: _split_dot(z, block_ones, 2)
    to_blockdiag = lambda x: jnp.where(blockdiag, jnp.concatenate([x, x], axis=1), 0.0)
    from_blockdiag = lambda s_: s_[:, :HEAD_DIM] + s_[:, HEAD_DIM:]
    masks = _wkv_masks(c)
    r2 = jnp.bitwise_and(_row_iota((WKV_SUB, WKV_SUB)), WKV_SUB - 1)
    c2 = lax.broadcasted_iota(jnp.int32, (WKV_SUB, WKV_SUB), 1)
    shift = int(math.log2(c))
    tril_b = ((jnp.right_shift(r2, shift) == jnp.right_shift(c2, shift)) & (c2 <= r2)).astype(BF16)
    lane_lo_c = lax.broadcasted_iota(jnp.int32, (c, LANES), 1) < HEAD_DIM

    if not sample:
        @pl.when(pl.program_id(2) == 0)
        def _():
            s_s[...] = jnp.zeros_like(s_s)

    lgs = []
    for sb in range(n_sub):
        rows = slice(sb * WKV_SUB, (sb + 1) * WKV_SUB)
        lgs.append(_split_dot(lw_ref[rows, :], tril_b, 3, dot=lambda x, b: _dot(b, x)))

    pairs = range(n_pairs)
    lanes = [slice(q * LANES, (q + 1) * LANES) for q in pairs]
    r = [r_ref[:, l] for l in lanes]
    k = [k_ref[:, l] for l in lanes]
    v = [v_ref[:, l] for l in lanes]
    a = [a_ref[:, l] for l in lanes]
    lw = [lw_ref[:, l] for l in lanes]
    kk = [k[q] * kk_ref[:, lanes[q]] for q in pairs]
    norms = [seg_sum(x * x) for x in kk]
    kk = [x / jnp.maximum(jnp.sqrt(n2), 1e-12) for x, n2 in zip(kk, norms)]
    kmod = [k[q] * (1.0 + (a[q] - 1.0) * ka_ref[:, lanes[q]]) for q in pairs]
    bt = [kk[q] * a[q] for q in pairs]

    subs = [(q, sb) for sb in range(n_sub) for q in pairs]
    units = []
    for q, sb in subs:
        rows = slice(sb * WKV_SUB, (sb + 1) * WKV_SUB)
        units.append((r[q][rows], kmod[q][rows], v[q][rows], kk[q][rows], bt[q][rows],
                      lgs[sb][:, lanes[q]], lw[q][rows]))
    phs = dict(zip(subs, _wkv_phase1(units, masks, c)))

    outs = {q: [] for q in pairs}
    if sample:
        per_sub = WKV_SUB // c
        todo = []
        for (q, sb), ph in phs.items():
            ph["tkm"] = _fold_heads(ph["tk_st"])
            ph["tav"] = _fold_heads(ph["tav_st"])
            todo += [(q, sb * per_sub + u, ph, u) for u in range(per_sub)]
        o_units, new_states = _wkv_units_small([to_blockdiag(sin_ref[seq, q]) for q, seq, _, _ in todo],
                                               [(ph, u) for _, _, ph, u in todo], c, lane_lo_c)
        for (q, seq, _, _), o, s_new in zip(todo, o_units, new_states):
            sout_ref[seq, q] = from_blockdiag(s_new)
            outs[q].append(o)
    else:
        gams, kgs, bcgs = _wkv_transitions([phs[key] for key in subs])
        trans = dict(zip(subs, zip(gams, kgs, bcgs)))
        s = [s_s[q] for q in pairs]
        for sb in range(n_sub):
            sb16 = [x.astype(BF16) for x in s]
            for q in pairs:
                outs[q].append(_dot_nt(phs[q, sb]["rq"].astype(BF16), sb16[q]) + phs[q, sb]["ov"])
            s = [s[q] * trans[q, sb][0] - _dot(sb16[q], trans[q, sb][1]) + trans[q, sb][2] for q in pairs]
        for q in pairs:
            s_s[q] = s[q]

        @pl.when(pl.program_id(2) == pl.num_programs(2) - 1)
        def _():
            for q in pairs:
                sout_ref[0, q] = from_blockdiag(s[q])

    o = [jnp.concatenate(outs[q], axis=0) for q in pairs]
    inv_n = 1.0 / HEAD_DIM
    mu = [_split_dot(x, block_ones, 1) * inv_n for x in o]
    dev = [x - m_ for x, m_ in zip(o, mu)]
    var = [_split_dot(x * x, block_ones, 1) * inv_n for x in dev]
    bonus = [seg_sum(r[q] * kmod[q] * rk_ref[:, lanes[q]]) * v[q] for q in pairs]
    for q in pairs:
        on = dev[q] * lax.rsqrt(var[q] + GN_EPS) * lnw_ref[:, lanes[q]] + lnb_ref[:, lanes[q]]
        o_ref[:, lanes[q]] = ((on + bonus[q]) * g_ref[:, lanes[q]]).astype(BF16)


def _wkv_call(rkv, lw, a, g, p, s_bd, *, sample, seq_len, t_blk, chunk, n_pairs):
    rows, gdim = lw.shape
    npair = gdim // LANES
    nseq = rows // seq_len
    width = n_pairs * LANES
    pair_blocks = npair // n_pairs
    head = [p["k_k"], p["k_a"], p["r_k"], p["ln_x_w"], p["ln_x_b"]]
    if sample:
        seq_blk = t_blk // seq_len
        grid = (nseq // seq_blk, pair_blocks)
        blk = lambda off: pl.BlockSpec((t_blk, width), lambda b, q, off=off: (b, off + q))
        hspec = pl.BlockSpec((1, width), lambda b, q: (0, q))
        sspec = pl.BlockSpec((seq_blk, n_pairs, LANES, HEAD_DIM), lambda b, q: (b, q, 0, 0))
        extra_specs, extra_args = [sspec], [s_bd]
        sem = ("arbitrary", "arbitrary")
        scratch = []
    else:
        nblk = seq_len // t_blk
        grid = (nseq, pair_blocks, nblk)
        blk = lambda off: pl.BlockSpec((t_blk, width), lambda b, q, n, off=off: (b * nblk + n, off + q))
        hspec = pl.BlockSpec((1, width), lambda b, q, n: (0, q))
        sspec = pl.BlockSpec((1, n_pairs, LANES, HEAD_DIM), lambda b, q, n: (b, q, 0, 0))
        extra_specs, extra_args = [], []
        sem = ("arbitrary", "arbitrary", "arbitrary")
        scratch = [pltpu.VMEM((n_pairs, LANES, LANES), F32)]
    in_specs = [blk(0), blk(pair_blocks), blk(2 * pair_blocks), blk(0), blk(0), blk(0)] + [hspec] * 5 + extra_specs
    args = [rkv, rkv, rkv, lw, a, g] + head + extra_args
    return pl.pallas_call(
        functools.partial(_wkv_kernel, sample=sample, chunk=chunk, n_pairs=n_pairs, n_sub=t_blk // WKV_SUB),
        grid=grid, in_specs=in_specs, out_specs=[blk(0), sspec],
        out_shape=[jax.ShapeDtypeStruct((rows, gdim), BF16),
                   jax.ShapeDtypeStruct((nseq, npair, LANES, HEAD_DIM), F32)],
        scratch_shapes=scratch, compiler_params=_params(*sem), name="wkv")(*args)


def _outproj_kernel(orw_ref, ocv_ref, w_ref, x_ref, o_ref):
    lhs = jnp.concatenate([orw_ref[...], ocv_ref[...]], axis=1)
    o_ref[...] = x_ref[...] + _dot(lhs, w_ref[...])


def _outproj_call(orw, ocv, w_out, x2d, *, tm, tn):
    rows, d = x2d.shape
    g, gc = orw.shape[1], ocv.shape[1]
    tile = pl.BlockSpec((tm, tn), lambda i, j: (i, j))
    return pl.pallas_call(
        _outproj_kernel, grid=(rows // tm, d // tn),
        in_specs=[pl.BlockSpec((tm, g), lambda i, j: (i, 0)), pl.BlockSpec((tm, gc), lambda i, j: (i, 0)),
                  pl.BlockSpec((d, tn), lambda i, j: (0, j)), tile],
        out_specs=tile, out_shape=jax.ShapeDtypeStruct((rows, d), F32),
        compiler_params=_params("arbitrary", "arbitrary"), name="outproj")(orw, ocv, w_out, x2d)


def _norm_kernel(x_ref, g_ref, o_ref):
    x = x_ref[...]
    y = x * lax.rsqrt(jnp.mean(x * x, axis=-1, keepdims=True) + RMS_EPS) * g_ref[...]
    o_ref[...] = y.astype(o_ref.dtype)


def _norm_call(x2d, g, dtype, *, tm):
    rows, d = x2d.shape
    blk = pl.BlockSpec((tm, d), lambda i: (i, 0))
    return pl.pallas_call(
        _norm_kernel, grid=(rows // tm,), in_specs=[blk, pl.BlockSpec((1, d), lambda i: (0, 0))],
        out_specs=blk, out_shape=jax.ShapeDtypeStruct((rows, d), dtype),
        compiler_params=_params("arbitrary"), name="rmsnorm")(x2d, g)


def _ffnup_kernel(*refs, sample, tiles_per_seq):
    if sample:
        x_ref, w1_ref, w3_ref, cw_ref, cb_ref, st1_ref, st0_ref, h_ref, last_ref = refs
        carry = None
    else:
        x_ref, w1_ref, w3_ref, cw_ref, cb_ref, h_ref, last_ref, carry_ref = refs
        carry = (carry_ref, pl.program_id(1), pl.program_id(0), tiles_per_seq)
        _init_carry(*carry[:3])
    tn = h_ref.shape[1]
    w13 = jnp.concatenate([w1_ref[...].astype(BF16), w3_ref[...].astype(BF16)], axis=1)
    uw = _dot(x_ref[...], w13)
    u = uw[:, :tn]
    tm = u.shape[0]
    states = (st1_ref[...], st0_ref[...]) if sample else None
    prev1, prev2 = _shifted_rows(u, carry, states)
    if sample:
        last_ref[...] = u.reshape(tm // SUBLANES, SUBLANES, tn)[:, SUBLANES - 2:, :]
    else:
        last_ref[...] = u[tm - SUBLANES:, :]
    z = prev2 * cw_ref[0:1, :] + prev1 * cw_ref[1:2, :] + u * cw_ref[2:3, :] + cb_ref[...]
    hz = 0.5 * z
    h_ref[...] = ((hz + hz * jnp.tanh(hz)) * uw[:, tn:]).astype(BF16)


def _ffnup_call(hn, w1, w3, conv_w, conv_b, st1, st0, *, sample, seq_len, tm, tn):
    rows, d = hn.shape
    dff = conv_w.shape[1]
    n_i, n_j = rows // tm, dff // tn
    tile = pl.BlockSpec((tm, tn), lambda i, j: (i, j))
    wspec = pl.BlockSpec((d, tn), lambda i, j: (0, j))
    in_specs = [pl.BlockSpec((tm, d), lambda i, j: (i, 0)), wspec, wspec,
                pl.BlockSpec((3, tn), lambda i, j: (0, j)), pl.BlockSpec((1, tn), lambda i, j: (0, j))]
    args = [hn, w1, w3, conv_w, conv_b]
    if sample:
        sspec = pl.BlockSpec((tm // seq_len, tn), lambda i, j: (i, j))
        in_specs += [sspec, sspec]
        args += [st1, st0]
    if sample:
        last_spec = pl.BlockSpec((tm // seq_len, 2, tn), lambda i, j: (i, 0, j))
        last_shape = jax.ShapeDtypeStruct((rows // seq_len, 2, dff), F32)
    else:
        last_spec = pl.BlockSpec((SUBLANES, tn), lambda i, j: (i, j))
        last_shape = jax.ShapeDtypeStruct((n_i * SUBLANES, dff), F32)
    return pl.pallas_call(
        functools.partial(_ffnup_kernel, sample=sample, tiles_per_seq=max(seq_len // tm, 1)),
        grid=(n_i, n_j), in_specs=in_specs,
        out_specs=[tile, last_spec], out_shape=[jax.ShapeDtypeStruct((rows, dff), BF16), last_shape],
        scratch_shapes=[] if sample else [pltpu.VMEM((n_j, SUBLANES, tn), F32)],
        compiler_params=_params("arbitrary", "arbitrary"), name="ffnup")(*args)


def _ffndown_kernel(h_ref, w_ref, x_ref, o_ref):
    o_ref[...] = x_ref[...] + _dot(h_ref[...], w_ref[...])


def _ffndown_call(h, w2, x1, *, tm, tn):
    rows, dff = h.shape
    d = w2.shape[1]
    tile = pl.BlockSpec((tm, tn), lambda i, j: (i, j))
    return pl.pallas_call(
        _ffndown_kernel, grid=(rows // tm, d // tn),
        in_specs=[pl.BlockSpec((tm, dff), lambda i, j: (i, 0)), pl.BlockSpec((dff, tn), lambda i, j: (0, j)), tile],
        out_specs=tile, out_shape=jax.ShapeDtypeStruct((rows, d), F32),
        compiler_params=_params("arbitrary", "arbitrary"), name="ffndown")(h, w2, x1)


def _tile(n, want):
    t = min(n, want)
    while n % t or (t % SUBLANES and t != n):
        t -= 1
    return t


def _col_tile(n, want):
    t = min(n, want)
    while n % t or t % LANES:
        t -= LANES
    return t


def _layer(x, states, p, *, sample):
    nseq, seq_len, d = x.shape
    rows = nseq * seq_len
    g = p["dw2"].shape[1]
    gc = d - g
    dff = p["ffn_conv_w"].shape[1]
    npair = g // LANES
    x2d = x.reshape(rows, d)
    big = dict(sample=sample, seq_len=seq_len)
    tm_big = rows if sample else _tile(seq_len, 1024)

    if sample:
        assert seq_len == SUBLANES, "the sample path shifts rows inside 8-row groups"
        shift, wkv, conv, ffn = states
        ext_x = shift
        ext_p = shift.astype(BF16)
        ce1, ce2 = conv[:, 1], conv[:, 0]
        fe1, fe2 = ffn[:, 1], ffn[:, 0]
        s_bd = wkv.reshape(nseq, npair, LANES, HEAD_DIM)
    else:
        ext_x = ext_p = ce1 = ce2 = fe1 = fe2 = s_bd = None

    tm_mix = _tile(rows if sample else seq_len, 256)
    xnb, lw, a, gate, xlast = _mix_call(x2d, ext_x, p, sample=sample, seq_len=seq_len, tm=tm_mix)
    rkv = _rkv_call(xnb, p["w_in"], p["mu_rkv"], ext_p, tm=tm_big, tn=_col_tile(3 * g, 1024), **big)
    tm_conv = _tile(rows if sample else seq_len, 1024)
    ycv, cxlast = _convbr_call(xnb, p["w_in"], p["conv_w"], ce1, ce2, col0=3 * g, tm=tm_conv,
                               tn=_col_tile(gc, 512), **big)
    if sample:
        t_blk, chunk = WKV_SUB, seq_len
    else:
        chunk = WKV_SUB
        t_blk = _tile(seq_len, 4 * WKV_SUB)
    orw, s_new = _wkv_call(rkv, lw, a, gate, p, s_bd, t_blk=t_blk, chunk=chunk,
                           n_pairs=math.gcd(npair, 16), **big)
    tm_e = _tile(rows, 512)
    ocv = _norm_call(ycv, p["conv_norm_g"], BF16, tm=_tile(rows, 512))
    x1 = _outproj_call(orw, ocv, p["w_out"], x2d, tm=_tile(rows, 1024), tn=_col_tile(d, 1024))
    hn = _norm_call(x1, p["norm2_g"], BF16, tm=_tile(rows, 512))
    tm_ffn = rows if sample else _tile(seq_len, 2048)
    h, ulast = _ffnup_call(hn, p["ffn_w1"], p["ffn_w3"], p["ffn_conv_w"], p["ffn_conv_b"], fe1, fe2,
                           tm=tm_ffn, tn=_col_tile(dff, 256), **big)
    x2 = _ffndown_call(h, p["ffn_w2"], x1, tm=tm_e, tn=_col_tile(d, 512))

    def last_rows(arr, tile_rows, k):
        if sample:
            return arr.reshape(nseq, seq_len, -1)[:, seq_len - k:]
        per_seq = seq_len // tile_rows
        return arr.reshape(nseq, per_seq, SUBLANES, -1)[:, -1, SUBLANES - k:]
    new_shift = last_rows(xlast, tm_mix, 1)[:, 0]
    new_conv = last_rows(cxlast, tm_conv, 2)
    new_ffn = ulast if sample else last_rows(ulast, tm_ffn, 2)
    new_wkv = s_new.reshape(nseq, 2 * npair, HEAD_DIM, HEAD_DIM)
    return x2.reshape(nseq, seq_len, d), new_shift, new_wkv, new_conv, new_ffn


def _pad_to(a, axis, mult):
    pad = (-a.shape[axis]) % mult
    if not pad:
        return a
    widths = [(0, 0)] * a.ndim
    widths[axis] = (0, pad)
    return jnp.pad(a, widths)


def kernel(x_prompt, x_sample, state_shift, state_wkv, state_conv, state_ffn, norm1_g, w_in, mu_rkv, mu_lora, decay_w0, decay_w1, decay_w2, aaa_a0, aaa_a1, aaa_a2, gate_g1, gate_g2, k_k, k_a, r_k, ln_x_w, ln_x_b, conv_w, conv_norm_g, w_out, norm2_g, ffn_w1, ffn_conv_w, ffn_conv_b, ffn_w3, ffn_w2, final_norm_g):
    depth = w_in.shape[0]
    d = x_prompt.shape[-1]
    row = lambda v: v.reshape(1, -1).astype(F32)
    yp, ys = x_prompt, x_sample
    outs_p, outs_s = [], []
    for l in range(depth):
        p = dict(
            norm1_g=row(norm1_g[l]), w_in=w_in[l].astype(BF16), mu_rkv=row(mu_rkv[l]), mu_lora=mu_lora[l],
            decay_w0=row(decay_w0[l]), aaa_a0=row(aaa_a0[l]),
            dw1=decay_w1[l].astype(BF16), dw2=decay_w2[l].astype(BF16),
            aw1=aaa_a1[l].astype(BF16), aw2=aaa_a2[l].astype(BF16),
            gw1=_pad_to(gate_g1[l], 1, LANES).astype(BF16), gw2=_pad_to(gate_g2[l], 0, LANES).astype(BF16),
            k_k=row(k_k[l]), k_a=row(k_a[l]), r_k=row(r_k[l]), ln_x_w=row(ln_x_w[l]), ln_x_b=row(ln_x_b[l]),
            conv_w=conv_w[l], conv_norm_g=row(conv_norm_g[l]), w_out=w_out[l].astype(BF16),
            norm2_g=row(norm2_g[l]), ffn_w1=ffn_w1[l], ffn_w3=ffn_w3[l],
            ffn_conv_w=ffn_conv_w[l], ffn_conv_b=row(ffn_conv_b[l]), ffn_w2=ffn_w2[l].astype(BF16))
        yp, *st_p = _layer(yp, None, p, sample=False)
        ys, *st_s = _layer(ys, (state_shift[l], state_wkv[l], state_conv[l], state_ffn[l]), p, sample=True)
        outs_p.append(st_p)
        outs_s.append(st_s)
    fin = row(final_norm_g)
    y_prompt = _norm_call(yp.reshape(-1, d), fin, F32, tm=_tile(yp.shape[0] * yp.shape[1], 512)).reshape(yp.shape)
    y_sample = _norm_call(ys.reshape(-1, d), fin, F32, tm=_tile(ys.shape[0] * ys.shape[1], 512)).reshape(ys.shape)
    stack = lambda outs, k: jnp.stack([o[k] for o in outs])
    return (y_prompt, y_sample,
            stack(outs_p, 0), stack(outs_p, 1), stack(outs_p, 2), stack(outs_p, 3),
            stack(outs_s, 0), stack(outs_s, 1), stack(outs_s, 2), stack(outs_s, 3))
```

```python
import functools
import math

import jax
import jax.numpy as jnp
from jax import lax
from jax.experimental import pallas as pl
from jax.experimental.pallas import tpu as pltpu

F32 = jnp.float32
BF16 = jnp.bfloat16

HEAD_DIM = 64
LANES = 128
SUBLANES = 8
WKV_SUB = 64
RMS_EPS = 1e-6
GN_EPS = 64e-5
VMEM_LIMIT_BYTES = 60 * 1024 * 1024


def _params(*sem):
    return pltpu.CompilerParams(dimension_semantics=sem, vmem_limit_bytes=VMEM_LIMIT_BYTES)


def _dot(a, b):
    return jnp.dot(a, b, preferred_element_type=F32)


def _dot_nt(a, b):
    return lax.dot_general(a, b, (((1,), (1,)), ((), ())), preferred_element_type=F32)


def _dot_tn(a, b):
    return lax.dot_general(a, b, (((0,), (0,)), ((), ())), preferred_element_type=F32)


def _sigmoid(z):
    return 1.0 / (1.0 + jnp.exp(-z))


def _split_dot(x, b_exact, terms, dot=_dot):
    acc = None
    rem = x
    for _ in range(terms):
        hi = rem.astype(BF16)
        part = dot(hi, b_exact)
        acc = part if acc is None else acc + part
        rem = rem - hi.astype(F32)
    return acc


def _row_iota(shape):
    return lax.broadcasted_iota(jnp.int32, shape, 0)


def _init_carry(carry_ref, j, i):
    @pl.when(i == 0)
    def _():
        carry_ref[j] = jnp.zeros(carry_ref.shape[1:], F32)


def _shifted_rows(u, carry, states):
    tm = u.shape[0]
    row = _row_iota(u.shape)
    r1 = pltpu.roll(u, 1, 0)
    r2 = pltpu.roll(u, 2, 0)
    if carry is not None:
        carry_ref, j, i, tiles_per_seq = carry
        c = carry_ref[j]
        keep = (i % tiles_per_seq) != 0
        first1 = jnp.where(keep, c[SUBLANES - 1:SUBLANES, :], 0.0)
        first2 = jnp.where(keep, c[SUBLANES - 2:SUBLANES - 1, :], 0.0)
        carry_ref[j] = u[tm - SUBLANES:, :]
        return jnp.where(row == 0, first1, r1), jnp.where(row == 0, first2, jnp.where(row == 1, first1, r2))
    first1, first2 = _expand_state_rows(states[0], states[1], tm)
    t = row % SUBLANES
    return jnp.where(t == 0, first1, r1), (None if first2 is None else jnp.where(t < 2, first2, r2))


def _expand_state_rows(st1, st0, tm):
    ns, n = st1.shape
    spread = lambda x: jnp.broadcast_to(x[:, None, :], (ns, SUBLANES, n)).reshape(tm, n)
    first1 = spread(st1)
    if st0 is None:
        return first1, None
    t = _row_iota((tm, n)) % SUBLANES
    return first1, jnp.where(t == 0, spread(st0), first1)


def _mix_kernel(*refs, sample, tiles_per_seq):
    if sample:
        (x_ref, ext_ref, g1_ref, mu_ref, w0_ref, a0_ref, dw1_ref, aw1_ref, gw1_ref, dw2_ref,
         aw2_ref, gw2_ref, xnb_ref, lw_ref, a_ref, g_ref, last_ref) = refs
    else:
        (x_ref, g1_ref, mu_ref, w0_ref, a0_ref, dw1_ref, aw1_ref, gw1_ref, dw2_ref,
         aw2_ref, gw2_ref, xnb_ref, lw_ref, a_ref, g_ref, last_ref, carry_ref) = refs
    if not sample:
        _init_carry(carry_ref, 0, pl.program_id(0))
    x = x_ref[...]
    xn = x * lax.rsqrt(jnp.mean(x * x, axis=-1, keepdims=True) + RMS_EPS) * g1_ref[...]
    tm = xn.shape[0]
    if sample:
        prev, _unused = _shifted_rows(xn, None, (ext_ref[...], None))
        last_ref[...] = xn
    else:
        prev, _unused = _shifted_rows(xn, (carry_ref, 0, pl.program_id(0), tiles_per_seq), None)
        last_ref[...] = xn[tm - SUBLANES:, :]
    dx = prev - xn
    xnb_ref[...] = xn.astype(BF16)
    xw = (xn + dx * mu_ref[0:1, :]).astype(BF16)
    hw = jnp.tanh(_dot(xw, dw1_ref[...]))
    wl = w0_ref[...] + _dot(hw.astype(BF16), dw2_ref[...])
    lw_ref[...] = -_sigmoid(wl) * math.exp(-0.5)
    xa = (xn + dx * mu_ref[1:2, :]).astype(BF16)
    ha = _dot(xa, aw1_ref[...])
    a_ref[...] = _sigmoid(a0_ref[...] + _dot(ha.astype(BF16), aw2_ref[...]))
    xg = (xn + dx * mu_ref[2:3, :]).astype(BF16)
    hg = _sigmoid(_dot(xg, gw1_ref[...]))
    g_ref[...] = _dot(hg.astype(BF16), gw2_ref[...])


def _mix_call(x2d, ext, p, *, sample, seq_len, tm):
    rows, d = x2d.shape
    g = p["dw2"].shape[1]
    n_i = rows // tm
    row_blk = lambda w: pl.BlockSpec((tm, w), lambda i: (i, 0))
    full = lambda a: pl.BlockSpec(a.shape, lambda i: (0, 0))
    weights = [p["norm1_g"], p["mu_lora"], p["decay_w0"], p["aaa_a0"], p["dw1"], p["aw1"], p["gw1"],
               p["dw2"], p["aw2"], p["gw2"]]
    state_blk = pl.BlockSpec((tm // seq_len, d), lambda i: (i, 0))
    in_specs = [row_blk(d)] + ([state_blk] if sample else []) + [full(w) for w in weights]
    args = [x2d] + ([ext] if sample else []) + weights
    last_rows = tm if sample else SUBLANES
    out_shape = [jax.ShapeDtypeStruct((rows, d), BF16)] + [jax.ShapeDtypeStruct((rows, g), F32)] * 3 + [
        jax.ShapeDtypeStruct((n_i * last_rows, d), F32)]
    out_specs = [row_blk(d), row_blk(g), row_blk(g), row_blk(g), pl.BlockSpec((last_rows, d), lambda i: (i, 0))]
    return pl.pallas_call(
        functools.partial(_mix_kernel, sample=sample, tiles_per_seq=max(seq_len // tm, 1)),
        grid=(n_i,), in_specs=in_specs, out_specs=out_specs, out_shape=out_shape,
        scratch_shapes=[] if sample else [pltpu.VMEM((1, SUBLANES, d), F32)],
        compiler_params=_params("arbitrary"), name="mix")(*args)


def _rkv_kernel(*refs, sample, tiles_per_seq):
    if sample:
        x_ref, w_ref, mu_ref, shift_ref, o_ref = refs
    else:
        x_ref, w_ref, mu_ref, o_ref, carry_ref = refs
    if not sample:
        _init_carry(carry_ref, pl.program_id(1), pl.program_id(0))
    w = w_ref[...]
    p = _dot(x_ref[...], w)
    if sample:
        prev, _unused = _shifted_rows(p, None, (_dot(shift_ref[...], w), None))
    else:
        prev, _unused = _shifted_rows(p, (carry_ref, pl.program_id(1), pl.program_id(0), tiles_per_seq), None)
    o_ref[...] = p + mu_ref[...] * (prev - p)


def _rkv_call(xnb, w_in, mu, ext, *, sample, seq_len, tm, tn):
    rows, d = xnb.shape
    n = mu.shape[1]
    n_i, n_j = rows // tm, n // tn
    in_specs = [pl.BlockSpec((tm, d), lambda i, j: (i, 0)), pl.BlockSpec((d, tn), lambda i, j: (0, j)),
                pl.BlockSpec((1, tn), lambda i, j: (0, j))]
    args = [xnb, w_in, mu]
    if sample:
        in_specs.append(pl.BlockSpec((tm // seq_len, d), lambda i, j: (i, 0)))
        args.append(ext)
    return pl.pallas_call(
        functools.partial(_rkv_kernel, sample=sample, tiles_per_seq=max(seq_len // tm, 1)),
        grid=(n_i, n_j), in_specs=in_specs, out_specs=pl.BlockSpec((tm, tn), lambda i, j: (i, j)),
        out_shape=jax.ShapeDtypeStruct((rows, n), F32),
        scratch_shapes=[] if sample else [pltpu.VMEM((n_j, SUBLANES, tn), F32)],
        compiler_params=_params("arbitrary", "arbitrary"), name="rkv")(*args)


def _convbr_kernel(*refs, sample, tiles_per_seq):
    if sample:
        x_ref, wb_ref, wc_ref, wx_ref, cw_ref, st1_ref, st0_ref, y_ref, last_ref = refs
    else:
        x_ref, wb_ref, wc_ref, wx_ref, cw_ref, y_ref, last_ref, carry_ref = refs
    if not sample:
        _init_carry(carry_ref, pl.program_id(1), pl.program_id(0))
    x = x_ref[...]
    cx = _dot(x, wc_ref[...]) * _dot(x, wx_ref[...])
    tm = cx.shape[0]
    if sample:
        prev1, prev2 = _shifted_rows(cx, None, (st1_ref[...], st0_ref[...]))
        last_ref[...] = cx
    else:
        prev1, prev2 = _shifted_rows(cx, (carry_ref, pl.program_id(1), pl.program_id(0), tiles_per_seq), None)
        last_ref[...] = cx[tm - SUBLANES:, :]
    hconv = prev2 * cw_ref[0:1, :] + prev1 * cw_ref[1:2, :] + cx * cw_ref[2:3, :]
    y_ref[...] = _dot(x, wb_ref[...]) * hconv


def _convbr_call(xnb, w_in, conv_w, e1, e2, *, col0, sample, seq_len, tm, tn):
    rows, d = xnb.shape
    gc = conv_w.shape[1]
    n_i, n_j = rows // tm, gc // tn
    off = col0 // tn
    nb = gc // tn
    wspec = lambda k: pl.BlockSpec((d, tn), lambda i, j: (0, off + k * nb + j))
    tile = pl.BlockSpec((tm, tn), lambda i, j: (i, j))
    in_specs = [pl.BlockSpec((tm, d), lambda i, j: (i, 0)), wspec(0), wspec(1), wspec(2),
                pl.BlockSpec((3, tn), lambda i, j: (0, j))]
    args = [xnb, w_in, w_in, w_in, conv_w]
    if sample:
        sspec = pl.BlockSpec((tm // seq_len, tn), lambda i, j: (i, j))
        in_specs += [sspec, sspec]
        args += [e1, e2]
    last_rows = tm if sample else SUBLANES
    return pl.pallas_call(
        functools.partial(_convbr_kernel, sample=sample, tiles_per_seq=max(seq_len // tm, 1)),
        grid=(n_i, n_j), in_specs=in_specs,
        out_specs=[tile, pl.BlockSpec((last_rows, tn), lambda i, j: (i, j))],
        out_shape=[jax.ShapeDtypeStruct((rows, gc), F32), jax.ShapeDtypeStruct((n_i * last_rows, gc), F32)],
        scratch_shapes=[] if sample else [pltpu.VMEM((n_j, SUBLANES, tn), F32)],
        compiler_params=_params("arbitrary", "arbitrary"), name="convbr")(*args)


def _wkv_masks(c):
    m = 2 * WKV_SUB
    r2 = jnp.bitwise_and(_row_iota((m, m)), WKV_SUB - 1)
    c2 = jnp.bitwise_and(lax.broadcasted_iota(jnp.int32, (m, m), 1), WKV_SUB - 1)
    shift = int(math.log2(c))
    same = jnp.right_shift(r2, shift) == jnp.right_shift(c2, shift)
    eye = (_row_iota((m, m)) == lax.broadcasted_iota(jnp.int32, (m, m), 1)).astype(F32)
    lane_lo = lax.broadcasted_iota(jnp.int32, (WKV_SUB, LANES), 1) < HEAD_DIM
    return lane_lo, same & (c2 < r2), same & (c2 <= r2), eye


def _stack_heads(x, lane_lo):
    z = jnp.zeros_like(x)
    return jnp.concatenate([jnp.where(lane_lo, x, z), jnp.where(lane_lo, z, x)], axis=0)


def _fold_heads(x):
    return x[:WKV_SUB] + x[WKV_SUB:]


def _wkv_phase1(units, masks, c):
    lane_lo, strict, incl, eye = masks
    m = 2 * WKV_SUB
    phs = []
    for r, kt, v, kp, bt, lg, lw in units:
        e1 = jnp.exp(lg)
        e0 = jnp.exp(lg - lw)
        ei = jnp.exp(-lg)
        phs.append(dict(
            e1=e1, rt_st=_stack_heads(r * e1, lane_lo), kp_b=_stack_heads(kp * e0, lane_lo).astype(BF16),
            kh_st=_stack_heads(kt * ei, lane_lo), bh_st=_stack_heads(bt * ei, lane_lo),
            v_st=_stack_heads(v, lane_lo)))
    for ph in phs:
        ph["kh_b"], ph["bh_b"], ph["v_b"] = (ph[n].astype(BF16) for n in ("kh_st", "bh_st", "v_st"))
    gs = [_dot_nt(jnp.concatenate([ph["kp_b"], ph["rt_st"].astype(BF16)], axis=0),
                  jnp.concatenate([ph["bh_b"], ph["kh_b"]], axis=0)) for ph in phs]
    ps = [-jnp.where(strict, g[:m, :m], 0.0) for g in gs]
    ts = [eye + p for p in ps]
    n = 1
    while 2 * n < c:
        pbs = [p.astype(BF16) for p in ps]
        ps = [_dot(pb, pb) for pb in pbs]
        ts = [t + _dot(t.astype(BF16), p.astype(BF16)) for t, p in zip(ts, ps)]
        n *= 2
    abvs = [_dot(jnp.concatenate([jnp.where(strict, g[:m, m:], 0.0), jnp.where(incl, g[m:, m:], 0.0)],
                                 axis=0).astype(BF16), ph["v_b"]) for g, ph in zip(gs, phs)]
    tts = [_dot(t.astype(BF16), jnp.concatenate([ph["kp_b"], abv[:m].astype(BF16)], axis=1))
           for t, ph, abv in zip(ts, phs, abvs)]
    bbtts = [_dot(jnp.where(incl, g[m:, :m], 0.0).astype(BF16), tt.astype(BF16)) for g, tt in zip(gs, tts)]
    for ph, abv, tt, bbtt in zip(phs, abvs, tts, bbtts):
        ph["rq"] = _fold_heads(ph["rt_st"] - bbtt[:, :LANES])
        ph["ov"] = _fold_heads(abv[m:] - bbtt[:, LANES:])
        ph["tk_st"], ph["tav_st"] = tt[:, :LANES], tt[:, LANES:]
    return phs


def _wkv_transitions(phs):
    gams = [ph["e1"][WKV_SUB - 1:WKV_SUB, :] for ph in phs]
    kgs = [(_dot_tn(ph["tk_st"].astype(BF16), ph["bh_b"]) * gam).astype(BF16) for ph, gam in zip(phs, gams)]
    bcgs = [_dot_tn(jnp.concatenate([ph["v_b"], (-ph["tav_st"]).astype(BF16)], axis=0),
                    jnp.concatenate([ph["kh_b"], ph["bh_b"]], axis=0)) * gam for ph, gam in zip(phs, gams)]
    return gams, kgs, bcgs


def _wkv_units_small(states, units, c, lane_lo_c):
    rows = lambda u: slice(u * c, (u + 1) * c)
    rows_hi = lambda u: slice(WKV_SUB + u * c, WKV_SUB + (u + 1) * c)
    ous = [_dot_nt(jnp.concatenate([ph["rq"][rows(u)], ph["tkm"][rows(u)]], axis=0).astype(BF16), s.astype(BF16))
           for s, (ph, u) in zip(states, units)]
    outs, new_states = [], []
    for s, (ph, u), ou in zip(states, units, ous):
        pick = lambda x: jnp.concatenate([x[rows(u)], x[rows_hi(u)]], axis=0)
        uu = ou[c:] + ph["tav"][rows(u)]
        z = jnp.zeros_like(uu)
        u_st = jnp.concatenate([jnp.where(lane_lo_c, uu, z), jnp.where(lane_lo_c, z, uu)], axis=0)
        gam = ph["e1"][(u + 1) * c - 1:(u + 1) * c, :]
        lhs = jnp.concatenate([pick(ph["v_st"]), -u_st], axis=0).astype(BF16)
        rhs = (jnp.concatenate([pick(ph["kh_st"]), pick(ph["bh_st"])], axis=0) * gam).astype(BF16)
        outs.append(ou[:c] + ph["ov"][rows(u)])
        new_states.append(s * gam + _dot_tn(lhs, rhs))
    return outs, new_states


def _wkv_kernel(*refs, sample, chunk, n_pairs, n_sub):
    if sample:
        (r_ref, k_ref, v_ref, lw_ref, a_ref, g_ref, kk_ref, ka_ref, rk_ref, lnw_ref, lnb_ref, sin_ref,
         o_ref, sout_ref) = refs
    else:
        (r_ref, k_ref, v_ref, lw_ref, a_ref, g_ref, kk_ref, ka_ref, rk_ref, lnw_ref, lnb_ref,
         o_ref, sout_ref, s_s) = refs
    c = chunk
    lane128 = lax.broadcasted_iota(jnp.int32, (LANES, LANES), 1)
    row128 = _row_iota((LANES, LANES))
    blockdiag = (row128 < HEAD_DIM) == (lane128 < HEAD_DIM)
    block_ones = blockdiag.astype(BF16)
    seg_sum = lambda z: _split_dot(z, block_ones, 2)
    to_blockdiag = lambda x: jnp.where(blockdiag, jnp.concatenate([x, x], axis=1), 0.0)
    from_blockdiag = lambda s_: s_[:, :HEAD_DIM] + s_[:, HEAD_DIM:]
    masks = _wkv_masks(c)
    r2 = jnp.bitwise_and(_row_iota((WKV_SUB, WKV_SUB)), WKV_SUB - 1)
    c2 = lax.broadcasted_iota(jnp.int32, (WKV_SUB, WKV_SUB), 1)
    shift = int(math.log2(c))
    tril_b = ((jnp.right_shift(r2, shift) == jnp.right_shift(c2, shift)) & (c2 <= r2)).astype(BF16)
    lane_lo_c = lax.broadcasted_iota(jnp.int32, (c, LANES), 1) < HEAD_DIM

    if not sample:
        @pl.when(pl.program_id(2) == 0)
        def _():
            s_s[...] = jnp.zeros_like(s_s)

    lgs = []
    for sb in range(n_sub):
        rows = slice(sb * WKV_SUB, (sb + 1) * WKV_SUB)
        lgs.append(_split_dot(lw_ref[rows, :], tril_b, 3, dot=lambda x, b: _dot(b, x)))

    pairs = range(n_pairs)
    lanes = [slice(q * LANES, (q + 1) * LANES) for q in pairs]
    r = [r_ref[:, l] for l in lanes]
    k = [k_ref[:, l] for l in lanes]
    v = [v_ref[:, l] for l in lanes]
    a = [a_ref[:, l] for l in lanes]
    lw = [lw_ref[:, l] for l in lanes]
    kk = [k[q] * kk_ref[:, lanes[q]] for q in pairs]
    norms = [seg_sum(x * x) for x in kk]
    kk = [x / jnp.maximum(jnp.sqrt(n2), 1e-12) for x, n2 in zip(kk, norms)]
    kmod = [k[q] * (1.0 + (a[q] - 1.0) * ka_ref[:, lanes[q]]) for q in pairs]
    bt = [kk[q] * a[q] for q in pairs]

    subs = [(q, sb) for sb in range(n_sub) for q in pairs]
    units = []
    for q, sb in subs:
        rows = slice(sb * WKV_SUB, (sb + 1) * WKV_SUB)
        units.append((r[q][rows], kmod[q][rows], v[q][rows], kk[q][rows], bt[q][rows],
                      lgs[sb][:, lanes[q]], lw[q][rows]))
    phs = dict(zip(subs, _wkv_phase1(units, masks, c)))

    outs = {q: [] for q in pairs}
    if sample:
        per_sub = WKV_SUB // c
        todo = []
        for (q, sb), ph in phs.items():
            ph["tkm"] = _fold_heads(ph["tk_st"])
            ph["tav"] = _fold_heads(ph["tav_st"])
            todo += [(q, sb * per_sub + u, ph, u) for u in range(per_sub)]
        o_units, new_states = _wkv_units_small([to_blockdiag(sin_ref[seq, q]) for q, seq, _, _ in todo],
                                               [(ph, u) for _, _, ph, u in todo], c, lane_lo_c)
        for (q, seq, _, _), o, s_new in zip(todo, o_units, new_states):
            sout_ref[seq, q] = from_blockdiag(s_new)
            outs[q].append(o)
    else:
        gams, kgs, bcgs = _wkv_transitions([phs[key] for key in subs])
        trans = dict(zip(subs, zip(gams, kgs, bcgs)))
        s = [s_s[q] for q in pairs]
        for sb in range(n_sub):
            sb16 = [x.astype(BF16) for x in s]
            for q in pairs:
                outs[q].append(_dot_nt(phs[q, sb]["rq"].astype(BF16), sb16[q]) + phs[q, sb]["ov"])
            s = [s[q] * trans[q, sb][0] - _dot(sb16[q], trans[q, sb][1]) + trans[q, sb][2] for q in pairs]
        for q in pairs:
            s_s[q] = s[q]

        @pl.when(pl.program_id(2) == pl.num_programs(2) - 1)
        def _():
            for q in pairs:
                sout_ref[0, q] = from_blockdiag(s[q])

    o = [jnp.concatenate(outs[q], axis=0) for q in pairs]
    inv_n = 1.0 / HEAD_DIM
    mu = [_split_dot(x, block_ones, 1) * inv_n for x in o]
    dev = [x - m_ for x, m_ in zip(o, mu)]
    var = [_split_dot(x * x, block_ones, 1) * inv_n for x in dev]
    bonus = [seg_sum(r[q] * kmod[q] * rk_ref[:, lanes[q]]) * v[q] for q in pairs]
    for q in pairs:
        on = dev[q] * lax.rsqrt(var[q] + GN_EPS) * lnw_ref[:, lanes[q]] + lnb_ref[:, lanes[q]]
        o_ref[:, lanes[q]] = ((on + bonus[q]) * g_ref[:, lanes[q]]).astype(BF16)


def _wkv_call(rkv, lw, a, g, p, s_bd, *, sample, seq_len, t_blk, chunk, n_pairs):
    rows, gdim = lw.shape
    npair = gdim // LANES
    nseq = rows // seq_len
    width = n_pairs * LANES
    pair_blocks = npair // n_pairs
    head = [p["k_k"], p["k_a"], p["r_k"], p["ln_x_w"], p["ln_x_b"]]
    if sample:
        seq_blk = t_blk // seq_len
        grid = (nseq // seq_blk, pair_blocks)
        blk = lambda off: pl.BlockSpec((t_blk, width), lambda b, q, off=off: (b, off + q))
        hspec = pl.BlockSpec((1, width), lambda b, q: (0, q))
        sspec = pl.BlockSpec((seq_blk, n_pairs, LANES, HEAD_DIM), lambda b, q: (b, q, 0, 0))
        extra_specs, extra_args = [sspec], [s_bd]
        sem = ("arbitrary", "arbitrary")
        scratch = []
    else:
        nblk = seq_len // t_blk
        grid = (nseq, pair_blocks, nblk)
        blk = lambda off: pl.BlockSpec((t_blk, width), lambda b, q, n, off=off: (b * nblk + n, off + q))
        hspec = pl.BlockSpec((1, width), lambda b, q, n: (0, q))
        sspec = pl.BlockSpec((1, n_pairs, LANES, HEAD_DIM), lambda b, q, n: (b, q, 0, 0))
        extra_specs, extra_args = [], []
        sem = ("arbitrary", "arbitrary", "arbitrary")
        scratch = [pltpu.VMEM((n_pairs, LANES, LANES), F32)]
    in_specs = [blk(0), blk(pair_blocks), blk(2 * pair_blocks), blk(0), blk(0), blk(0)] + [hspec] * 5 + extra_specs
    args = [rkv, rkv, rkv, lw, a, g] + head + extra_args
    return pl.pallas_call(
        functools.partial(_wkv_kernel, sample=sample, chunk=chunk, n_pairs=n_pairs, n_sub=t_blk // WKV_SUB),
        grid=grid, in_specs=in_specs, out_specs=[blk(0), sspec],
        out_shape=[jax.ShapeDtypeStruct((rows, gdim), BF16),
                   jax.ShapeDtypeStruct((nseq, npair, LANES, HEAD_DIM), F32)],
        scratch_shapes=scratch, compiler_params=_params(*sem), name="wkv")(*args)


def _outproj_kernel(orw_ref, ocv_ref, w_ref, x_ref, o_ref):
    lhs = jnp.concatenate([orw_ref[...], ocv_ref[...]], axis=1)
    o_ref[...] = x_ref[...] + _dot(lhs, w_ref[...])


def _outproj_call(orw, ocv, w_out, x2d, *, tm, tn):
    rows, d = x2d.shape
    g, gc = orw.shape[1], ocv.shape[1]
    tile = pl.BlockSpec((tm, tn), lambda i, j: (i, j))
    return pl.pallas_call(
        _outproj_kernel, grid=(rows // tm, d // tn),
        in_specs=[pl.BlockSpec((tm, g), lambda i, j: (i, 0)), pl.BlockSpec((tm, gc), lambda i, j: (i, 0)),
                  pl.BlockSpec((d, tn), lambda i, j: (0, j)), tile],
        out_specs=tile, out_shape=jax.ShapeDtypeStruct((rows, d), F32),
        compiler_params=_params("arbitrary", "arbitrary"), name="outproj")(orw, ocv, w_out, x2d)


def _norm_kernel(x_ref, g_ref, o_ref):
    x = x_ref[...]
    y = x * lax.rsqrt(jnp.mean(x * x, axis=-1, keepdims=True) + RMS_EPS) * g_ref[...]
    o_ref[...] = y.astype(o_ref.dtype)


def _norm_call(x2d, g, dtype, *, tm):
    rows, d = x2d.shape
    blk = pl.BlockSpec((tm, d), lambda i: (i, 0))
    return pl.pallas_call(
        _norm_kernel, grid=(rows // tm,), in_specs=[blk, pl.BlockSpec((1, d), lambda i: (0, 0))],
        out_specs=blk, out_shape=jax.ShapeDtypeStruct((rows, d), dtype),
        compiler_params=_params("arbitrary"), name="rmsnorm")(x2d, g)


def _ffnup_kernel(*refs, sample, tiles_per_seq):
    if sample:
        x_ref, w1_ref, w3_ref, cw_ref, cb_ref, st1_ref, st0_ref, h_ref, last_ref = refs
        carry = None
    else:
        x_ref, w1_ref, w3_ref, cw_ref, cb_ref, h_ref, last_ref, carry_ref = refs
        carry = (carry_ref, pl.program_id(1), pl.program_id(0), tiles_per_seq)
        _init_carry(*carry[:3])
    tn = h_ref.shape[1]
    w13 = jnp.concatenate([w1_ref[...].astype(BF16), w3_ref[...].astype(BF16)], axis=1)
    uw = _dot(x_ref[...], w13)
    u = uw[:, :tn]
    tm = u.shape[0]
    states = (st1_ref[...], st0_ref[...]) if sample else None
    prev1, prev2 = _shifted_rows(u, carry, states)
    if sample:
        last_ref[...] = u.reshape(tm // SUBLANES, SUBLANES, tn)[:, SUBLANES - 2:, :]
    else:
        last_ref[...] = u[tm - SUBLANES:, :]
    cw = 0.5 * cw_ref[...]
    hz = prev2 * cw[0:1, :] + prev1 * cw[1:2, :] + u * cw[2:3, :] + 0.5 * cb_ref[...]
    h_ref[...] = ((hz + hz * jnp.tanh(hz)) * uw[:, tn:]).astype(BF16)


def _ffnup_call(hn, w1, w3, conv_w, conv_b, st1, st0, *, sample, seq_len, tm, tn):
    rows, d = hn.shape
    dff = conv_w.shape[1]
    n_i, n_j = rows // tm, dff // tn
    tile = pl.BlockSpec((tm, tn), lambda i, j: (i, j))
    wspec = pl.BlockSpec((d, tn), lambda i, j: (0, j))
    in_specs = [pl.BlockSpec((tm, d), lambda i, j: (i, 0)), wspec, wspec,
                pl.BlockSpec((3, tn), lambda i, j: (0, j)), pl.BlockSpec((1, tn), lambda i, j: (0, j))]
    args = [hn, w1, w3, conv_w, conv_b]
    if sample:
        sspec = pl.BlockSpec((tm // seq_len, tn), lambda i, j: (i, j))
        in_specs += [sspec, sspec]
        args += [st1, st0]
    if sample:
        last_spec = pl.BlockSpec((tm // seq_len, 2, tn), lambda i, j: (i, 0, j))
        last_shape = jax.ShapeDtypeStruct((rows // seq_len, 2, dff), F32)
    else:
        last_spec = pl.BlockSpec((SUBLANES, tn), lambda i, j: (i, j))
        last_shape = jax.ShapeDtypeStruct((n_i * SUBLANES, dff), F32)
    return pl.pallas_call(
        functools.partial(_ffnup_kernel, sample=sample, tiles_per_seq=max(seq_len // tm, 1)),
        grid=(n_i, n_j), in_specs=in_specs,
        out_specs=[tile, last_spec], out_shape=[jax.ShapeDtypeStruct((rows, dff), BF16), last_shape],
        scratch_shapes=[] if sample else [pltpu.VMEM((n_j, SUBLANES, tn), F32)],
        compiler_params=_params("arbitrary", "arbitrary"), name="ffnup")(*args)


def _ffndown_kernel(h_ref, w_ref, x_ref, o_ref):
    o_ref[...] = x_ref[...] + _dot(h_ref[...], w_ref[...])


def _ffndown_call(h, w2, x1, *, tm, tn):
    rows, dff = h.shape
    d = w2.shape[1]
    tile = pl.BlockSpec((tm, tn), lambda i, j: (i, j))
    return pl.pallas_call(
        _ffndown_kernel, grid=(rows // tm, d // tn),
        in_specs=[pl.BlockSpec((tm, dff), lambda i, j: (i, 0)), pl.BlockSpec((dff, tn), lambda i, j: (0, j)), tile],
        out_specs=tile, out_shape=jax.ShapeDtypeStruct((rows, d), F32),
        compiler_params=_params("arbitrary", "arbitrary"), name="ffndown")(h, w2, x1)


def _tile(n, want):
    t = min(n, want)
    while n % t or (t % SUBLANES and t != n):
        t -= 1
    return t


def _col_tile(n, want):
    t = min(n, want)
    while n % t or t % LANES:
        t -= LANES
    return t


def _layer(x, states, p, *, sample):
    nseq, seq_len, d = x.shape
    rows = nseq * seq_len
    g = p["dw2"].shape[1]
    gc = d - g
    dff = p["ffn_conv_w"].shape[1]
    npair = g // LANES
    x2d = x.reshape(rows, d)
    big = dict(sample=sample, seq_len=seq_len)
    tm_big = rows if sample else _tile(seq_len, 1024)

    if sample:
        assert seq_len == SUBLANES, "the sample path shifts rows inside 8-row groups"
        shift, wkv, conv, ffn = states
        ext_x = shift
        ext_p = shift.astype(BF16)
        ce1, ce2 = conv[:, 1], conv[:, 0]
        fe1, fe2 = ffn[:, 1], ffn[:, 0]
        s_bd = wkv.reshape(nseq, npair, LANES, HEAD_DIM)
    else:
        ext_x = ext_p = ce1 = ce2 = fe1 = fe2 = s_bd = None

    tm_mix = _tile(rows if sample else seq_len, 256)
    xnb, lw, a, gate, xlast = _mix_call(x2d, ext_x, p, sample=sample, seq_len=seq_len, tm=tm_mix)
    rkv = _rkv_call(xnb, p["w_in"], p["mu_rkv"], ext_p, tm=tm_big, tn=_col_tile(3 * g, 1024), **big)
    tm_conv = _tile(rows if sample else seq_len, 1024)
    ycv, cxlast = _convbr_call(xnb, p["w_in"], p["conv_w"], ce1, ce2, col0=3 * g, tm=tm_conv,
                               tn=_col_tile(gc, 512), **big)
    if sample:
        t_blk, chunk = WKV_SUB, seq_len
    else:
        chunk = WKV_SUB
        t_blk = _tile(seq_len, 4 * WKV_SUB)
    orw, s_new = _wkv_call(rkv, lw, a, gate, p, s_bd, t_blk=t_blk, chunk=chunk,
                           n_pairs=math.gcd(npair, 16), **big)
    tm_e = _tile(rows, 512)
    ocv = _norm_call(ycv, p["conv_norm_g"], BF16, tm=_tile(rows, 512))
    x1 = _outproj_call(orw, ocv, p["w_out"], x2d, tm=_tile(rows, 1024), tn=_col_tile(d, 1024))
    hn = _norm_call(x1, p["norm2_g"], BF16, tm=_tile(rows, 512))
    tm_ffn = rows if sample else _tile(seq_len, 2048)
    h, ulast = _ffnup_call(hn, p["ffn_w1"], p["ffn_w3"], p["ffn_conv_w"], p["ffn_conv_b"], fe1, fe2,
                           tm=tm_ffn, tn=_col_tile(dff, 256), **big)
    x2 = _ffndown_call(h, p["ffn_w2"], x1, tm=tm_e, tn=_col_tile(d, 512))

    def last_rows(arr, tile_rows, k):
        if sample:
            return arr.reshape(nseq, seq_len, -1)[:, seq_len - k:]
        per_seq = seq_len // tile_rows
        return arr.reshape(nseq, per_seq, SUBLANES, -1)[:, -1, SUBLANES - k:]
    new_shift = last_rows(xlast, tm_mix, 1)[:, 0]
    new_conv = last_rows(cxlast, tm_conv, 2)
    new_ffn = ulast if sample else last_rows(ulast, tm_ffn, 2)
    new_wkv = s_new.reshape(nseq, 2 * npair, HEAD_DIM, HEAD_DIM)
    return x2.reshape(nseq, seq_len, d), new_shift, new_wkv, new_conv, new_ffn


def _pad_to(a, axis, mult):
    pad = (-a.shape[axis]) % mult
    if not pad:
        return a
    widths = [(0, 0)] * a.ndim
    widths[axis] = (0, pad)
    return jnp.pad(a, widths)


def kernel(x_prompt, x_sample, state_shift, state_wkv, state_conv, state_ffn, norm1_g, w_in, mu_rkv, mu_lora, decay_w0, decay_w1, decay_w2, aaa_a0, aaa_a1, aaa_a2, gate_g1, gate_g2, k_k, k_a, r_k, ln_x_w, ln_x_b, conv_w, conv_norm_g, w_out, norm2_g, ffn_w1, ffn_conv_w, ffn_conv_b, ffn_w3, ffn_w2, final_norm_g):
    depth = w_in.shape[0]
    d = x_prompt.shape[-1]
    row = lambda v: v.reshape(1, -1).astype(F32)
    yp, ys = x_prompt, x_sample
    outs_p, outs_s = [], []
    for l in range(depth):
        p = dict(
            norm1_g=row(norm1_g[l]), w_in=w_in[l].astype(BF16), mu_rkv=row(mu_rkv[l]), mu_lora=mu_lora[l],
            decay_w0=row(decay_w0[l]), aaa_a0=row(aaa_a0[l]),
            dw1=decay_w1[l].astype(BF16), dw2=decay_w2[l].astype(BF16),
            aw1=aaa_a1[l].astype(BF16), aw2=aaa_a2[l].astype(BF16),
            gw1=_pad_to(gate_g1[l], 1, LANES).astype(BF16), gw2=_pad_to(gate_g2[l], 0, LANES).astype(BF16),
            k_k=row(k_k[l]), k_a=row(k_a[l]), r_k=row(r_k[l]), ln_x_w=row(ln_x_w[l]), ln_x_b=row(ln_x_b[l]),
            conv_w=conv_w[l], conv_norm_g=row(conv_norm_g[l]), w_out=w_out[l].astype(BF16),
            norm2_g=row(norm2_g[l]), ffn_w1=ffn_w1[l], ffn_w3=ffn_w3[l],
            ffn_conv_w=ffn_conv_w[l], ffn_conv_b=row(ffn_conv_b[l]), ffn_w2=ffn_w2[l].astype(BF16))
        yp, *st_p = _layer(yp, None, p, sample=False)
        ys, *st_s = _layer(ys, (state_shift[l], state_wkv[l], state_conv[l], state_ffn[l]), p, sample=True)
        outs_p.append(st_p)
        outs_s.append(st_s)
    fin = row(final_norm_g)
    y_prompt = _norm_call(yp.reshape(-1, d), fin, F32, tm=_tile(yp.shape[0] * yp.shape[1], 512)).reshape(yp.shape)
    y_sample = _norm_call(ys.reshape(-1, d), fin, F32, tm=_tile(ys.shape[0] * ys.shape[1], 512)).reshape(ys.shape)
    stack = lambda outs, k: jnp.stack([o[k] for o in outs])
    return (y_prompt, y_sample,
            stack(outs_p, 0), stack(outs_p, 1), stack(outs_p, 2), stack(outs_p, 3),
            stack(outs_s, 0), stack(outs_s, 1), stack(outs_s, 2), stack(outs_s, 3))
```
